```python
import math
import jax, jax.numpy as jnp
from jax import lax
import numpy as np

D_MODEL = 1024
BATCH = 8
SEQ = 2048
DEPTH = 2
DEC_BATCH = 128
DEC_SEQ = 8
PAST_LEN = 8192
PAGE_SIZE = 128

N_A_LAYERS = DEPTH // 2
N_B_LAYERS = DEPTH - N_A_LAYERS
D_FF = 2816
EXPAND = 2
D_INNER = EXPAND * D_MODEL
SSM_HEAD_DIM = 64
SSM_HEADS = D_INNER // SSM_HEAD_DIM
SSM_GROUPS = 4
SSM_D_STATE = 128
CONV_WIDTH = 4
CONV_DIM = D_INNER + 2 * SSM_GROUPS * SSM_D_STATE
D_IN_PROJ = D_INNER + CONV_DIM + SSM_HEADS
SSD_CHUNK = 128
WINDOW = 128
ATT_HEAD_DIM = 64
N_Q_HEADS = D_MODEL // ATT_HEAD_DIM
N_KV_HEADS = 4
Q_PER_KV = N_Q_HEADS // N_KV_HEADS
NORM_EPS = 1e-5

kernel_name = 'hybrid_ssd_swa_yoco_step'


def rmsnorm(x, g):
    xf = x.astype(jnp.float32)
    y = xf * lax.rsqrt(jnp.mean(xf * xf, axis=-1, keepdims=True) + NORM_EPS)
    return (y * g.astype(jnp.float32)).astype(x.dtype)


def swiglu(h, w_gu, w_down):
    g, u = jnp.split(h @ w_gu, 2, axis=-1)
    return (jax.nn.silu(g) * u) @ w_down


def alibi_slopes():
    i = jnp.arange(1, N_Q_HEADS + 1, dtype=jnp.float32)
    return (2.0 ** (-8.0 * i / N_Q_HEADS)).reshape(N_KV_HEADS, Q_PER_KV)


def ssd_scan(x, dt, a, b_in, c_in, h0):
    bsz, L = x.shape[0], x.shape[1]
    l = SSD_CHUNK if L % SSD_CHUNK == 0 else L
    nc = L // l
    G, R = SSM_GROUPS, SSM_HEADS // SSM_GROUPS
    P, N = SSM_HEAD_DIM, SSM_D_STATE
    x = x.reshape(bsz, nc, l, G, R, P)
    dt = dt.reshape(bsz, nc, l, G, R)
    bm = b_in.reshape(bsz, nc, l, G, N)
    cm = c_in.reshape(bsz, nc, l, G, N)
    acs = jnp.cumsum(dt * a.reshape(G, R), axis=2)
    xdt = x * dt[..., None]
    at = jnp.moveaxis(acs, 2, -1)
    seg = at[..., :, None] - at[..., None, :]
    causal = jnp.tril(jnp.ones((l, l), dtype=bool))
    decay = jnp.exp(jnp.where(causal, seg, -jnp.inf))
    cb = jnp.einsum('bctgn,bcsgn->bcgts', cm, bm)
    y_diag = jnp.einsum('bcgrts,bcsgrp->bctgrp', cb[:, :, :, None] * decay, xdt)
    decay_s = jnp.exp(acs[:, :, -1:] - acs)
    states = jnp.einsum('bcsgn,bcsgr,bcsgrp->bcgrpn', bm, decay_s, xdt)
    chunk_decay = jnp.exp(acs[:, :, -1])

    def step(h, inp):
        s_c, d_c = inp
        return h * d_c[..., None, None] + s_c, h

    h_fin, h_in = lax.scan(step, h0.reshape(bsz, G, R, P, N),
                           (jnp.moveaxis(states, 1, 0), jnp.moveaxis(chunk_decay, 1, 0)))
    h_in = jnp.moveaxis(h_in, 0, 1)
    y_off = jnp.einsum('bctgn,bcgrpn,bctgr->bctgrp', cm, h_in, jnp.exp(acs))
    y = (y_diag + y_off).reshape(bsz, L, SSM_HEADS, P)
    return y, h_fin.reshape(bsz, SSM_HEADS, P, N)


def mamba2_mixer(h, w_in, conv_w, conv_b, dt_bias, a_log, d_skip, gate_norm, w_out, ssm0, conv0):
    bsz, L = h.shape[0], h.shape[1]
    f32 = jnp.float32
    proj = h @ w_in
    z = proj[..., :D_INNER]
    xbc = proj[..., D_INNER:D_INNER + CONV_DIM]
    dt_raw = proj[..., D_INNER + CONV_DIM:]
    xin = jnp.concatenate([conv0.astype(xbc.dtype), xbc], axis=1)
    conv_new = xin[:, L:]
    xc = conv_b + sum(xin[:, k:k + L] * conv_w[k] for k in range(CONV_WIDTH))
    xc = jax.nn.silu(xc)
    gn = SSM_GROUPS * SSM_D_STATE
    xs = xc[..., :D_INNER].reshape(bsz, L, SSM_HEADS, SSM_HEAD_DIM).astype(f32)
    bm = xc[..., D_INNER:D_INNER + gn].reshape(bsz, L, SSM_GROUPS, SSM_D_STATE).astype(f32)
    cm = xc[..., D_INNER + gn:].reshape(bsz, L, SSM_GROUPS, SSM_D_STATE).astype(f32)
    dt = jax.nn.softplus(dt_raw.astype(f32) + dt_bias.astype(f32))
    a = -jnp.exp(a_log.astype(f32))
    y, h_fin = ssd_scan(xs, dt, a, bm, cm, ssm0.astype(f32))
    y = y + d_skip.astype(f32)[:, None] * xs
    y = y.reshape(bsz, L, D_INNER) * jax.nn.silu(z.astype(f32))
    yg = y.reshape(bsz, L, SSM_GROUPS, D_INNER // SSM_GROUPS)
    yg = yg * lax.rsqrt(jnp.mean(yg * yg, axis=-1, keepdims=True) + NORM_EPS)
    y = yg.reshape(bsz, L, D_INNER) * gate_norm.astype(f32)
    out = y.astype(h.dtype) @ w_out
    return out, h_fin.astype(ssm0.dtype), conv_new


def sink_alibi_attention(q, k, v, dist, valid, sinks):
    s = jnp.einsum('...qkgd,...skd->...kgqs', q, k).astype(jnp.float32) * (ATT_HEAD_DIM ** -0.5)
    s = s - alibi_slopes()[:, :, None, None] * dist
    s = jnp.where(valid, s, -jnp.inf)
    sink_col = jnp.broadcast_to(sinks.astype(jnp.float32).reshape(N_KV_HEADS, Q_PER_KV)[:, :, None, None],
                                s.shape[:-1] + (1,))
    p = jax.nn.softmax(jnp.concatenate([s, sink_col], axis=-1), axis=-1)[..., :-1]
    return jnp.einsum('...kgqs,...skd->...qkgd', p.astype(v.dtype), v)


def swa_prompt(q, k, v, sinks):
    bsz, L = q.shape[0], q.shape[1]
    nb = L // WINDOW
    qb = q.reshape(bsz, nb, WINDOW, N_KV_HEADS, Q_PER_KV, ATT_HEAD_DIM)

    def band(t):
        tb = t.reshape(bsz, nb, WINDOW, N_KV_HEADS, ATT_HEAD_DIM)
        prev = jnp.concatenate([jnp.zeros_like(tb[:, :1]), tb[:, :-1]], axis=1)
        return jnp.concatenate([prev, tb], axis=2)

    qi = WINDOW + jnp.arange(WINDOW)
    kj = jnp.arange(2 * WINDOW)
    dist = qi[:, None] - kj[None, :]
    kabs = jnp.arange(nb)[:, None] * WINDOW - WINDOW + kj[None, :]
    valid = (dist >= 0) & (dist < WINDOW) & (kabs[:, None, None, None, :] >= 0)
    o = sink_alibi_attention(qb, band(k), band(v), dist.astype(jnp.float32), valid, sinks)
    return o.reshape(bsz, L, N_Q_HEADS * ATT_HEAD_DIM)


def swa_sample(q, k_all, v_all, sinks):
    bsz, T = q.shape[0], q.shape[1]
    Tk = k_all.shape[1]
    qb = q.reshape(bsz, T, N_KV_HEADS, Q_PER_KV, ATT_HEAD_DIM)
    qi = WINDOW + jnp.arange(T)
    kj = jnp.arange(Tk)
    dist = qi[:, None] - kj[None, :]
    kabs = PAST_LEN - WINDOW + kj
    valid = (dist >= 0) & (dist < WINDOW) & (kabs[None, :] >= 0)
    o = sink_alibi_attention(qb, k_all, v_all, dist.astype(jnp.float32), valid, sinks)
    return o.reshape(bsz, T, N_Q_HEADS * ATT_HEAD_DIM)


def trunk(x, ssm_in, conv_in, k_buf, v_buf, p):
    is_prompt = k_buf is None
    bsz, L = x.shape[0], x.shape[1]
    ssm_out, conv_out = [], []
    k_all = v_all = None
    for layer in range(DEPTH):
        if layer == N_A_LAYERS:
            kv = rmsnorm(x, p['kv_norm']) @ p['w_kv']
            nkv = N_KV_HEADS * ATT_HEAD_DIM
            k_new = kv[..., :nkv].reshape(bsz, L, N_KV_HEADS, ATT_HEAD_DIM)
            v_new = kv[..., nkv:].reshape(bsz, L, N_KV_HEADS, ATT_HEAD_DIM)
            if is_prompt:
                k_all, v_all = k_new, v_new
            else:
                k_all = jnp.concatenate([k_buf.astype(k_new.dtype), k_new], axis=1)
                v_all = jnp.concatenate([v_buf.astype(v_new.dtype), v_new], axis=1)
        x = x + 0.5 * swiglu(rmsnorm(x, p['ffn1_norm'][layer]), p['ffn1_w_gu'][layer], p['ffn1_w_down'][layer])
        h = rmsnorm(x, p['mix_norm'][layer])
        if layer < N_A_LAYERS:
            i = layer
            out, s_f, c_f = mamba2_mixer(h, p['ssm_w_in'][i], p['ssm_conv_w'][i], p['ssm_conv_b'][i],
                                         p['ssm_dt_bias'][i], p['ssm_a_log'][i], p['ssm_d'][i],
                                         p['ssm_gate_norm'][i], p['ssm_w_out'][i], ssm_in[i], conv_in[i])
            ssm_out.append(s_f)
            conv_out.append(c_f)
        else:
            j = layer - N_A_LAYERS
            q = (h @ p['attn_w_q'][j]).reshape(bsz, L, N_Q_HEADS, ATT_HEAD_DIM)
            if is_prompt:
                o = swa_prompt(q, k_all, v_all, p['attn_sinks'][j])
            else:
                o = swa_sample(q, k_all, v_all, p['attn_sinks'][j])
            out = o @ p['attn_w_o'][j]
        x = x + out
        x = x + 0.5 * swiglu(rmsnorm(x, p['ffn2_norm'][layer]), p['ffn2_w_gu'][layer], p['ffn2_w_down'][layer])
    y = rmsnorm(x, p['final_norm'])
    return y, jnp.stack(ssm_out), jnp.stack(conv_out), k_all[:, -WINDOW:], v_all[:, -WINDOW:]


def setup_inputs(seed: int = 0) -> dict:
    key = jax.random.key(seed)
    ks = iter(jax.random.split(key, 32))
    f32 = jnp.float32

    def nrm(shape, scale):
        return jax.random.normal(next(ks), shape, f32) * scale

    def gain(shape):
        return 1.0 + 0.05 * jax.random.normal(next(ks), shape, f32)

    dt0 = jnp.exp(jax.random.uniform(next(ks), (N_A_LAYERS, SSM_HEADS), f32,
                                     math.log(1e-3), math.log(1e-1)))
    return {
        'x_prompt': nrm((BATCH, SEQ, D_MODEL), 1.0),
        'x_sample': nrm((DEC_BATCH, DEC_SEQ, D_MODEL), 1.0),
        'state_ssm': nrm((N_A_LAYERS, DEC_BATCH, SSM_HEADS, SSM_HEAD_DIM, SSM_D_STATE), 0.1),
        'state_conv': nrm((N_A_LAYERS, DEC_BATCH, CONV_WIDTH - 1, CONV_DIM), 1.0),
        'cache_k_win': nrm((DEC_BATCH, WINDOW, N_KV_HEADS, ATT_HEAD_DIM), 1.0),
        'cache_v_win': nrm((DEC_BATCH, WINDOW, N_KV_HEADS, ATT_HEAD_DIM), 1.0),
        'ffn1_norm': gain((DEPTH, D_MODEL)),
        'ffn1_w_gu': nrm((DEPTH, D_MODEL, 2 * D_FF), D_MODEL ** -0.5),
        'ffn1_w_down': nrm((DEPTH, D_FF, D_MODEL), D_FF ** -0.5),
        'mix_norm': gain((DEPTH, D_MODEL)),
        'ffn2_norm': gain((DEPTH, D_MODEL)),
        'ffn2_w_gu': nrm((DEPTH, D_MODEL, 2 * D_FF), D_MODEL ** -0.5),
        'ffn2_w_down': nrm((DEPTH, D_FF, D_MODEL), D_FF ** -0.5),
        'ssm_w_in': nrm((N_A_LAYERS, D_MODEL, D_IN_PROJ), D_MODEL ** -0.5),
        'ssm_conv_w': jax.random.uniform(next(ks), (N_A_LAYERS, CONV_WIDTH, CONV_DIM), f32, -0.5, 0.5),
        'ssm_conv_b': nrm((N_A_LAYERS, CONV_DIM), 0.02),
        'ssm_dt_bias': dt0 + jnp.log(-jnp.expm1(-dt0)),
        'ssm_a_log': jnp.log(jax.random.uniform(next(ks), (N_A_LAYERS, SSM_HEADS), f32, 1.0, 16.0)),
        'ssm_d': gain((N_A_LAYERS, SSM_HEADS)),
        'ssm_gate_norm': gain((N_A_LAYERS, D_INNER)),
        'ssm_w_out': nrm((N_A_LAYERS, D_INNER, D_MODEL), D_INNER ** -0.5),
        'kv_norm': gain((D_MODEL,)),
        'w_kv': nrm((D_MODEL, 2 * N_KV_HEADS * ATT_HEAD_DIM), D_MODEL ** -0.5),
        'attn_w_q': nrm((N_B_LAYERS, D_MODEL, N_Q_HEADS * ATT_HEAD_DIM), D_MODEL ** -0.5),
        'attn_sinks': nrm((N_B_LAYERS, N_Q_HEADS), 0.5),
        'attn_w_o': nrm((N_B_LAYERS, N_Q_HEADS * ATT_HEAD_DIM, D_MODEL), (N_Q_HEADS * ATT_HEAD_DIM) ** -0.5),
        'final_norm': gain((D_MODEL,)),
    }


def reference(x_prompt, x_sample, state_ssm, state_conv, cache_k_win, cache_v_win,
              ffn1_norm, ffn1_w_gu, ffn1_w_down, mix_norm, ffn2_norm, ffn2_w_gu, ffn2_w_down,
              ssm_w_in, ssm_conv_w, ssm_conv_b, ssm_dt_bias, ssm_a_log, ssm_d, ssm_gate_norm, ssm_w_out,
              kv_norm, w_kv, attn_w_q, attn_sinks, attn_w_o, final_norm):
    p = dict(ffn1_norm=ffn1_norm, ffn1_w_gu=ffn1_w_gu, ffn1_w_down=ffn1_w_down, mix_norm=mix_norm,
             ffn2_norm=ffn2_norm, ffn2_w_gu=ffn2_w_gu, ffn2_w_down=ffn2_w_down,
             ssm_w_in=ssm_w_in, ssm_conv_w=ssm_conv_w, ssm_conv_b=ssm_conv_b, ssm_dt_bias=ssm_dt_bias,
             ssm_a_log=ssm_a_log, ssm_d=ssm_d, ssm_gate_norm=ssm_gate_norm, ssm_w_out=ssm_w_out,
             kv_norm=kv_norm, w_kv=w_kv, attn_w_q=attn_w_q, attn_sinks=attn_sinks, attn_w_o=attn_w_o,
             final_norm=final_norm)
    bp = x_prompt.shape[0]
    ssm0 = jnp.zeros((N_A_LAYERS, bp, SSM_HEADS, SSM_HEAD_DIM, SSM_D_STATE), x_prompt.dtype)
    conv0 = jnp.zeros((N_A_LAYERS, bp, CONV_WIDTH - 1, CONV_DIM), x_prompt.dtype)
    y_prompt, ssm_p, conv_p, kw_p, vw_p = trunk(x_prompt, ssm0, conv0, None, None, p)
    y_sample, ssm_s, conv_s, kw_s, vw_s = trunk(x_sample, state_ssm, state_conv, cache_k_win, cache_v_win, p)
    return (y_prompt, y_sample, ssm_p, conv_p, kw_p, vw_p, ssm_s, conv_s, kw_s, vw_s)
```

```python
import functools
import math

import jax
import jax.numpy as jnp
from jax import lax
from jax.experimental import pallas as pl
from jax.experimental.pallas import tpu as pltpu

F32 = jnp.float32
BF16 = jnp.bfloat16

D_MODEL = 1024
D_FF = 2816
D_INNER = 2048
SSM_HEAD_DIM = 64
SSM_HEADS = 32
SSM_GROUPS = 4
HEADS_PER_GROUP = SSM_HEADS // SSM_GROUPS
SSM_D_STATE = 128
GROUP_WIDTH = D_INNER // SSM_GROUPS
CONV_WIDTH = 4
CONV_DIM = D_INNER + 2 * SSM_GROUPS * SSM_D_STATE
SSD_CHUNK = 128
WINDOW = 128
ATT_HEAD_DIM = 64
N_Q_HEADS = 16
N_KV_HEADS = 4
Q_PER_KV = N_Q_HEADS // N_KV_HEADS
KV_WIDTH = N_KV_HEADS * ATT_HEAD_DIM
PAST_LEN = 8192
NORM_EPS = 1e-5

LANES = 128
SUBLANES = 8
VMEM_LIMIT_BYTES = 56 * 1024 * 1024
FF_CHUNK = 256
ROW_TILE = 1024
PROJ_ROW_TILE = 512
PROJ_COL_CHUNK = 512

_NT = (((1,), (1,)), ((), ()))
_TN = (((0,), (0,)), ((), ()))


def _rms(x, g):
    ms = jnp.mean(x * x, axis=-1, keepdims=True)
    return x * lax.rsqrt(ms + NORM_EPS) * g


def _silu(x):
    return x * (1.0 / (1.0 + jnp.exp(-x)))


def _softplus(x):
    return jnp.maximum(x, 0.0) + jnp.log1p(jnp.exp(-jnp.abs(x)))


def _params(*sem):
    return pltpu.CompilerParams(dimension_semantics=sem, vmem_limit_bytes=VMEM_LIMIT_BYTES)


def _full(shape):
    return pl.BlockSpec(shape, lambda *_: (0,) * len(shape))


def _ffn_kernel(x_ref, g_ref, wgu_ref, wd_ref, fg_ref, o_ref, acc_ref, *, final_norm):
    x = x_ref[...]
    h = _rms(x, g_ref[...]).astype(BF16)
    for j in range(D_FF // FF_CHUNK):
        lo = j * FF_CHUNK
        gate = jnp.dot(h, wgu_ref[:, lo:lo + FF_CHUNK], preferred_element_type=F32)
        up = jnp.dot(h, wgu_ref[:, D_FF + lo:D_FF + lo + FF_CHUNK], preferred_element_type=F32)
        act = (_silu(gate) * up).astype(BF16)
        part = jnp.dot(act, wd_ref[lo:lo + FF_CHUNK, :], preferred_element_type=F32)
        if j == 0:
            acc_ref[...] = part
        else:
            acc_ref[...] += part
    out = x + 0.5 * acc_ref[...]
    if final_norm:
        out = _rms(out, fg_ref[...])
    o_ref[...] = out


def _ffn(x, g, wgu, wd, fg=None):
    t = x.shape[0]
    tm = min(ROW_TILE, t)
    final_norm = fg is not None
    if fg is None:
        fg = g
    return pl.pallas_call(
        functools.partial(_ffn_kernel, final_norm=final_norm),
        grid=(t // tm,),
        in_specs=[pl.BlockSpec((tm, D_MODEL), lambda i: (i, 0)),
                  _full((1, D_MODEL)), _full((D_MODEL, 2 * D_FF)), _full((D_FF, D_MODEL)),
                  _full((1, D_MODEL))],
        out_specs=pl.BlockSpec((tm, D_MODEL), lambda i: (i, 0)),
        out_shape=jax.ShapeDtypeStruct((t, D_MODEL), F32),
        scratch_shapes=[pltpu.VMEM((tm, D_MODEL), F32)],
        compiler_params=_params("parallel"),
        name="ffn",
    )(x, g.reshape(1, D_MODEL), wgu, wd, fg.reshape(1, D_MODEL))


def _norm_proj_kernel(x_ref, g_ref, *refs):
    n = len(refs) // 2
    h = _rms(x_ref[...], g_ref[...]).astype(BF16)
    for w_ref, o_ref in zip(refs[:n], refs[n:]):
        width = w_ref.shape[1]
        step = min(PROJ_COL_CHUNK, width)
        for lo in range(0, width, step):
            o_ref[:, lo:lo + step] = jnp.dot(h, w_ref[:, lo:lo + step], preferred_element_type=F32)


def _norm_proj(x, g, weights):
    t = x.shape[0]
    tm = min(PROJ_ROW_TILE, t)
    return pl.pallas_call(
        _norm_proj_kernel,
        grid=(t // tm,),
        in_specs=[pl.BlockSpec((tm, D_MODEL), lambda i: (i, 0)), _full((1, D_MODEL))]
        + [_full(w.shape) for w in weights],
        out_specs=[pl.BlockSpec((tm, w.shape[1]), lambda i: (i, 0)) for w in weights],
        out_shape=[jax.ShapeDtypeStruct((t, w.shape[1]), F32) for w in weights],
        compiler_params=_params("parallel"),
        name="norm_proj",
    )(x, g.reshape(1, D_MODEL), *weights)


def _proj_res_kernel(y_ref, w_ref, x_ref, o_ref):
    o_ref[...] = x_ref[...] + jnp.dot(y_ref[...].astype(BF16), w_ref[...], preferred_element_type=F32)


def _proj_res(y, w, x):
    t, k = y.shape
    tm = min(PROJ_ROW_TILE, t)
    return pl.pallas_call(
        _proj_res_kernel,
        grid=(t // tm,),
        in_specs=[pl.BlockSpec((tm, k), lambda i: (i, 0)), _full(w.shape),
                  pl.BlockSpec((tm, D_MODEL), lambda i: (i, 0))],
        out_specs=pl.BlockSpec((tm, D_MODEL), lambda i: (i, 0)),
        out_shape=jax.ShapeDtypeStruct((t, D_MODEL), F32),
        compiler_params=_params("parallel"),
        name="proj_res",
    )(y, w, x)


def _ssd_kernel(xbc_ref, z_ref, dt_ref, dtt_ref, conv0_ref, ssm0_ref,
                cw_ref, cb_ref, dtb_c_ref, alog_c_ref, dtb_r_ref, alog_r_ref, dskip_ref, gn_ref,
                y_ref, ssm_ref, conv_ref,
                ext_scr, state_scr, *pad_scr, lt, ls, nc):
    c = pl.program_id(1)

    @pl.when(c == 0)
    def _():
        ext_scr[0:SUBLANES, :] = conv0_ref[...]
        state_scr[...] = ssm0_ref[...]

    ext_scr[SUBLANES:SUBLANES + lt, :] = xbc_ref[...]
    taps = [ext_scr[SUBLANES - 3 + k:SUBLANES - 3 + k + lt, :] * cw_ref[k:k + 1, :]
            for k in range(CONV_WIDTH)]
    xc = _silu((((taps[0] + taps[1]) + taps[2]) + taps[3]) + cb_ref[...])
    tail = ext_scr[lt:lt + SUBLANES, :]
    conv_ref[...] = tail
    ext_scr[0:SUBLANES, :] = tail

    def pad_rows(v, scr):
        if ls == lt:
            return v
        scr[...] = jnp.zeros(scr.shape, F32)
        scr[0:lt, :] = v
        return scr[...]

    xs = xc[:, :D_INNER]
    bm = xc[:, D_INNER:D_INNER + GROUP_WIDTH]
    cm = xc[:, D_INNER + GROUP_WIDTH:]

    dt_c = _softplus(dt_ref[...] + dtb_c_ref[...])
    da_c = dt_c * -jnp.exp(alog_c_ref[...])
    dt_r = _softplus(dtt_ref[...] + dtb_r_ref[...])
    da_r = dt_r * -jnp.exp(alog_r_ref[...])

    t_idx = lax.broadcasted_iota(jnp.int32, (lt, ls), 0)
    s_idx = lax.broadcasted_iota(jnp.int32, (lt, ls), 1)
    causal = s_idx <= t_idx
    upper = (lax.broadcasted_iota(jnp.int32, (ls, ls), 0)
             <= lax.broadcasted_iota(jnp.int32, (ls, ls), 1)).astype(F32)
    acs_c = jnp.dot(causal.astype(F32), pad_rows(da_c, pad_scr[0] if pad_scr else None),
                    precision=lax.Precision.HIGHEST, preferred_element_type=F32)
    acs_r = jnp.dot(da_r, upper, precision=lax.Precision.HIGHEST,
                    preferred_element_type=F32)

    last_c = acs_c[lt - 1:lt, :]
    w_c = dt_c * jnp.exp(last_c - acs_c)
    e_c = jnp.exp(acs_c)
    chunk_decay = jnp.exp(jnp.broadcast_to(acs_r[:, lt - 1:lt], (SSM_HEADS, LANES)))

    spread = (lax.shift_right_logical(lax.broadcasted_iota(jnp.int32, (LANES, D_INNER), 1),
                                      int(math.log2(SSM_HEAD_DIM)))
              == lax.broadcasted_iota(jnp.int32, (LANES, D_INNER), 0)).astype(F32)
    w_x = jnp.dot(w_c, spread, precision=lax.Precision.HIGHEST, preferred_element_type=F32)
    e_x = jnp.dot(e_c, spread, precision=lax.Precision.HIGHEST, preferred_element_type=F32)

    xs_s = pad_rows(xs, pad_scr[1] if pad_scr else None)
    bm_s = pad_rows(bm, pad_scr[2] if pad_scr else None)
    xd_s = pad_rows(xs * w_x, pad_scr[3] if pad_scr else None)
    low_half = lax.broadcasted_iota(jnp.int32, (ls, LANES), 1) < SSM_HEAD_DIM

    zg = z_ref[...]
    for g in range(SSM_GROUPS):
        cm_g = cm[:, g * SSM_D_STATE:(g + 1) * SSM_D_STATE].astype(BF16)
        bm_g = bm_s[:, g * SSM_D_STATE:(g + 1) * SSM_D_STATE].astype(BF16)
        cb_g = lax.dot_general(cm_g, bm_g, _NT, preferred_element_type=F32)
        rows = slice(g * GROUP_WIDTH, (g + 1) * GROUP_WIDTH)
        state_g = state_scr[rows, :]
        y_off = lax.dot_general(cm_g, state_g.astype(BF16), _NT, preferred_element_type=F32)
        tiles = []
        for jj in range(HEADS_PER_GROUP // 2):
            tile = g * (HEADS_PER_GROUP // 2) + jj
            cols = slice(tile * LANES, (tile + 1) * LANES)
            mats = []
            for h in (2 * tile, 2 * tile + 1):
                seg = acs_c[:, h:h + 1] - acs_r[h:h + 1, :]
                decay = jnp.exp(jnp.where(causal, seg, -jnp.inf))
                mats.append((cb_g * decay * dt_r[h:h + 1, :]).astype(BF16))
            xp = xs_s[:, cols]
            x_lo = jnp.where(low_half, xp, 0.0).astype(BF16)
            x_hi = jnp.where(low_half, 0.0, xp).astype(BF16)
            y_diag = (jnp.dot(mats[0], x_lo, preferred_element_type=F32)
                      + jnp.dot(mats[1], x_hi, preferred_element_type=F32))
            tiles.append(y_diag + y_off[:, jj * LANES:(jj + 1) * LANES] * e_x[:, cols]
                         + dskip_ref[:, cols] * xs[:, cols])
        yg = jnp.concatenate(tiles, axis=1) * _silu(zg[:, rows])
        ms = jnp.mean(yg * yg, axis=-1, keepdims=True)
        y_ref[:, rows] = yg * lax.rsqrt(ms + NORM_EPS) * gn_ref[:, rows]

        upd = lax.dot_general(xd_s[:, rows].astype(BF16), bm_g, _TN, preferred_element_type=F32)
        for hh in range(HEADS_PER_GROUP):
            h = g * HEADS_PER_GROUP + hh
            hrows = slice(h * SSM_HEAD_DIM, (h + 1) * SSM_HEAD_DIM)
            state_scr[hrows, :] = (state_scr[hrows, :] * chunk_decay[h:h + 1, :]
                                   + upd[hh * SSM_HEAD_DIM:(hh + 1) * SSM_HEAD_DIM, :])

    @pl.when(c == nc - 1)
    def _():
        ssm_ref[...] = state_scr[...]


def _ssd(xbc, z, dt, conv0, ssm0, p, bsz, seq):
    lt = SSD_CHUNK if seq % SSD_CHUNK == 0 else seq
    ls = max(lt, LANES)
    nc = seq // lt
    xbc = xbc.reshape(bsz, seq, CONV_DIM)
    z = z.reshape(bsz, seq, D_INNER)
    dt = dt.reshape(bsz, seq, LANES)
    dtt = jnp.swapaxes(dt[:, :, :SSM_HEADS], 1, 2)
    if ls != lt:
        dtt = jnp.pad(dtt, ((0, 0), (0, 0), (0, ls - lt)))
    conv0 = jnp.pad(conv0, ((0, 0), (SUBLANES - (CONV_WIDTH - 1), 0), (0, 0)))
    ssm0 = ssm0.reshape(bsz, D_INNER, SSM_D_STATE)

    def head_cols(v):
        return jnp.pad(v.astype(F32), (0, LANES - SSM_HEADS)).reshape(1, LANES)

    def head_rows(v):
        return jnp.broadcast_to(v.astype(F32)[:, None], (SSM_HEADS, ls))

    pad_scratch = [] if ls == lt else [pltpu.VMEM((ls, LANES), F32), pltpu.VMEM((ls, D_INNER), F32),
                                       pltpu.VMEM((ls, GROUP_WIDTH), F32), pltpu.VMEM((ls, D_INNER), F32)]
    y, ssm, conv = pl.pallas_call(
        functools.partial(_ssd_kernel, lt=lt, ls=ls, nc=nc),
        grid=(bsz, nc),
        in_specs=[pl.BlockSpec((None, lt, CONV_DIM), lambda b, c: (b, c, 0)),
                  pl.BlockSpec((None, lt, D_INNER), lambda b, c: (b, c, 0)),
                  pl.BlockSpec((None, lt, LANES), lambda b, c: (b, c, 0)),
                  pl.BlockSpec((None, SSM_HEADS, ls), lambda b, c: (b, 0, c)),
                  pl.BlockSpec((None, SUBLANES, CONV_DIM), lambda b, c: (b, 0, 0)),
                  pl.BlockSpec((None, D_INNER, SSM_D_STATE), lambda b, c: (b, 0, 0)),
                  _full((CONV_WIDTH, CONV_DIM)), _full((1, CONV_DIM)),
                  _full((1, LANES)), _full((1, LANES)),
                  _full((SSM_HEADS, ls)), _full((SSM_HEADS, ls)),
                  _full((1, D_INNER)), _full((1, D_INNER))],
        out_specs=[pl.BlockSpec((None, lt, D_INNER), lambda b, c: (b, c, 0)),
                   pl.BlockSpec((None, D_INNER, SSM_D_STATE), lambda b, c: (b, 0, 0)),
                   pl.BlockSpec((None, SUBLANES, CONV_DIM), lambda b, c: (b, 0, 0))],
        out_shape=[jax.ShapeDtypeStruct((bsz, seq, D_INNER), F32),
                   jax.ShapeDtypeStruct((bsz, D_INNER, SSM_D_STATE), F32),
                   jax.ShapeDtypeStruct((bsz, SUBLANES, CONV_DIM), F32)],
        scratch_shapes=[pltpu.VMEM((SUBLANES + lt, CONV_DIM), F32),
                        pltpu.VMEM((D_INNER, SSM_D_STATE), F32)] + pad_scratch,
        compiler_params=_params("parallel", "arbitrary"),
        name="ssd",
    )(xbc, z, dt, dtt, conv0, ssm0,
      p["conv_w"], p["conv_b"].reshape(1, CONV_DIM),
      head_cols(p["dt_bias"]), head_cols(p["a_log"]), head_rows(p["dt_bias"]), head_rows(p["a_log"]),
      jnp.repeat(p["d_skip"].astype(F32), SSM_HEAD_DIM).reshape(1, D_INNER),
      p["gate_norm"].reshape(1, D_INNER))
    return (y.reshape(bsz * seq, D_INNER),
            ssm.reshape(bsz, SSM_HEADS, SSM_HEAD_DIM, SSM_D_STATE),
            conv[:, SUBLANES - (CONV_WIDTH - 1):, :])


def _alibi_slope(head):
    return 2.0 ** (-8.0 * (head + 1) / N_Q_HEADS)


def _attn_kernel(sink_ref, q_ref, kp_ref, kc_ref, vp_ref, vc_ref, o_ref, k_scr, v_scr, *, tq, base):
    i = pl.program_id(1)
    if tq < WINDOW:
        k_scr[...] = jnp.zeros(k_scr.shape, F32)
        v_scr[...] = jnp.zeros(v_scr.shape, F32)
    k_scr[0:WINDOW, :] = kp_ref[...]
    v_scr[0:WINDOW, :] = vp_ref[...]
    k_scr[WINDOW:WINDOW + tq, :] = kc_ref[...]
    v_scr[WINDOW:WINDOW + tq, :] = vc_ref[...]

    nkeys = 2 * WINDOW
    rows = lax.broadcasted_iota(jnp.int32, (Q_PER_KV * tq, nkeys), 0)
    kj = lax.broadcasted_iota(jnp.int32, (Q_PER_KV * tq, nkeys), 1)
    t = rows & (tq - 1)
    sub = lax.shift_right_logical(rows, int(math.log2(tq)))
    dist = WINDOW + t - kj
    kabs = base + (i - 1) * WINDOW + kj
    valid = (dist >= 0) & (dist < WINDOW) & (kabs >= 0)
    distf = dist.astype(F32)
    low_half = lax.broadcasted_iota(jnp.int32, (nkeys, LANES), 1) < ATT_HEAD_DIM

    def pick(vals):
        return jnp.where(sub == 0, vals[0], jnp.where(sub == 1, vals[1],
                                                      jnp.where(sub == 2, vals[2], vals[3])))

    for j in range(N_KV_HEADS):
        cols = slice((j // 2) * LANES, (j // 2 + 1) * LANES)
        kt = k_scr[:, cols]
        vt = v_scr[:, cols]
        if j % 2 == 0:
            k_lo = jnp.where(low_half, kt, 0.0)
            v_lo = jnp.where(low_half, vt, 0.0)
        else:
            k_lo = pltpu.roll(jnp.where(low_half, 0.0, kt), ATT_HEAD_DIM, 1)
            v_lo = pltpu.roll(jnp.where(low_half, 0.0, vt), ATT_HEAD_DIM, 1)
        q0 = q_ref[:, (2 * j) * LANES:(2 * j + 1) * LANES] * (ATT_HEAD_DIM ** -0.5)
        q1 = q_ref[:, (2 * j + 1) * LANES:(2 * j + 2) * LANES] * (ATT_HEAD_DIM ** -0.5)
        qs = jnp.concatenate([q0, pltpu.roll(q0, ATT_HEAD_DIM, 1),
                              q1, pltpu.roll(q1, ATT_HEAD_DIM, 1)], axis=0)
        s = lax.dot_general(qs.astype(BF16), k_lo.astype(BF16), _NT, preferred_element_type=F32)
        slope = pick([_alibi_slope(Q_PER_KV * j + g) for g in range(Q_PER_KV)])
        sink = pick([sink_ref[Q_PER_KV * j + g] for g in range(Q_PER_KV)])[:, 0:1]
        s = jnp.where(valid, s - slope * distf, -jnp.inf)
        mx = jnp.maximum(jnp.max(s, axis=-1, keepdims=True), sink)
        e = jnp.exp(s - mx)
        den = jnp.sum(e, axis=-1, keepdims=True) + jnp.exp(sink - mx)
        prob = (e * (1.0 / den)).astype(BF16)
        o = jnp.dot(prob, v_lo.astype(BF16), preferred_element_type=F32)
        o_ref[:, (2 * j) * LANES:(2 * j + 1) * LANES] = (
            o[0:tq] + pltpu.roll(o[tq:2 * tq], ATT_HEAD_DIM, 1))
        o_ref[:, (2 * j + 1) * LANES:(2 * j + 2) * LANES] = (
            o[2 * tq:3 * tq] + pltpu.roll(o[3 * tq:4 * tq], ATT_HEAD_DIM, 1))


def _attention(q, k_prev, k_cur, v_prev, v_cur, sinks, *, tq, base, prev_map, cur_map, grid):
    bsz, seq = q.shape[0], q.shape[1]
    return pl.pallas_call(
        functools.partial(_attn_kernel, tq=tq, base=base),
        grid=grid,
        in_specs=[pl.BlockSpec(memory_space=pltpu.SMEM),
                  pl.BlockSpec((None, tq, D_MODEL), lambda b, i: (b, i, 0)),
                  pl.BlockSpec((None, WINDOW, KV_WIDTH), prev_map),
                  pl.BlockSpec((None, tq, KV_WIDTH), cur_map),
                  pl.BlockSpec((None, WINDOW, KV_WIDTH), prev_map),
                  pl.BlockSpec((None, tq, KV_WIDTH), cur_map)],
        out_specs=pl.BlockSpec((None, tq, D_MODEL), lambda b, i: (b, i, 0)),
        out_shape=jax.ShapeDtypeStruct((bsz, seq, D_MODEL), F32),
        scratch_shapes=[pltpu.VMEM((2 * WINDOW, KV_WIDTH), F32), pltpu.VMEM((2 * WINDOW, KV_WIDTH), F32)],
        compiler_params=_params("parallel", "arbitrary"),
        name="swa",
    )(sinks.astype(F32), q, k_prev, k_cur, v_prev, v_cur)


def _trunk(x, ssm_in, conv_in, k_buf, v_buf, w):
    bsz, seq = x.shape[0], x.shape[1]
    is_prompt = k_buf is None
    x = x.reshape(bsz * seq, D_MODEL)

    x = _ffn(x, w["ffn1_norm"][0], w["ffn1_w_gu"][0], w["ffn1_w_down"][0])
    z, xbc, dt = _norm_proj(x, w["mix_norm"][0], [w["ssm_w_z"], w["ssm_w_xbc"], w["ssm_w_dt"]])
    y, ssm_out, conv_out = _ssd(xbc, z, dt, conv_in, ssm_in, w["ssm"], bsz, seq)
    x = _proj_res(y, w["ssm_w_out"], x)
    x = _ffn(x, w["ffn2_norm"][0], w["ffn2_w_gu"][0], w["ffn2_w_down"][0])

    k_new, v_new = _norm_proj(x, w["kv_norm"], [w["w_k"], w["w_v"]])
    k_new = k_new.reshape(bsz, seq, KV_WIDTH)
    v_new = v_new.reshape(bsz, seq, KV_WIDTH)

    x = _ffn(x, w["ffn1_norm"][1], w["ffn1_w_gu"][1], w["ffn1_w_down"][1])
    (q,) = _norm_proj(x, w["mix_norm"][1], [w["attn_w_q"]])
    q = q.reshape(bsz, seq, D_MODEL)
    if is_prompt:
        o = _attention(q, k_new, k_new, v_new, v_new, w["attn_sinks"], tq=WINDOW, base=0,
                       prev_map=lambda b, i: (b, jnp.maximum(i - 1, 0), 0),
                       cur_map=lambda b, i: (b, i, 0), grid=(bsz, seq // WINDOW))
        k_win, v_win = k_new[:, -WINDOW:], v_new[:, -WINDOW:]
    else:
        k_buf = k_buf.reshape(bsz, WINDOW, KV_WIDTH)
        v_buf = v_buf.reshape(bsz, WINDOW, KV_WIDTH)
        o = _attention(q, k_buf, k_new, v_buf, v_new, w["attn_sinks"], tq=seq, base=PAST_LEN,
                       prev_map=lambda b, i: (b, 0, 0), cur_map=lambda b, i: (b, 0, 0),
                       grid=(bsz, 1))
        k_win = jnp.concatenate([k_buf, k_new], axis=1)[:, -WINDOW:]
        v_win = jnp.concatenate([v_buf, v_new], axis=1)[:, -WINDOW:]
    x = _proj_res(o.reshape(bsz * seq, D_MODEL), w["attn_w_o"], x)
    y = _ffn(x, w["ffn2_norm"][1], w["ffn2_w_gu"][1], w["ffn2_w_down"][1], fg=w["final_norm"])

    return (y.reshape(bsz, seq, D_MODEL), ssm_out[None], conv_out[None],
            k_win.reshape(bsz, WINDOW, N_KV_HEADS, ATT_HEAD_DIM),
            v_win.reshape(bsz, WINDOW, N_KV_HEADS, ATT_HEAD_DIM))


def kernel(x_prompt, x_sample, state_ssm, state_conv, cache_k_win, cache_v_win,
           ffn1_norm, ffn1_w_gu, ffn1_w_down, mix_norm, ffn2_norm, ffn2_w_gu, ffn2_w_down,
           ssm_w_in, ssm_conv_w, ssm_conv_b, ssm_dt_bias, ssm_a_log, ssm_d, ssm_gate_norm, ssm_w_out,
           kv_norm, w_kv, attn_w_q, attn_sinks, attn_w_o, final_norm):
    w_in = ssm_w_in[0]
    w_dt = jnp.pad(w_in[:, D_INNER + CONV_DIM:], ((0, 0), (0, LANES - SSM_HEADS)))
    w = dict(
        ffn1_norm=ffn1_norm, ffn2_norm=ffn2_norm, mix_norm=mix_norm, kv_norm=kv_norm,
        final_norm=final_norm, attn_sinks=attn_sinks[0],
        ffn1_w_gu=ffn1_w_gu.astype(BF16), ffn1_w_down=ffn1_w_down.astype(BF16),
        ffn2_w_gu=ffn2_w_gu.astype(BF16), ffn2_w_down=ffn2_w_down.astype(BF16),
        ssm_w_z=w_in[:, :D_INNER].astype(BF16),
        ssm_w_xbc=w_in[:, D_INNER:D_INNER + CONV_DIM].astype(BF16),
        ssm_w_dt=w_dt.astype(BF16),
        ssm_w_out=ssm_w_out[0].astype(BF16),
        w_k=w_kv[:, :KV_WIDTH].astype(BF16), w_v=w_kv[:, KV_WIDTH:].astype(BF16),
        attn_w_q=attn_w_q[0].astype(BF16), attn_w_o=attn_w_o[0].astype(BF16),
        ssm=dict(conv_w=ssm_conv_w[0], conv_b=ssm_conv_b[0], dt_bias=ssm_dt_bias[0],
                 a_log=ssm_a_log[0], d_skip=ssm_d[0], gate_norm=ssm_gate_norm[0]),
    )
    bp = x_prompt.shape[0]
    ssm0 = jnp.zeros((bp, SSM_HEADS, SSM_HEAD_DIM, SSM_D_STATE), F32)
    conv0 = jnp.zeros((bp, CONV_WIDTH - 1, CONV_DIM), F32)
    y_p, ssm_p, conv_p, kw_p, vw_p = _trunk(x_prompt, ssm0, conv0, None, None, w)
    y_s, ssm_s, conv_s, kw_s, vw_s = _trunk(x_sample, state_ssm[0], state_conv[0],
                                            cache_k_win, cache_v_win, w)
    return (y_p, y_s, ssm_p, conv_p, kw_p, vw_p, ssm_s, conv_s, kw_s, vw_s)
```

```python
import functools
import math

import jax
import jax.numpy as jnp
from jax import lax
from jax.experimental import pallas as pl
from jax.experimental.pallas import tpu as pltpu

F32 = jnp.float32
BF16 = jnp.bfloat16

D_MODEL = 1024
D_FF = 2816
D_INNER = 2048
SSM_HEAD_DIM = 64
SSM_HEADS = 32
SSM_GROUPS = 4
HEADS_PER_GROUP = SSM_HEADS // SSM_GROUPS
SSM_D_STATE = 128
GROUP_WIDTH = D_INNER // SSM_GROUPS
CONV_WIDTH = 4
CONV_DIM = D_INNER + 2 * SSM_GROUPS * SSM_D_STATE
SSD_CHUNK = 128
WINDOW = 128
ATT_HEAD_DIM = 64
N_Q_HEADS = 16
N_KV_HEADS = 4
Q_PER_KV = N_Q_HEADS // N_KV_HEADS
KV_WIDTH = N_KV_HEADS * ATT_HEAD_DIM
PAST_LEN = 8192
NORM_EPS = 1e-5
LOG2E = 1.0 / math.log(2.0)

LANES = 128
SUBLANES = 8
VMEM_LIMIT_BYTES = 56 * 1024 * 1024
FF_CHUNK = 256
ROW_TILE = 1024
PROJ_ROW_TILE = 512
PROJ_COL_CHUNK = 512

_NT = (((1,), (1,)), ((), ()))
_TN = (((0,), (0,)), ((), ()))


def _rms(x, g):
    ms = jnp.mean(x * x, axis=-1, keepdims=True)
    return x * lax.rsqrt(ms + NORM_EPS) * g


def _silu(x):
    return x * (1.0 / (1.0 + jnp.exp(-x)))


def _softplus(x):
    return jnp.maximum(x, 0.0) + jnp.log1p(jnp.exp(-jnp.abs(x)))


def _params(*sem):
    return pltpu.CompilerParams(dimension_semantics=sem, vmem_limit_bytes=VMEM_LIMIT_BYTES)


def _full(shape):
    return pl.BlockSpec(shape, lambda *_: (0,) * len(shape))


def _ffn_kernel(x_ref, g_ref, wgu_ref, wd_ref, fg_ref, o_ref, acc_ref, *, final_norm):
    x = x_ref[...]
    h = _rms(x, g_ref[...]).astype(BF16)
    for j in range(D_FF // FF_CHUNK):
        lo = j * FF_CHUNK
        gate = jnp.dot(h, wgu_ref[:, lo:lo + FF_CHUNK], preferred_element_type=F32)
        up = jnp.dot(h, wgu_ref[:, D_FF + lo:D_FF + lo + FF_CHUNK], preferred_element_type=F32)
        act = (_silu(gate) * up).astype(BF16)
        part = jnp.dot(act, wd_ref[lo:lo + FF_CHUNK, :], preferred_element_type=F32)
        if j == 0:
            acc_ref[...] = part
        else:
            acc_ref[...] += part
    out = x + 0.5 * acc_ref[...]
    if final_norm:
        out = _rms(out, fg_ref[...])
    o_ref[...] = out


def _ffn(x, g, wgu, wd, fg=None):
    t = x.shape[0]
    tm = min(ROW_TILE, t)
    final_norm = fg is not None
    if fg is None:
        fg = g
    return pl.pallas_call(
        functools.partial(_ffn_kernel, final_norm=final_norm),
        grid=(t // tm,),
        in_specs=[pl.BlockSpec((tm, D_MODEL), lambda i: (i, 0)),
                  _full((1, D_MODEL)), _full((D_MODEL, 2 * D_FF)), _full((D_FF, D_MODEL)),
                  _full((1, D_MODEL))],
        out_specs=pl.BlockSpec((tm, D_MODEL), lambda i: (i, 0)),
        out_shape=jax.ShapeDtypeStruct((t, D_MODEL), F32),
        scratch_shapes=[pltpu.VMEM((tm, D_MODEL), F32)],
        compiler_params=_params("parallel"),
        name="ffn",
    )(x, g.reshape(1, D_MODEL), wgu, wd, fg.reshape(1, D_MODEL))


def _norm_proj_kernel(x_ref, g_ref, *refs):
    n = len(refs) // 2
    h = _rms(x_ref[...], g_ref[...]).astype(BF16)
    for w_ref, o_ref in zip(refs[:n], refs[n:]):
        width = w_ref.shape[1]
        step = min(PROJ_COL_CHUNK, width)
        for lo in range(0, width, step):
            o_ref[:, lo:lo + step] = jnp.dot(h, w_ref[:, lo:lo + step], preferred_element_type=F32)


def _norm_proj(x, g, weights):
    t = x.shape[0]
    tm = min(PROJ_ROW_TILE, t)
    return pl.pallas_call(
        _norm_proj_kernel,
        grid=(t // tm,),
        in_specs=[pl.BlockSpec((tm, D_MODEL), lambda i: (i, 0)), _full((1, D_MODEL))]
        + [_full(w.shape) for w in weights],
        out_specs=[pl.BlockSpec((tm, w.shape[1]), lambda i: (i, 0)) for w in weights],
        out_shape=[jax.ShapeDtypeStruct((t, w.shape[1]), F32) for w in weights],
        compiler_params=_params("parallel"),
        name="norm_proj",
    )(x, g.reshape(1, D_MODEL), *weights)


def _proj_res_kernel(y_ref, w_ref, x_ref, o_ref):
    o_ref[...] = x_ref[...] + jnp.dot(y_ref[...].astype(BF16), w_ref[...], preferred_element_type=F32)


def _proj_res(y, w, x):
    t, k = y.shape
    tm = min(PROJ_ROW_TILE, t)
    return pl.pallas_call(
        _proj_res_kernel,
        grid=(t // tm,),
        in_specs=[pl.BlockSpec((tm, k), lambda i: (i, 0)), _full(w.shape),
                  pl.BlockSpec((tm, D_MODEL), lambda i: (i, 0))],
        out_specs=pl.BlockSpec((tm, D_MODEL), lambda i: (i, 0)),
        out_shape=jax.ShapeDtypeStruct((t, D_MODEL), F32),
        compiler_params=_params("parallel"),
        name="proj_res",
    )(y, w, x)


def _split3(x):
    hi = x.astype(BF16)
    r1 = x - hi.astype(F32)
    mid = r1.astype(BF16)
    lo = (r1 - mid.astype(F32)).astype(BF16)
    return hi, mid, lo


def _ssd_kernel(xbc_ref, z_ref, dtt_ref, conv0_ref, ssm0_ref,
                cw_ref, cb_ref, dtb_ref, alog_ref, dskip_ref, gn_ref, spread_ref,
                y_ref, ssm_ref, conv_ref,
                ext_scr, state_scr, *pad_scr, lt, ls, nc):
    c = pl.program_id(1)

    @pl.when(c == 0)
    def _():
        ext_scr[0:SUBLANES, :] = conv0_ref[...]
        state_scr[...] = ssm0_ref[...]

    ext_scr[SUBLANES:SUBLANES + lt, :] = xbc_ref[...]
    ext = ext_scr[...]
    tiles3 = ext.reshape(lt // SUBLANES + 1, SUBLANES, CONV_DIM)
    sub = lax.broadcasted_iota(jnp.int32, (1, SUBLANES, CONV_DIM), 1)
    taps = []
    for k in range(CONV_WIDTH - 1):
        d = CONV_WIDTH - 1 - k
        rot = pltpu.roll(tiles3, d, 1)
        shifted = jnp.where(sub < d, rot[:-1], rot[1:]).reshape(lt, CONV_DIM)
        taps.append(shifted * cw_ref[k:k + 1, :])
    taps.append(ext[SUBLANES:] * cw_ref[CONV_WIDTH - 1:CONV_WIDTH, :])
    xc = _silu((((taps[0] + taps[1]) + taps[2]) + taps[3]) + cb_ref[...])
    tail = ext[lt:lt + SUBLANES]
    conv_ref[...] = tail
    ext_scr[0:SUBLANES, :] = tail

    def pad_rows(v, k):
        if ls == lt:
            return v
        pad_scr[k][...] = jnp.zeros(pad_scr[k].shape, F32)
        pad_scr[k][0:lt, :] = v
        return pad_scr[k][...]

    xs = xc[:, :D_INNER]
    bm = xc[:, D_INNER:D_INNER + GROUP_WIDTH]
    cm = xc[:, D_INNER + GROUP_WIDTH:]

    dt_r = _softplus(dtt_ref[...] + dtb_ref[...])
    da_r = dt_r * -jnp.exp(alog_ref[...])
    upper = (lax.broadcasted_iota(jnp.int32, (ls, ls), 0)
             <= lax.broadcasted_iota(jnp.int32, (ls, ls), 1)).astype(F32).astype(BF16)
    parts = jnp.dot(jnp.concatenate(_split3(da_r), axis=0), upper, preferred_element_type=F32)
    acs_r = (parts[0:SSM_HEADS] + parts[SSM_HEADS:2 * SSM_HEADS]) + parts[2 * SSM_HEADS:]
    last_r = acs_r[:, lt - 1:lt]
    c2_r = acs_r * LOG2E
    r2_r = jnp.log(dt_r) * LOG2E - c2_r
    w_r = dt_r * jnp.exp(last_r - acs_r)
    e_r = jnp.exp(acs_r)
    chunk_decay = jnp.exp(jnp.broadcast_to(last_r, (SSM_HEADS, LANES)))
    col = jnp.concatenate([c2_r, w_r, e_r, jnp.zeros((LANES - 3 * SSM_HEADS, ls), F32)],
                          axis=0).T[0:lt]
    wide = jnp.dot(jnp.concatenate(_split3(col), axis=1), spread_ref[...],
                   preferred_element_type=F32)
    w_x = wide[:, :D_INNER]
    e_x = wide[:, D_INNER:]

    causal = (lax.broadcasted_iota(jnp.int32, (lt, ls), 1)
              <= lax.broadcasted_iota(jnp.int32, (lt, ls), 0))
    low_half = lax.broadcasted_iota(jnp.int32, (lt, LANES), 1) < SSM_HEAD_DIM
    xs_s = pad_rows(xs, 0)
    bm_s = pad_rows(bm, 1)
    xd_s = pad_rows(xs * w_x, 2)

    zg = z_ref[...]
    for g in range(SSM_GROUPS):
        cm_g = cm[:, g * SSM_D_STATE:(g + 1) * SSM_D_STATE].astype(BF16)
        bm_g = bm_s[:, g * SSM_D_STATE:(g + 1) * SSM_D_STATE].astype(BF16)
        cb_g = lax.dot_general(cm_g, bm_g, _NT, preferred_element_type=F32)
        rows = slice(g * GROUP_WIDTH, (g + 1) * GROUP_WIDTH)
        y_off = lax.dot_general(cm_g, state_scr[rows, :].astype(BF16), _NT,
                                preferred_element_type=F32)
        tiles = []
        for jj in range(HEADS_PER_GROUP // 2):
            tile = g * (HEADS_PER_GROUP // 2) + jj
            cols = slice(tile * LANES, (tile + 1) * LANES)
            xp = xs_s[:, cols].astype(BF16)
            pair = []
            for h in (2 * tile, 2 * tile + 1):
                expo = col[:, h:h + 1] + r2_r[h:h + 1, :]
                mat = (cb_g * jnp.exp2(jnp.where(causal, expo, -jnp.inf))).astype(BF16)
                pair.append(jnp.dot(mat, xp, preferred_element_type=F32))
            y_diag = jnp.where(low_half, pair[0], pair[1])
            tiles.append(y_diag + y_off[:, jj * LANES:(jj + 1) * LANES] * e_x[:, cols]
                         + dskip_ref[:, cols] * xs[:, cols])
        yg = jnp.concatenate(tiles, axis=1) * _silu(zg[:, rows])
        ms = jnp.mean(yg * yg, axis=-1, keepdims=True)
        y_ref[:, rows] = yg * lax.rsqrt(ms + NORM_EPS) * gn_ref[:, rows]

        upd = lax.dot_general(xd_s[:, rows].astype(BF16), bm_g, _TN, preferred_element_type=F32)
        for hh in range(HEADS_PER_GROUP):
            h = g * HEADS_PER_GROUP + hh
            hrows = slice(h * SSM_HEAD_DIM, (h + 1) * SSM_HEAD_DIM)
            state_scr[hrows, :] = (state_scr[hrows, :] * chunk_decay[h:h + 1, :]
                                   + upd[hh * SSM_HEAD_DIM:(hh + 1) * SSM_HEAD_DIM, :])

    @pl.when(c == nc - 1)
    def _():
        ssm_ref[...] = state_scr[...]


def _spread_matrix():
    k = jnp.arange(LANES)[:, None]
    head = jnp.arange(D_INNER)[None, :] // SSM_HEAD_DIM
    one = jnp.concatenate([k == SSM_HEADS + head, k == 2 * SSM_HEADS + head], axis=1)
    return jnp.concatenate([one, one, one], axis=0).astype(BF16)


def _ssd(xbc, z, dt, conv0, ssm0, p, bsz, seq):
    lt = SSD_CHUNK if seq % SSD_CHUNK == 0 else seq
    ls = max(lt, LANES)
    nc = seq // lt
    xbc = xbc.reshape(bsz, seq, CONV_DIM)
    z = z.reshape(bsz, seq, D_INNER)
    dtt = jnp.swapaxes(dt.reshape(bsz, seq, LANES)[:, :, :SSM_HEADS], 1, 2)
    if ls != lt:
        dtt = jnp.pad(dtt, ((0, 0), (0, 0), (0, ls - lt)))
    conv0 = jnp.pad(conv0, ((0, 0), (SUBLANES - (CONV_WIDTH - 1), 0), (0, 0)))
    ssm0 = ssm0.reshape(bsz, D_INNER, SSM_D_STATE)

    def head_rows(v):
        return jnp.broadcast_to(v.astype(F32)[:, None], (SSM_HEADS, ls))

    pad_scratch = [] if ls == lt else [pltpu.VMEM((ls, D_INNER), F32), pltpu.VMEM((ls, GROUP_WIDTH), F32),
                                       pltpu.VMEM((ls, D_INNER), F32)]
    y, ssm, conv = pl.pallas_call(
        functools.partial(_ssd_kernel, lt=lt, ls=ls, nc=nc),
        grid=(bsz, nc),
        in_specs=[pl.BlockSpec((None, lt, CONV_DIM), lambda b, c: (b, c, 0)),
                  pl.BlockSpec((None, lt, D_INNER), lambda b, c: (b, c, 0)),
                  pl.BlockSpec((None, SSM_HEADS, ls), lambda b, c: (b, 0, c)),
                  pl.BlockSpec((None, SUBLANES, CONV_DIM), lambda b, c: (b, 0, 0)),
                  pl.BlockSpec((None, D_INNER, SSM_D_STATE), lambda b, c: (b, 0, 0)),
                  _full((CONV_WIDTH, CONV_DIM)), _full((1, CONV_DIM)),
                  _full((SSM_HEADS, ls)), _full((SSM_HEADS, ls)),
                  _full((1, D_INNER)), _full((1, D_INNER)),
                  _full((3 * LANES, 2 * D_INNER))],
        out_specs=[pl.BlockSpec((None, lt, D_INNER), lambda b, c: (b, c, 0)),
                   pl.BlockSpec((None, D_INNER, SSM_D_STATE), lambda b, c: (b, 0, 0)),
                   pl.BlockSpec((None, SUBLANES, CONV_DIM), lambda b, c: (b, 0, 0))],
        out_shape=[jax.ShapeDtypeStruct((bsz, seq, D_INNER), F32),
                   jax.ShapeDtypeStruct((bsz, D_INNER, SSM_D_STATE), F32),
                   jax.ShapeDtypeStruct((bsz, SUBLANES, CONV_DIM), F32)],
        scratch_shapes=[pltpu.VMEM((SUBLANES + lt, CONV_DIM), F32),
                        pltpu.VMEM((D_INNER, SSM_D_STATE), F32)] + pad_scratch,
        compiler_params=_params("parallel", "arbitrary"),
        name="ssd",
    )(xbc, z, dtt, conv0, ssm0,
      p["conv_w"], p["conv_b"].reshape(1, CONV_DIM),
      head_rows(p["dt_bias"]), head_rows(p["a_log"]),
      jnp.repeat(p["d_skip"].astype(F32), SSM_HEAD_DIM).reshape(1, D_INNER),
      p["gate_norm"].reshape(1, D_INNER), _spread_matrix())
    return (y.reshape(bsz * seq, D_INNER),
            ssm.reshape(bsz, SSM_HEADS, SSM_HEAD_DIM, SSM_D_STATE),
            conv[:, SUBLANES - (CONV_WIDTH - 1):, :])


def _alibi_slope(head):
    return 2.0 ** (-8.0 * (head + 1) / N_Q_HEADS)


def _attn_kernel(sink_ref, q_ref, kp_ref, kc_ref, vp_ref, vc_ref, o_ref, k_scr, v_scr, *, tq, base):
    i = pl.program_id(1)
    if tq < WINDOW:
        k_scr[...] = jnp.zeros(k_scr.shape, F32)
        v_scr[...] = jnp.zeros(v_scr.shape, F32)
    k_scr[0:WINDOW, :] = kp_ref[...]
    v_scr[0:WINDOW, :] = vp_ref[...]
    k_scr[WINDOW:WINDOW + tq, :] = kc_ref[...]
    v_scr[WINDOW:WINDOW + tq, :] = vc_ref[...]

    nkeys = 2 * WINDOW
    rows = lax.broadcasted_iota(jnp.int32, (Q_PER_KV * tq, nkeys), 0)
    kj = lax.broadcasted_iota(jnp.int32, (Q_PER_KV * tq, nkeys), 1)
    t = rows & (tq - 1)
    sub = lax.shift_right_logical(rows, int(math.log2(tq)))
    dist = WINDOW + t - kj
    kabs = base + (i - 1) * WINDOW + kj
    valid = (dist >= 0) & (dist < WINDOW) & (kabs >= 0)
    distf = dist.astype(F32)
    low_half = lax.broadcasted_iota(jnp.int32, (nkeys, LANES), 1) < ATT_HEAD_DIM

    def pick(vals):
        return jnp.where(sub == 0, vals[0], jnp.where(sub == 1, vals[1],
                                                      jnp.where(sub == 2, vals[2], vals[3])))

    for j in range(N_KV_HEADS):
        cols = slice((j // 2) * LANES, (j // 2 + 1) * LANES)
        kt = k_scr[:, cols]
        vt = v_scr[:, cols]
        if j % 2 == 0:
            k_lo = jnp.where(low_half, kt, 0.0)
            v_lo = jnp.where(low_half, vt, 0.0)
        else:
            k_lo = pltpu.roll(jnp.where(low_half, 0.0, kt), ATT_HEAD_DIM, 1)
            v_lo = pltpu.roll(jnp.where(low_half, 0.0, vt), ATT_HEAD_DIM, 1)
        q0 = q_ref[:, (2 * j) * LANES:(2 * j + 1) * LANES] * (ATT_HEAD_DIM ** -0.5)
        q1 = q_ref[:, (2 * j + 1) * LANES:(2 * j + 2) * LANES] * (ATT_HEAD_DIM ** -0.5)
        qs = jnp.concatenate([q0, pltpu.roll(q0, ATT_HEAD_DIM, 1),
                              q1, pltpu.roll(q1, ATT_HEAD_DIM, 1)], axis=0)
        s = lax.dot_general(qs.astype(BF16), k_lo.astype(BF16), _NT, preferred_element_type=F32)
        slope = pick([_alibi_slope(Q_PER_KV * j + g) for g in range(Q_PER_KV)])
        sink = pick([sink_ref[Q_PER_KV * j + g] for g in range(Q_PER_KV)])[:, 0:1]
        s = jnp.where(valid, s - slope * distf, -jnp.inf)
        mx = jnp.maximum(jnp.max(s, axis=-1, keepdims=True), sink)
        e = jnp.exp(s - mx)
        den = jnp.sum(e, axis=-1, keepdims=True) + jnp.exp(sink - mx)
        prob = (e * (1.0 / den)).astype(BF16)
        o = jnp.dot(prob, v_lo.astype(BF16), preferred_element_type=F32)
        o_ref[:, (2 * j) * LANES:(2 * j + 1) * LANES] = (
            o[0:tq] + pltpu.roll(o[tq:2 * tq], ATT_HEAD_DIM, 1))
        o_ref[:, (2 * j + 1) * LANES:(2 * j + 2) * LANES] = (
            o[2 * tq:3 * tq] + pltpu.roll(o[3 * tq:4 * tq], ATT_HEAD_DIM, 1))


def _attention(q, k_prev, k_cur, v_prev, v_cur, sinks, *, tq, base, prev_map, cur_map, grid):
    bsz, seq = q.shape[0], q.shape[1]
    return pl.pallas_call(
        functools.partial(_attn_kernel, tq=tq, base=base),
        grid=grid,
        in_specs=[pl.BlockSpec(memory_space=pltpu.SMEM),
                  pl.BlockSpec((None, tq, D_MODEL), lambda b, i: (b, i, 0)),
                  pl.BlockSpec((None, WINDOW, KV_WIDTH), prev_map),
                  pl.BlockSpec((None, tq, KV_WIDTH), cur_map),
                  pl.BlockSpec((None, WINDOW, KV_WIDTH), prev_map),
                  pl.BlockSpec((None, tq, KV_WIDTH), cur_map)],
        out_specs=pl.BlockSpec((None, tq, D_MODEL), lambda b, i: (b, i, 0)),
        out_shape=jax.ShapeDtypeStruct((bsz, seq, D_MODEL), F32),
        scratch_shapes=[pltpu.VMEM((2 * WINDOW, KV_WIDTH), F32), pltpu.VMEM((2 * WINDOW, KV_WIDTH), F32)],
        compiler_params=_params("parallel", "arbitrary"),
        name="swa",
    )(sinks.astype(F32), q, k_prev, k_cur, v_prev, v_cur)


def _trunk(x, ssm_in, conv_in, k_buf, v_buf, w):
    bsz, seq = x.shape[0], x.shape[1]
    is_prompt = k_buf is None
    x = x.reshape(bsz * seq, D_MODEL)

    x = _ffn(x, w["ffn1_norm"][0], w["ffn1_w_gu"][0], w["ffn1_w_down"][0])
    z, xbc, dt = _norm_proj(x, w["mix_norm"][0], [w["ssm_w_z"], w["ssm_w_xbc"], w["ssm_w_dt"]])
    y, ssm_out, conv_out = _ssd(xbc, z, dt, conv_in, ssm_in, w["ssm"], bsz, seq)
    x = _proj_res(y, w["ssm_w_out"], x)
    x = _ffn(x, w["ffn2_norm"][0], w["ffn2_w_gu"][0], w["ffn2_w_down"][0])

    k_new, v_new = _norm_proj(x, w["kv_norm"], [w["w_k"], w["w_v"]])
    k_new = k_new.reshape(bsz, seq, KV_WIDTH)
    v_new = v_new.reshape(bsz, seq, KV_WIDTH)

    x = _ffn(x, w["ffn1_norm"][1], w["ffn1_w_gu"][1], w["ffn1_w_down"][1])
    (q,) = _norm_proj(x, w["mix_norm"][1], [w["attn_w_q"]])
    q = q.reshape(bsz, seq, D_MODEL)
    if is_prompt:
        o = _attention(q, k_new, k_new, v_new, v_new, w["attn_sinks"], tq=WINDOW, base=0,
                       prev_map=lambda b, i: (b, jnp.maximum(i - 1, 0), 0),
                       cur_map=lambda b, i: (b, i, 0), grid=(bsz, seq // WINDOW))
        k_win, v_win = k_new[:, -WINDOW:], v_new[:, -WINDOW:]
    else:
        k_buf = k_buf.reshape(bsz, WINDOW, KV_WIDTH)
        v_buf = v_buf.reshape(bsz, WINDOW, KV_WIDTH)
        o = _attention(q, k_buf, k_new, v_buf, v_new, w["attn_sinks"], tq=seq, base=PAST_LEN,
                       prev_map=lambda b, i: (b, 0, 0), cur_map=lambda b, i: (b, 0, 0),
                       grid=(bsz, 1))
        k_win = jnp.concatenate([k_buf, k_new], axis=1)[:, -WINDOW:]
        v_win = jnp.concatenate([v_buf, v_new], axis=1)[:, -WINDOW:]
    x = _proj_res(o.reshape(bsz * seq, D_MODEL), w["attn_w_o"], x)
    y = _ffn(x, w["ffn2_norm"][1], w["ffn2_w_gu"][1], w["ffn2_w_down"][1], fg=w["final_norm"])

    return (y.reshape(bsz, seq, D_MODEL), ssm_out[None], conv_out[None],
            k_win.reshape(bsz, WINDOW, N_KV_HEADS, ATT_HEAD_DIM),
            v_win.reshape(bsz, WINDOW, N_KV_HEADS, ATT_HEAD_DIM))


def kernel(x_prompt, x_sample, state_ssm, state_conv, cache_k_win, cache_v_win,
           ffn1_norm, ffn1_w_gu, ffn1_w_down, mix_norm, ffn2_norm, ffn2_w_gu, ffn2_w_down,
           ssm_w_in, ssm_conv_w, ssm_conv_b, ssm_dt_bias, ssm_a_log, ssm_d, ssm_gate_norm, ssm_w_out,
           kv_norm, w_kv, attn_w_q, attn_sinks, attn_w_o, final_norm):
    w_in = ssm_w_in[0]
    w_dt = jnp.pad(w_in[:, D_INNER + CONV_DIM:], ((0, 0), (0, LANES - SSM_HEADS)))
    w = dict(
        ffn1_norm=ffn1_norm, ffn2_norm=ffn2_norm, mix_norm=mix_norm, kv_norm=kv_norm,
        final_norm=final_norm, attn_sinks=attn_sinks[0],
        ffn1_w_gu=ffn1_w_gu.astype(BF16), ffn1_w_down=ffn1_w_down.astype(BF16),
        ffn2_w_gu=ffn2_w_gu.astype(BF16), ffn2_w_down=ffn2_w_down.astype(BF16),
        ssm_w_z=w_in[:, :D_INNER].astype(BF16),
        ssm_w_xbc=w_in[:, D_INNER:D_INNER + CONV_DIM].astype(BF16),
        ssm_w_dt=w_dt.astype(BF16),
        ssm_w_out=ssm_w_out[0].astype(BF16),
        w_k=w_kv[:, :KV_WIDTH].astype(BF16), w_v=w_kv[:, KV_WIDTH:].astype(BF16),
        attn_w_q=attn_w_q[0].astype(BF16), attn_w_o=attn_w_o[0].astype(BF16),
        ssm=dict(conv_w=ssm_conv_w[0], conv_b=ssm_conv_b[0], dt_bias=ssm_dt_bias[0],
                 a_log=ssm_a_log[0], d_skip=ssm_d[0], gate_norm=ssm_gate_norm[0]),
    )
    bp = x_prompt.shape[0]
    ssm0 = jnp.zeros((bp, SSM_HEADS, SSM_HEAD_DIM, SSM_D_STATE), F32)
    conv0 = jnp.zeros((bp, CONV_WIDTH - 1, CONV_DIM), F32)
    y_p, ssm_p, conv_p, kw_p, vw_p = _trunk(x_prompt, ssm0, conv0, None, None, w)
    y_s, ssm_s, conv_s, kw_s, vw_s = _trunk(x_sample, state_ssm[0], state_conv[0],
                                            cache_k_win, cache_v_win, w)
    return (y_p, y_s, ssm_p, conv_p, kw_p, vw_p, ssm_s, conv_s, kw_s, vw_s)
```

```python
import functools
import math

import jax
import jax.numpy as jnp
from jax import lax
from jax.experimental import pallas as pl
from jax.experimental.pallas import tpu as pltpu

F32 = jnp.float32
BF16 = jnp.bfloat16

D_MODEL = 1024
D_FF = 2816
D_INNER = 2048
SSM_HEAD_DIM = 64
SSM_HEADS = 32
SSM_GROUPS = 4
HEADS_PER_GROUP = SSM_HEADS // SSM_GROUPS
SSM_D_STATE = 128
GROUP_WIDTH = D_INNER // SSM_GROUPS
CONV_WIDTH = 4
CONV_DIM = D_INNER + 2 * SSM_GROUPS * SSM_D_STATE
SSD_CHUNK = 128
WINDOW = 128
ATT_HEAD_DIM = 64
N_Q_HEADS = 16
N_KV_HEADS = 4
Q_PER_KV = N_Q_HEADS // N_KV_HEADS
KV_WIDTH = N_KV_HEADS * ATT_HEAD_DIM
PAST_LEN = 8192
NORM_EPS = 1e-5
LOG2E = 1.0 / math.log(2.0)

LANES = 128
SUBLANES = 8
VMEM_LIMIT_BYTES = 56 * 1024 * 1024
FF_CHUNK = 256
ROW_TILE = 1024
PROJ_ROW_TILE = 512
PROJ_COL_CHUNK = 512
MIX_ROWS = 256
DECODE_SEQS = 4

_NT = (((1,), (1,)), ((), ()))
_TN = (((0,), (0,)), ((), ()))


def _rms(x, g):
    ms = jnp.mean(x * x, axis=-1, keepdims=True)
    return x * lax.rsqrt(ms + NORM_EPS) * g


def _silu(x):
    return x * (1.0 / (1.0 + jnp.exp(-x)))


def _softplus(x):
    return jnp.maximum(x, 0.0) + jnp.log1p(jnp.exp(-jnp.abs(x)))


def _params(*sem):
    return pltpu.CompilerParams(dimension_semantics=sem, vmem_limit_bytes=VMEM_LIMIT_BYTES)


def _full(shape):
    return pl.BlockSpec(shape, lambda *_: (0,) * len(shape))


def _ffn_kernel(x_ref, g_ref, wgu_ref, wd_ref, fg_ref, o_ref, acc_ref, *, final_norm):
    x = x_ref[...]
    h = _rms(x, g_ref[...]).astype(BF16)
    for j in range(D_FF // FF_CHUNK):
        lo = j * FF_CHUNK
        gate = jnp.dot(h, wgu_ref[:, lo:lo + FF_CHUNK], preferred_element_type=F32)
        up = jnp.dot(h, wgu_ref[:, D_FF + lo:D_FF + lo + FF_CHUNK], preferred_element_type=F32)
        act = (_silu(gate) * up).astype(BF16)
        part = jnp.dot(act, wd_ref[lo:lo + FF_CHUNK, :], preferred_element_type=F32)
        if j == 0:
            acc_ref[...] = part
        else:
            acc_ref[...] += part
    out = x + 0.5 * acc_ref[...]
    if final_norm:
        out = _rms(out, fg_ref[...])
    o_ref[...] = out


def _ffn(x, g, wgu, wd, fg=None):
    t = x.shape[0]
    tm = min(ROW_TILE, t)
    final_norm = fg is not None
    if fg is None:
        fg = g
    return pl.pallas_call(
        functools.partial(_ffn_kernel, final_norm=final_norm),
        grid=(t // tm,),
        in_specs=[pl.BlockSpec((tm, D_MODEL), lambda i: (i, 0)),
                  _full((1, D_MODEL)), _full((D_MODEL, 2 * D_FF)), _full((D_FF, D_MODEL)),
                  _full((1, D_MODEL))],
        out_specs=pl.BlockSpec((tm, D_MODEL), lambda i: (i, 0)),
        out_shape=jax.ShapeDtypeStruct((t, D_MODEL), F32),
        scratch_shapes=[pltpu.VMEM((tm, D_MODEL), F32)],
        compiler_params=_params("parallel"),
        name="ffn",
    )(x, g.reshape(1, D_MODEL), wgu, wd, fg.reshape(1, D_MODEL))


def _norm_proj_kernel(x_ref, g_ref, *refs):
    n = len(refs) // 2
    h = _rms(x_ref[...], g_ref[...]).astype(BF16)
    for w_ref, o_ref in zip(refs[:n], refs[n:]):
        width = w_ref.shape[1]
        step = min(PROJ_COL_CHUNK, width)
        for lo in range(0, width, step):
            o_ref[:, lo:lo + step] = jnp.dot(h, w_ref[:, lo:lo + step], preferred_element_type=F32)


def _norm_proj(x, g, weights):
    t = x.shape[0]
    tm = min(PROJ_ROW_TILE, t)
    return pl.pallas_call(
        _norm_proj_kernel,
        grid=(t // tm,),
        in_specs=[pl.BlockSpec((tm, D_MODEL), lambda i: (i, 0)), _full((1, D_MODEL))]
        + [_full(w.shape) for w in weights],
        out_specs=[pl.BlockSpec((tm, w.shape[1]), lambda i: (i, 0)) for w in weights],
        out_shape=[jax.ShapeDtypeStruct((t, w.shape[1]), F32) for w in weights],
        compiler_params=_params("parallel"),
        name="norm_proj",
    )(x, g.reshape(1, D_MODEL), *weights)


def _proj_res_kernel(y_ref, w_ref, x_ref, o_ref):
    o_ref[...] = x_ref[...] + jnp.dot(y_ref[...].astype(BF16), w_ref[...], preferred_element_type=F32)


def _proj_res(y, w, x):
    t, k = y.shape
    tm = min(PROJ_ROW_TILE, t)
    return pl.pallas_call(
        _proj_res_kernel,
        grid=(t // tm,),
        in_specs=[pl.BlockSpec((tm, k), lambda i: (i, 0)), _full(w.shape),
                  pl.BlockSpec((tm, D_MODEL), lambda i: (i, 0))],
        out_specs=pl.BlockSpec((tm, D_MODEL), lambda i: (i, 0)),
        out_shape=jax.ShapeDtypeStruct((t, D_MODEL), F32),
        compiler_params=_params("parallel"),
        name="proj_res",
    )(y, w, x)


def _split3(x):
    hi = x.astype(BF16)
    r1 = x - hi.astype(F32)
    mid = r1.astype(BF16)
    lo = (r1 - mid.astype(F32)).astype(BF16)
    return hi, mid, lo


def _conv_silu(ext_scr, n, cw_ref, cb_ref):
    ext = ext_scr[...]
    tiles3 = ext.reshape(n // SUBLANES + 1, SUBLANES, CONV_DIM)
    sub = lax.broadcasted_iota(jnp.int32, (1, SUBLANES, CONV_DIM), 1)
    taps = []
    for k in range(CONV_WIDTH - 1):
        d = CONV_WIDTH - 1 - k
        rot = pltpu.roll(tiles3, d, 1)
        shifted = jnp.where(sub < d, rot[:-1], rot[1:]).reshape(n, CONV_DIM)
        taps.append(shifted * cw_ref[k:k + 1, :])
    taps.append(ext[SUBLANES:] * cw_ref[CONV_WIDTH - 1:CONV_WIDTH, :])
    xc = _silu((((taps[0] + taps[1]) + taps[2]) + taps[3]) + cb_ref[...])
    return xc, ext[n:n + SUBLANES]


def _ssd_chunk(xc, zg, dtt, state_scr, dtb_ref, alog_ref, dskip_ref, gn_ref, spread_ref,
               pad_scr, lt, ls, emit):
    def pad_rows(v, k):
        if ls == lt:
            return v
        pad_scr[k][...] = jnp.zeros(pad_scr[k].shape, F32)
        pad_scr[k][0:lt, :] = v
        return pad_scr[k][...]

    xs = xc[:, :D_INNER]
    bm = xc[:, D_INNER:D_INNER + GROUP_WIDTH]
    cm = xc[:, D_INNER + GROUP_WIDTH:]

    dt_r = _softplus(dtt + dtb_ref[...])
    da_r = dt_r * -jnp.exp(alog_ref[...])
    upper = (lax.broadcasted_iota(jnp.int32, (ls, ls), 0)
             <= lax.broadcasted_iota(jnp.int32, (ls, ls), 1)).astype(F32).astype(BF16)
    parts = jnp.dot(jnp.concatenate(_split3(da_r), axis=0), upper, preferred_element_type=F32)
    acs_r = (parts[0:SSM_HEADS] + parts[SSM_HEADS:2 * SSM_HEADS]) + parts[2 * SSM_HEADS:]
    last_r = acs_r[:, lt - 1:lt]
    c2_r = acs_r * LOG2E
    r2_r = jnp.log(dt_r) * LOG2E - c2_r
    w_r = dt_r * jnp.exp(last_r - acs_r)
    e_r = jnp.exp(acs_r)
    chunk_decay = jnp.exp(jnp.broadcast_to(last_r, (SSM_HEADS, LANES)))
    col = jnp.concatenate([c2_r, w_r, e_r, jnp.zeros((LANES - 3 * SSM_HEADS, ls), F32)],
                          axis=0).T[0:lt]
    wide = jnp.dot(jnp.concatenate(_split3(col), axis=1), spread_ref[...],
                   preferred_element_type=F32)
    w_x = wide[:, :D_INNER]
    e_x = wide[:, D_INNER:]

    causal = (lax.broadcasted_iota(jnp.int32, (lt, ls), 1)
              <= lax.broadcasted_iota(jnp.int32, (lt, ls), 0))
    low_half = lax.broadcasted_iota(jnp.int32, (lt, LANES), 1) < SSM_HEAD_DIM
    xs_s = pad_rows(xs, 0)
    bm_s = pad_rows(bm, 1)
    xd_s = pad_rows(xs * w_x, 2)

    for g in range(SSM_GROUPS):
        cm_g = cm[:, g * SSM_D_STATE:(g + 1) * SSM_D_STATE].astype(BF16)
        bm_g = bm_s[:, g * SSM_D_STATE:(g + 1) * SSM_D_STATE].astype(BF16)
        cb_g = lax.dot_general(cm_g, bm_g, _NT, preferred_element_type=F32)
        rows = slice(g * GROUP_WIDTH, (g + 1) * GROUP_WIDTH)
        y_off = lax.dot_general(cm_g, state_scr[rows, :].astype(BF16), _NT,
                                preferred_element_type=F32)
        tiles = []
        for jj in range(HEADS_PER_GROUP // 2):
            tile = g * (HEADS_PER_GROUP // 2) + jj
            cols = slice(tile * LANES, (tile + 1) * LANES)
            xp = xs_s[:, cols].astype(BF16)
            pair = []
            for h in (2 * tile, 2 * tile + 1):
                expo = col[:, h:h + 1] + r2_r[h:h + 1, :]
                mat = (cb_g * jnp.exp2(jnp.where(causal, expo, -jnp.inf))).astype(BF16)
                pair.append(jnp.dot(mat, xp, preferred_element_type=F32))
            y_diag = jnp.where(low_half, pair[0], pair[1])
            tiles.append(y_diag + y_off[:, jj * LANES:(jj + 1) * LANES] * e_x[:, cols]
                         + dskip_ref[:, cols] * xs[:, cols])
        yg = jnp.concatenate(tiles, axis=1) * _silu(zg[:, rows])
        ms = jnp.mean(yg * yg, axis=-1, keepdims=True)
        emit(g, yg * lax.rsqrt(ms + NORM_EPS) * gn_ref[:, rows])

        upd = lax.dot_general(xd_s[:, rows].astype(BF16), bm_g, _TN, preferred_element_type=F32)
        for hh in range(HEADS_PER_GROUP):
            h = g * HEADS_PER_GROUP + hh
            hrows = slice(h * SSM_HEAD_DIM, (h + 1) * SSM_HEAD_DIM)
            state_scr[hrows, :] = (state_scr[hrows, :] * chunk_decay[h:h + 1, :]
                                   + upd[hh * SSM_HEAD_DIM:(hh + 1) * SSM_HEAD_DIM, :])


def _ssd_decode_kernel(xbc_ref, z_ref, dtt_ref, conv0_ref, ssm0_ref,
                       cw_ref, cb_ref, dtb_ref, alog_ref, dskip_ref, gn_ref, spread_ref,
                       y_ref, ssm_ref, conv_ref, ext_scr, *pad_scr, lt, ls, nseq):
    for i in range(nseq):
        ext_i = ext_scr.at[i]
        ext_i[0:SUBLANES, :] = conv0_ref[i]
        ext_i[SUBLANES:SUBLANES + lt, :] = xbc_ref[i]
        xc, tail = _conv_silu(ext_i, lt, cw_ref, cb_ref)
        conv_ref[i] = tail
        ssm_ref[i] = ssm0_ref[i]

        def emit(g, y, i=i):
            y_ref[i, :, g * GROUP_WIDTH:(g + 1) * GROUP_WIDTH] = y

        _ssd_chunk(xc, z_ref[i], dtt_ref[i], ssm_ref.at[i], dtb_ref, alog_ref, dskip_ref, gn_ref,
                   spread_ref, [p.at[i] for p in pad_scr], lt, ls, emit)


def _mixer0_kernel(x_ref, g_ref, wz_ref, wxbc_ref, wdtt_ref, wout_ref, conv0_ref, ssm0_ref,
                   cw_ref, cb_ref, dtb_ref, alog_ref, dskip_ref, gn_ref, spread_ref,
                   o_ref, ssm_ref, conv_ref,
                   ext_scr, state_scr, z_scr, y_scr, *, rb, nsteps):
    c = pl.program_id(1)

    @pl.when(c == 0)
    def _():
        ext_scr[0:SUBLANES, :] = conv0_ref[...]
        state_scr[...] = ssm0_ref[...]

    x = x_ref[...]
    h = _rms(x, g_ref[...]).astype(BF16)
    for lo in range(0, CONV_DIM, PROJ_COL_CHUNK):
        ext_scr[SUBLANES:SUBLANES + rb, lo:lo + PROJ_COL_CHUNK] = jnp.dot(
            h, wxbc_ref[:, lo:lo + PROJ_COL_CHUNK], preferred_element_type=F32)
    for lo in range(0, D_INNER, PROJ_COL_CHUNK):
        z_scr[:, lo:lo + PROJ_COL_CHUNK] = jnp.dot(
            h, wz_ref[:, lo:lo + PROJ_COL_CHUNK], preferred_element_type=F32)
    dtt = lax.dot_general(wdtt_ref[...], h, _NT, preferred_element_type=F32)

    xc, tail = _conv_silu(ext_scr, rb, cw_ref, cb_ref)
    conv_ref[...] = tail
    ext_scr[0:SUBLANES, :] = tail

    for k in range(rb // SSD_CHUNK):
        rows = slice(k * SSD_CHUNK, (k + 1) * SSD_CHUNK)

        def emit(g, y, rows=rows):
            y_scr[rows, g * GROUP_WIDTH:(g + 1) * GROUP_WIDTH] = y.astype(BF16)

        _ssd_chunk(xc[rows], z_scr[rows, :], dtt[:, rows], state_scr, dtb_ref, alog_ref,
                   dskip_ref, gn_ref, spread_ref, (), SSD_CHUNK, SSD_CHUNK, emit)

    o_ref[...] = x + jnp.dot(y_scr[...], wout_ref[...], preferred_element_type=F32)

    @pl.when(c == nsteps - 1)
    def _():
        ssm_ref[...] = state_scr[...]


def _spread_matrix():
    k = jnp.arange(LANES)[:, None]
    head = jnp.arange(D_INNER)[None, :] // SSM_HEAD_DIM
    one = jnp.concatenate([k == SSM_HEADS + head, k == 2 * SSM_HEADS + head], axis=1)
    return jnp.concatenate([one, one, one], axis=0).astype(BF16)


def _ssd_decode(xbc, z, dt, conv0, ssm0, p, bsz, seq):
    lt, ls, nseq = seq, LANES, DECODE_SEQS
    xbc = xbc.reshape(bsz, seq, CONV_DIM)
    z = z.reshape(bsz, seq, D_INNER)
    dtt = jnp.swapaxes(dt.reshape(bsz, seq, LANES)[:, :, :SSM_HEADS], 1, 2)
    dtt = jnp.pad(dtt, ((0, 0), (0, 0), (0, ls - lt)))
    conv0 = jnp.pad(conv0, ((0, 0), (SUBLANES - (CONV_WIDTH - 1), 0), (0, 0)))
    ssm0 = ssm0.reshape(bsz, D_INNER, SSM_D_STATE)

    def head_rows(v):
        return jnp.broadcast_to(v.astype(F32)[:, None], (SSM_HEADS, ls))

    def seqs(*dims):
        return pl.BlockSpec((nseq,) + dims, lambda b: (b,) + (0,) * len(dims))

    y, ssm, conv = pl.pallas_call(
        functools.partial(_ssd_decode_kernel, lt=lt, ls=ls, nseq=nseq),
        grid=(bsz // nseq,),
        in_specs=[seqs(lt, CONV_DIM), seqs(lt, D_INNER), seqs(SSM_HEADS, ls),
                  seqs(SUBLANES, CONV_DIM), seqs(D_INNER, SSM_D_STATE),
                  _full((CONV_WIDTH, CONV_DIM)), _full((1, CONV_DIM)),
                  _full((SSM_HEADS, ls)), _full((SSM_HEADS, ls)),
                  _full((1, D_INNER)), _full((1, D_INNER)),
                  _full((3 * LANES, 2 * D_INNER))],
        out_specs=[seqs(lt, D_INNER), seqs(D_INNER, SSM_D_STATE), seqs(SUBLANES, CONV_DIM)],
        out_shape=[jax.ShapeDtypeStruct((bsz, seq, D_INNER), F32),
                   jax.ShapeDtypeStruct((bsz, D_INNER, SSM_D_STATE), F32),
                   jax.ShapeDtypeStruct((bsz, SUBLANES, CONV_DIM), F32)],
        scratch_shapes=[pltpu.VMEM((nseq, SUBLANES + lt, CONV_DIM), F32),
                        pltpu.VMEM((nseq, ls, D_INNER), F32), pltpu.VMEM((nseq, ls, GROUP_WIDTH), F32),
                        pltpu.VMEM((nseq, ls, D_INNER), F32)],
        compiler_params=_params("parallel"),
        name="ssd_decode",
    )(xbc, z, dtt, conv0, ssm0,
      p["conv_w"], p["conv_b"].reshape(1, CONV_DIM),
      head_rows(p["dt_bias"]), head_rows(p["a_log"]),
      jnp.repeat(p["d_skip"].astype(F32), SSM_HEAD_DIM).reshape(1, D_INNER),
      p["gate_norm"].reshape(1, D_INNER), _spread_matrix())
    return (y.reshape(bsz * seq, D_INNER),
            ssm.reshape(bsz, SSM_HEADS, SSM_HEAD_DIM, SSM_D_STATE),
            conv[:, SUBLANES - (CONV_WIDTH - 1):, :])


def _mixer0(x, conv0, ssm0, w, bsz, seq):
    rb = MIX_ROWS
    nsteps = seq // rb
    p = w["ssm"]
    conv0 = jnp.pad(conv0, ((0, 0), (SUBLANES - (CONV_WIDTH - 1), 0), (0, 0)))
    ssm0 = ssm0.reshape(bsz, D_INNER, SSM_D_STATE)

    def head_rows(v):
        return jnp.broadcast_to(v.astype(F32)[:, None], (SSM_HEADS, SSD_CHUNK))

    out, ssm, conv = pl.pallas_call(
        functools.partial(_mixer0_kernel, rb=rb, nsteps=nsteps),
        grid=(bsz, nsteps),
        in_specs=[pl.BlockSpec((rb, D_MODEL), lambda b, c: (b * nsteps + c, 0)),
                  _full((1, D_MODEL)), _full((D_MODEL, D_INNER)), _full((D_MODEL, CONV_DIM)),
                  _full((SSM_HEADS, D_MODEL)), _full((D_INNER, D_MODEL)),
                  pl.BlockSpec((None, SUBLANES, CONV_DIM), lambda b, c: (b, 0, 0)),
                  pl.BlockSpec((None, D_INNER, SSM_D_STATE), lambda b, c: (b, 0, 0)),
                  _full((CONV_WIDTH, CONV_DIM)), _full((1, CONV_DIM)),
                  _full((SSM_HEADS, SSD_CHUNK)), _full((SSM_HEADS, SSD_CHUNK)),
                  _full((1, D_INNER)), _full((1, D_INNER)),
                  _full((3 * LANES, 2 * D_INNER))],
        out_specs=[pl.BlockSpec((rb, D_MODEL), lambda b, c: (b * nsteps + c, 0)),
                   pl.BlockSpec((None, D_INNER, SSM_D_STATE), lambda b, c: (b, 0, 0)),
                   pl.BlockSpec((None, SUBLANES, CONV_DIM), lambda b, c: (b, 0, 0))],
        out_shape=[jax.ShapeDtypeStruct((bsz * seq, D_MODEL), F32),
                   jax.ShapeDtypeStruct((bsz, D_INNER, SSM_D_STATE), F32),
                   jax.ShapeDtypeStruct((bsz, SUBLANES, CONV_DIM), F32)],
        scratch_shapes=[pltpu.VMEM((SUBLANES + rb, CONV_DIM), F32),
                        pltpu.VMEM((D_INNER, SSM_D_STATE), F32),
                        pltpu.VMEM((rb, D_INNER), F32),
                        pltpu.VMEM((rb, D_INNER), BF16)],
        compiler_params=_params("parallel", "arbitrary"),
        name="mixer0",
    )(x, w["mix_norm"][0].reshape(1, D_MODEL), w["ssm_w_z"], w["ssm_w_xbc"], w["ssm_w_dtt"],
      w["ssm_w_out"], conv0, ssm0,
      p["conv_w"], p["conv_b"].reshape(1, CONV_DIM),
      head_rows(p["dt_bias"]), head_rows(p["a_log"]),
      jnp.repeat(p["d_skip"].astype(F32), SSM_HEAD_DIM).reshape(1, D_INNER),
      p["gate_norm"].reshape(1, D_INNER), _spread_matrix())
    return (out, ssm.reshape(bsz, SSM_HEADS, SSM_HEAD_DIM, SSM_D_STATE),
            conv[:, SUBLANES - (CONV_WIDTH - 1):, :])


def _alibi_slope(head):
    return 2.0 ** (-8.0 * (head + 1) / N_Q_HEADS)


def _attn_block(q, keys, vals, sink_ref, first_key_pos, tq):
    nkeys = 2 * WINDOW
    rows = lax.broadcasted_iota(jnp.int32, (Q_PER_KV * tq, nkeys), 0)
    kj = lax.broadcasted_iota(jnp.int32, (Q_PER_KV * tq, nkeys), 1)
    t = rows & (tq - 1)
    sub = lax.shift_right_logical(rows, int(math.log2(tq)))
    dist = WINDOW + t - kj
    valid = (dist >= 0) & (dist < WINDOW) & (first_key_pos + kj >= 0)
    distf = dist.astype(F32)
    low_half = lax.broadcasted_iota(jnp.int32, (nkeys, LANES), 1) < ATT_HEAD_DIM

    def pick(choices):
        return jnp.where(sub == 0, choices[0], jnp.where(sub == 1, choices[1],
                                                         jnp.where(sub == 2, choices[2], choices[3])))

    out = []
    for j in range(N_KV_HEADS):
        cols = slice((j // 2) * LANES, (j // 2 + 1) * LANES)
        kt = keys[:, cols]
        vt = vals[:, cols]
        if j % 2 == 0:
            k_lo = jnp.where(low_half, kt, 0.0)
            v_lo = jnp.where(low_half, vt, 0.0)
        else:
            k_lo = pltpu.roll(jnp.where(low_half, 0.0, kt), ATT_HEAD_DIM, 1)
            v_lo = pltpu.roll(jnp.where(low_half, 0.0, vt), ATT_HEAD_DIM, 1)
        q0 = q[:, (2 * j) * LANES:(2 * j + 1) * LANES] * (ATT_HEAD_DIM ** -0.5)
        q1 = q[:, (2 * j + 1) * LANES:(2 * j + 2) * LANES] * (ATT_HEAD_DIM ** -0.5)
        qs = jnp.concatenate([q0, pltpu.roll(q0, ATT_HEAD_DIM, 1),
                              q1, pltpu.roll(q1, ATT_HEAD_DIM, 1)], axis=0)
        s = lax.dot_general(qs.astype(BF16), k_lo.astype(BF16), _NT, preferred_element_type=F32)
        slope = pick([_alibi_slope(Q_PER_KV * j + g) for g in range(Q_PER_KV)])
        sink = pick([sink_ref[Q_PER_KV * j + g] for g in range(Q_PER_KV)])[:, 0:1]
        s = jnp.where(valid, s - slope * distf, -jnp.inf)
        mx = jnp.maximum(jnp.max(s, axis=-1, keepdims=True), sink)
        e = jnp.exp(s - mx)
        den = jnp.sum(e, axis=-1, keepdims=True) + jnp.exp(sink - mx)
        prob = (e * (1.0 / den)).astype(BF16)
        o = jnp.dot(prob, v_lo.astype(BF16), preferred_element_type=F32)
        out.append(o[0:tq] + pltpu.roll(o[tq:2 * tq], ATT_HEAD_DIM, 1))
        out.append(o[2 * tq:3 * tq] + pltpu.roll(o[3 * tq:4 * tq], ATT_HEAD_DIM, 1))
    return out


def _attn_decode_kernel(sink_ref, q_ref, kp_ref, kc_ref, vp_ref, vc_ref, o_ref, k_scr, v_scr,
                        *, tq, nseq):
    k_scr[...] = jnp.zeros(k_scr.shape, F32)
    v_scr[...] = jnp.zeros(v_scr.shape, F32)
    for i in range(nseq):
        k_scr[i, 0:WINDOW, :] = kp_ref[i]
        v_scr[i, 0:WINDOW, :] = vp_ref[i]
        k_scr[i, WINDOW:WINDOW + tq, :] = kc_ref[i]
        v_scr[i, WINDOW:WINDOW + tq, :] = vc_ref[i]
        tiles = _attn_block(q_ref[i], k_scr[i], v_scr[i], sink_ref, PAST_LEN - WINDOW, tq)
        for m, tile in enumerate(tiles):
            o_ref[i, :, m * LANES:(m + 1) * LANES] = tile


def _mixer1_kernel(sink_ref, x_ref, xkv_ref, gq_ref, gkv_ref, wq_ref, wk_ref, wv_ref, wo_ref,
                   o_ref, kwin_ref, vwin_ref, k_scr, v_scr, att_scr, *, rb):
    c = pl.program_id(1)

    @pl.when(c == 0)
    def _():
        k_scr[0:WINDOW, :] = jnp.zeros((WINDOW, KV_WIDTH), F32)
        v_scr[0:WINDOW, :] = jnp.zeros((WINDOW, KV_WIDTH), F32)

    hkv = _rms(xkv_ref[...], gkv_ref[...]).astype(BF16)
    k_scr[WINDOW:WINDOW + rb, :] = jnp.dot(hkv, wk_ref[...], preferred_element_type=F32)
    v_scr[WINDOW:WINDOW + rb, :] = jnp.dot(hkv, wv_ref[...], preferred_element_type=F32)
    x = x_ref[...]
    q = jnp.dot(_rms(x, gq_ref[...]).astype(BF16), wq_ref[...], preferred_element_type=F32)
    for blk in range(rb // WINDOW):
        lo = blk * WINDOW
        tiles = _attn_block(q[lo:lo + WINDOW], k_scr[lo:lo + 2 * WINDOW, :],
                            v_scr[lo:lo + 2 * WINDOW, :], sink_ref, c * rb + lo - WINDOW, WINDOW)
        for m, tile in enumerate(tiles):
            att_scr[lo:lo + WINDOW, m * LANES:(m + 1) * LANES] = tile.astype(BF16)
    o_ref[...] = x + jnp.dot(att_scr[...], wo_ref[...], preferred_element_type=F32)
    k_last = k_scr[rb:rb + WINDOW, :]
    v_last = v_scr[rb:rb + WINDOW, :]
    kwin_ref[...] = k_last
    vwin_ref[...] = v_last
    k_scr[0:WINDOW, :] = k_last
    v_scr[0:WINDOW, :] = v_last


def _mixer1(x, xkv, w, bsz, seq):
    rb = MIX_ROWS
    nsteps = seq // rb
    row_spec = pl.BlockSpec((rb, D_MODEL), lambda b, c: (b * nsteps + c, 0))
    win_spec = pl.BlockSpec((None, WINDOW, KV_WIDTH), lambda b, c: (b, 0, 0))
    return pl.pallas_call(
        functools.partial(_mixer1_kernel, rb=rb),
        grid=(bsz, nsteps),
        in_specs=[pl.BlockSpec(memory_space=pltpu.SMEM), row_spec, row_spec,
                  _full((1, D_MODEL)), _full((1, D_MODEL)),
                  _full((D_MODEL, D_MODEL)), _full((D_MODEL, KV_WIDTH)), _full((D_MODEL, KV_WIDTH)),
                  _full((D_MODEL, D_MODEL))],
        out_specs=[row_spec, win_spec, win_spec],
        out_shape=[jax.ShapeDtypeStruct((bsz * seq, D_MODEL), F32),
                   jax.ShapeDtypeStruct((bsz, WINDOW, KV_WIDTH), F32),
                   jax.ShapeDtypeStruct((bsz, WINDOW, KV_WIDTH), F32)],
        scratch_shapes=[pltpu.VMEM((WINDOW + rb, KV_WIDTH), F32), pltpu.VMEM((WINDOW + rb, KV_WIDTH), F32),
                        pltpu.VMEM((rb, D_MODEL), BF16)],
        compiler_params=_params("parallel", "arbitrary"),
        name="mixer1",
    )(w["attn_sinks"].astype(F32), x, xkv, w["mix_norm"][1].reshape(1, D_MODEL),
      w["kv_norm"].reshape(1, D_MODEL), w["attn_w_q"], w["w_k"], w["w_v"], w["attn_w_o"])


def _attention_decode(q, k_cache, k_new, v_cache, v_new, sinks):
    bsz, seq = q.shape[0], q.shape[1]
    nseq = DECODE_SEQS

    def seqs(*dims):
        return pl.BlockSpec((nseq,) + dims, lambda b: (b,) + (0,) * len(dims))

    return pl.pallas_call(
        functools.partial(_attn_decode_kernel, tq=seq, nseq=nseq),
        grid=(bsz // nseq,),
        in_specs=[pl.BlockSpec(memory_space=pltpu.SMEM), seqs(seq, D_MODEL),
                  seqs(WINDOW, KV_WIDTH), seqs(seq, KV_WIDTH), seqs(WINDOW, KV_WIDTH), seqs(seq, KV_WIDTH)],
        out_specs=seqs(seq, D_MODEL),
        out_shape=jax.ShapeDtypeStruct((bsz, seq, D_MODEL), F32),
        scratch_shapes=[pltpu.VMEM((nseq, 2 * WINDOW, KV_WIDTH), F32),
                        pltpu.VMEM((nseq, 2 * WINDOW, KV_WIDTH), F32)],
        compiler_params=_params("parallel"),
        name="swa_decode",
    )(sinks.astype(F32), q, k_cache, k_new, v_cache, v_new)


def _trunk_prompt(x, ssm_in, conv_in, w):
    bsz, seq = x.shape[0], x.shape[1]
    x = x.reshape(bsz * seq, D_MODEL)
    x = _ffn(x, w["ffn1_norm"][0], w["ffn1_w_gu"][0], w["ffn1_w_down"][0])
    x, ssm_out, conv_out = _mixer0(x, conv_in, ssm_in, w, bsz, seq)
    x_kv = _ffn(x, w["ffn2_norm"][0], w["ffn2_w_gu"][0], w["ffn2_w_down"][0])
    x = _ffn(x_kv, w["ffn1_norm"][1], w["ffn1_w_gu"][1], w["ffn1_w_down"][1])
    x, k_win, v_win = _mixer1(x, x_kv, w, bsz, seq)
    y = _ffn(x, w["ffn2_norm"][1], w["ffn2_w_gu"][1], w["ffn2_w_down"][1], fg=w["final_norm"])
    return (y.reshape(bsz, seq, D_MODEL), ssm_out[None], conv_out[None],
            k_win.reshape(bsz, WINDOW, N_KV_HEADS, ATT_HEAD_DIM),
            v_win.reshape(bsz, WINDOW, N_KV_HEADS, ATT_HEAD_DIM))


def _trunk_sample(x, ssm_in, conv_in, k_buf, v_buf, w):
    bsz, seq = x.shape[0], x.shape[1]
    x = x.reshape(bsz * seq, D_MODEL)

    x = _ffn(x, w["ffn1_norm"][0], w["ffn1_w_gu"][0], w["ffn1_w_down"][0])
    z, xbc, dt = _norm_proj(x, w["mix_norm"][0], [w["ssm_w_z"], w["ssm_w_xbc"], w["ssm_w_dt"]])
    y, ssm_out, conv_out = _ssd_decode(xbc, z, dt, conv_in, ssm_in, w["ssm"], bsz, seq)
    x = _proj_res(y, w["ssm_w_out"], x)
    x = _ffn(x, w["ffn2_norm"][0], w["ffn2_w_gu"][0], w["ffn2_w_down"][0])

    k_new, v_new = _norm_proj(x, w["kv_norm"], [w["w_k"], w["w_v"]])
    k_new = k_new.reshape(bsz, seq, KV_WIDTH)
    v_new = v_new.reshape(bsz, seq, KV_WIDTH)

    x = _ffn(x, w["ffn1_norm"][1], w["ffn1_w_gu"][1], w["ffn1_w_down"][1])
    (q,) = _norm_proj(x, w["mix_norm"][1], [w["attn_w_q"]])
    k_buf = k_buf.reshape(bsz, WINDOW, KV_WIDTH)
    v_buf = v_buf.reshape(bsz, WINDOW, KV_WIDTH)
    o = _attention_decode(q.reshape(bsz, seq, D_MODEL), k_buf, k_new, v_buf, v_new, w["attn_sinks"])
    k_win = jnp.concatenate([k_buf, k_new], axis=1)[:, -WINDOW:]
    v_win = jnp.concatenate([v_buf, v_new], axis=1)[:, -WINDOW:]
    x = _proj_res(o.reshape(bsz * seq, D_MODEL), w["attn_w_o"], x)
    y = _ffn(x, w["ffn2_norm"][1], w["ffn2_w_gu"][1], w["ffn2_w_down"][1], fg=w["final_norm"])

    return (y.reshape(bsz, seq, D_MODEL), ssm_out[None], conv_out[None],
            k_win.reshape(bsz, WINDOW, N_KV_HEADS, ATT_HEAD_DIM),
            v_win.reshape(bsz, WINDOW, N_KV_HEADS, ATT_HEAD_DIM))


def kernel(x_prompt, x_sample, state_ssm, state_conv, cache_k_win, cache_v_win,
           ffn1_norm, ffn1_w_gu, ffn1_w_down, mix_norm, ffn2_norm, ffn2_w_gu, ffn2_w_down,
           ssm_w_in, ssm_conv_w, ssm_conv_b, ssm_dt_bias, ssm_a_log, ssm_d, ssm_gate_norm, ssm_w_out,
           kv_norm, w_kv, attn_w_q, attn_sinks, attn_w_o, final_norm):
    w_in = ssm_w_in[0]
    w_dt = jnp.pad(w_in[:, D_INNER + CONV_DIM:], ((0, 0), (0, LANES - SSM_HEADS)))
    w = dict(
        ffn1_norm=ffn1_norm, ffn2_norm=ffn2_norm, mix_norm=mix_norm, kv_norm=kv_norm,
        final_norm=final_norm, attn_sinks=attn_sinks[0],
        ffn1_w_gu=ffn1_w_gu.astype(BF16), ffn1_w_down=ffn1_w_down.astype(BF16),
        ffn2_w_gu=ffn2_w_gu.astype(BF16), ffn2_w_down=ffn2_w_down.astype(BF16),
        ssm_w_z=w_in[:, :D_INNER].astype(BF16),
        ssm_w_xbc=w_in[:, D_INNER:D_INNER + CONV_DIM].astype(BF16),
        ssm_w_dt=w_dt.astype(BF16),
        ssm_w_dtt=w_in[:, D_INNER + CONV_DIM:].T.astype(BF16),
        ssm_w_out=ssm_w_out[0].astype(BF16),
        w_k=w_kv[:, :KV_WIDTH].astype(BF16), w_v=w_kv[:, KV_WIDTH:].astype(BF16),
        attn_w_q=attn_w_q[0].astype(BF16), attn_w_o=attn_w_o[0].astype(BF16),
        ssm=dict(conv_w=ssm_conv_w[0], conv_b=ssm_conv_b[0], dt_bias=ssm_dt_bias[0],
                 a_log=ssm_a_log[0], d_skip=ssm_d[0], gate_norm=ssm_gate_norm[0]),
    )
    bp = x_prompt.shape[0]
    ssm0 = jnp.zeros((bp, SSM_HEADS, SSM_HEAD_DIM, SSM_D_STATE), F32)
    conv0 = jnp.zeros((bp, CONV_WIDTH - 1, CONV_DIM), F32)
    y_p, ssm_p, conv_p, kw_p, vw_p = _trunk_prompt(x_prompt, ssm0, conv0, w)
    y_s, ssm_s, conv_s, kw_s, vw_s = _trunk_sample(x_sample, state_ssm[0], state_conv[0],
                                                   cache_k_win, cache_v_win, w)
    return (y_p, y_s, ssm_p, conv_p, kw_p, vw_p, ssm_s, conv_s, kw_s, vw_s)
```

```python
import functools
import math

import jax
import jax.numpy as jnp
import numpy as np
from jax import lax
from jax.experimental import pallas as pl
from jax.experimental.pallas import tpu as pltpu

F32 = jnp.float32
BF16 = jnp.bfloat16

D_MODEL = 1024
D_FF = 2816
D_INNER = 2048
SSM_HEAD_DIM = 64
SSM_HEADS = 32
SSM_GROUPS = 4
HEADS_PER_GROUP = SSM_HEADS // SSM_GROUPS
SSM_D_STATE = 128
GROUP_WIDTH = D_INNER // SSM_GROUPS
CONV_WIDTH = 4
CONV_DIM = D_INNER + 2 * SSM_GROUPS * SSM_D_STATE
SSD_CHUNK = 128
WINDOW = 128
ATT_HEAD_DIM = 64
N_Q_HEADS = 16
N_KV_HEADS = 4
Q_PER_KV = N_Q_HEADS // N_KV_HEADS
KV_WIDTH = N_KV_HEADS * ATT_HEAD_DIM
PAST_LEN = 8192
NORM_EPS = 1e-5
LOG2E = 1.0 / math.log(2.0)

LANES = 128
SUBLANES = 8
VMEM_LIMIT_BYTES = 56 * 1024 * 1024
FF_CHUNK = 256
ROW_TILE = 1024
PROJ_ROW_TILE = 512
PROJ_COL_CHUNK = 512
MIX_ROWS = 256
DECODE_SEQS = 4

_NT = (((1,), (1,)), ((), ()))
_TN = (((0,), (0,)), ((), ()))


def _rms(x, g):
    ms = jnp.mean(x * x, axis=-1, keepdims=True)
    return x * lax.rsqrt(ms + NORM_EPS) * g


def _silu(x):
    return x * (1.0 / (1.0 + jnp.exp(-x)))


def _softplus(x):
    return jnp.maximum(x, 0.0) + jnp.log1p(jnp.exp(-jnp.abs(x)))


def _params(*sem):
    return pltpu.CompilerParams(dimension_semantics=sem, vmem_limit_bytes=VMEM_LIMIT_BYTES)


def _full(shape):
    return pl.BlockSpec(shape, lambda *_: (0,) * len(shape))


def _ffn_kernel(x_ref, g_ref, wgu_ref, wd_ref, fg_ref, o_ref, acc_ref, *, final_norm):
    x = x_ref[...]
    h = _rms(x, g_ref[...]).astype(BF16)
    for j in range(D_FF // FF_CHUNK):
        lo = j * FF_CHUNK
        gate = jnp.dot(h, wgu_ref[:, lo:lo + FF_CHUNK], preferred_element_type=F32)
        up = jnp.dot(h, wgu_ref[:, D_FF + lo:D_FF + lo + FF_CHUNK], preferred_element_type=F32)
        act = (_silu(gate) * up).astype(BF16)
        part = jnp.dot(act, wd_ref[lo:lo + FF_CHUNK, :], preferred_element_type=F32)
        if j == 0:
            acc_ref[...] = part
        else:
            acc_ref[...] += part
    out = x + 0.5 * acc_ref[...]
    if final_norm:
        out = _rms(out, fg_ref[...])
    o_ref[...] = out


def _ffn(x, g, wgu, wd, layer, fg=None):
    t = x.shape[0]
    tm = min(ROW_TILE, t)
    final_norm = fg is not None
    g = g[layer]
    if fg is None:
        fg = g

    def layer_weights(*dims):
        return pl.BlockSpec((None,) + dims, lambda i: (layer, 0, 0), pipeline_mode=pl.Buffered(1))

    return pl.pallas_call(
        functools.partial(_ffn_kernel, final_norm=final_norm),
        grid=(t // tm,),
        in_specs=[pl.BlockSpec((tm, D_MODEL), lambda i: (i, 0)),
                  _full((1, D_MODEL)), layer_weights(D_MODEL, 2 * D_FF), layer_weights(D_FF, D_MODEL),
                  _full((1, D_MODEL))],
        out_specs=pl.BlockSpec((tm, D_MODEL), lambda i: (i, 0)),
        out_shape=jax.ShapeDtypeStruct((t, D_MODEL), F32),
        scratch_shapes=[pltpu.VMEM((tm, D_MODEL), F32)],
        compiler_params=_params("parallel"),
        name="ffn",
    )(x, g.reshape(1, D_MODEL), wgu, wd, fg.reshape(1, D_MODEL))


def _norm_proj_kernel(x_ref, g_ref, *refs):
    n = len(refs) // 2
    h = _rms(x_ref[...], g_ref[...]).astype(BF16)
    for w_ref, o_ref in zip(refs[:n], refs[n:]):
        width = w_ref.shape[1]
        step = min(PROJ_COL_CHUNK, width)
        for lo in range(0, width, step):
            o_ref[:, lo:lo + step] = jnp.dot(h, w_ref[:, lo:lo + step], preferred_element_type=F32)


def _norm_proj(x, g, weights):
    t = x.shape[0]
    tm = min(PROJ_ROW_TILE, t)
    return pl.pallas_call(
        _norm_proj_kernel,
        grid=(t // tm,),
        in_specs=[pl.BlockSpec((tm, D_MODEL), lambda i: (i, 0)), _full((1, D_MODEL))]
        + [_full(w.shape) for w in weights],
        out_specs=[pl.BlockSpec((tm, w.shape[1]), lambda i: (i, 0)) for w in weights],
        out_shape=[jax.ShapeDtypeStruct((t, w.shape[1]), F32) for w in weights],
        compiler_params=_params("parallel"),
        name="norm_proj",
    )(x, g.reshape(1, D_MODEL), *weights)


def _proj_res_kernel(y_ref, w_ref, x_ref, o_ref):
    o_ref[...] = x_ref[...] + jnp.dot(y_ref[...].astype(BF16), w_ref[...], preferred_element_type=F32)


def _proj_res(y, w, x):
    t, k = y.shape
    tm = min(PROJ_ROW_TILE, t)
    return pl.pallas_call(
        _proj_res_kernel,
        grid=(t // tm,),
        in_specs=[pl.BlockSpec((tm, k), lambda i: (i, 0)), _full(w.shape),
                  pl.BlockSpec((tm, D_MODEL), lambda i: (i, 0))],
        out_specs=pl.BlockSpec((tm, D_MODEL), lambda i: (i, 0)),
        out_shape=jax.ShapeDtypeStruct((t, D_MODEL), F32),
        compiler_params=_params("parallel"),
        name="proj_res",
    )(y, w, x)


def _split3(x):
    hi = x.astype(BF16)
    r1 = x - hi.astype(F32)
    mid = r1.astype(BF16)
    lo = (r1 - mid.astype(F32)).astype(BF16)
    return hi, mid, lo


def _conv_silu(ext_scr, n, cw_ref, cb_ref):
    ext = ext_scr[...]
    tiles3 = ext.reshape(n // SUBLANES + 1, SUBLANES, CONV_DIM)
    sub = lax.broadcasted_iota(jnp.int32, (1, SUBLANES, CONV_DIM), 1)
    taps = []
    for k in range(CONV_WIDTH - 1):
        d = CONV_WIDTH - 1 - k
        rot = pltpu.roll(tiles3, d, 1)
        shifted = jnp.where(sub < d, rot[:-1], rot[1:]).reshape(n, CONV_DIM)
        taps.append(shifted * cw_ref[k:k + 1, :])
    taps.append(ext[SUBLANES:] * cw_ref[CONV_WIDTH - 1:CONV_WIDTH, :])
    xc = _silu((((taps[0] + taps[1]) + taps[2]) + taps[3]) + cb_ref[...])
    return xc, ext[n:n + SUBLANES]


def _ssd_chunk(xc, zg, dtt, state_scr, dtb_ref, alog_ref, dskip_ref, gn_ref, spread_ref,
               pad_scr, lt, ls, emit):
    def pad_rows(v, k):
        if ls == lt:
            return v
        pad_scr[k][...] = jnp.zeros(pad_scr[k].shape, F32)
        pad_scr[k][0:lt, :] = v
        return pad_scr[k][...]

    xs = xc[:, :D_INNER]
    bm = xc[:, D_INNER:D_INNER + GROUP_WIDTH]
    cm = xc[:, D_INNER + GROUP_WIDTH:]

    dt_r = _softplus(dtt + dtb_ref[...])
    da_r = dt_r * -jnp.exp(alog_ref[...])
    upper = (lax.broadcasted_iota(jnp.int32, (ls, ls), 0)
             <= lax.broadcasted_iota(jnp.int32, (ls, ls), 1)).astype(F32).astype(BF16)
    parts = jnp.dot(jnp.concatenate(_split3(da_r), axis=0), upper, preferred_element_type=F32)
    acs_r = (parts[0:SSM_HEADS] + parts[SSM_HEADS:2 * SSM_HEADS]) + parts[2 * SSM_HEADS:]
    last_r = acs_r[:, lt - 1:lt]
    c2_r = acs_r * LOG2E
    r2_r = jnp.log(dt_r) * LOG2E - c2_r
    w_r = dt_r * jnp.exp(last_r - acs_r)
    e_r = jnp.exp(acs_r)
    chunk_decay = jnp.exp(jnp.broadcast_to(last_r, (SSM_HEADS, LANES)))
    col = jnp.concatenate([c2_r, w_r, e_r, jnp.zeros((LANES - 3 * SSM_HEADS, ls), F32)],
                          axis=0).T[0:lt]
    wide = jnp.dot(jnp.concatenate(_split3(col), axis=1), spread_ref[...],
                   preferred_element_type=F32)
    w_x = wide[:, :D_INNER]
    e_x = wide[:, D_INNER:]

    causal = (lax.broadcasted_iota(jnp.int32, (lt, ls), 1)
              <= lax.broadcasted_iota(jnp.int32, (lt, ls), 0))
    low_half = lax.broadcasted_iota(jnp.int32, (lt, LANES), 1) < SSM_HEAD_DIM
    xs_s = pad_rows(xs, 0)
    bm_s = pad_rows(bm, 1)
    xd_s = pad_rows(xs * w_x, 2)

    for g in range(SSM_GROUPS):
        cm_g = cm[:, g * SSM_D_STATE:(g + 1) * SSM_D_STATE].astype(BF16)
        bm_g = bm_s[:, g * SSM_D_STATE:(g + 1) * SSM_D_STATE].astype(BF16)
        cb_g = lax.dot_general(cm_g, bm_g, _NT, preferred_element_type=F32)
        rows = slice(g * GROUP_WIDTH, (g + 1) * GROUP_WIDTH)
        y_off = lax.dot_general(cm_g, state_scr[rows, :].astype(BF16), _NT,
                                preferred_element_type=F32)
        tiles = []
        for jj in range(HEADS_PER_GROUP // 2):
            tile = g * (HEADS_PER_GROUP // 2) + jj
            cols = slice(tile * LANES, (tile + 1) * LANES)
            xp = xs_s[:, cols].astype(BF16)
            pair = []
            for h in (2 * tile, 2 * tile + 1):
                expo = col[:, h:h + 1] + r2_r[h:h + 1, :]
                mat = (cb_g * jnp.exp2(jnp.where(causal, expo, -jnp.inf))).astype(BF16)
                pair.append(jnp.dot(mat, xp, preferred_element_type=F32))
            y_diag = jnp.where(low_half, pair[0], pair[1])
            tiles.append(y_diag + y_off[:, jj * LANES:(jj + 1) * LANES] * e_x[:, cols]
                         + dskip_ref[:, cols] * xs[:, cols])
        yg = jnp.concatenate(tiles, axis=1) * _silu(zg[:, rows])
        ms = jnp.mean(yg * yg, axis=-1, keepdims=True)
        emit(g, yg * lax.rsqrt(ms + NORM_EPS) * gn_ref[:, rows])

        upd = lax.dot_general(xd_s[:, rows].astype(BF16), bm_g, _TN, preferred_element_type=F32)
        for hh in range(HEADS_PER_GROUP):
            h = g * HEADS_PER_GROUP + hh
            hrows = slice(h * SSM_HEAD_DIM, (h + 1) * SSM_HEAD_DIM)
            state_scr[hrows, :] = (state_scr[hrows, :] * chunk_decay[h:h + 1, :]
                                   + upd[hh * SSM_HEAD_DIM:(hh + 1) * SSM_HEAD_DIM, :])


def _ssd_decode_kernel(xbc_ref, z_ref, dtt_ref, conv0_ref, ssm0_ref,
                       cw_ref, cb_ref, dtb_ref, alog_ref, dskip_ref, gn_ref, spread_ref,
                       y_ref, ssm_ref, conv_ref, ext_scr, *pad_scr, lt, ls, nseq):
    for i in range(nseq):
        ext_i = ext_scr.at[i]
        ext_i[0:SUBLANES, :] = conv0_ref[i]
        ext_i[SUBLANES:SUBLANES + lt, :] = xbc_ref[i]
        xc, tail = _conv_silu(ext_i, lt, cw_ref, cb_ref)
        conv_ref[i] = tail
        ssm_ref[i] = ssm0_ref[i]

        def emit(g, y, i=i):
            y_ref[i, :, g * GROUP_WIDTH:(g + 1) * GROUP_WIDTH] = y

        _ssd_chunk(xc, z_ref[i], dtt_ref[i], ssm_ref.at[i], dtb_ref, alog_ref, dskip_ref, gn_ref,
                   spread_ref, [p.at[i] for p in pad_scr], lt, ls, emit)


def _mixer0_kernel(x_ref, g_ref, wz_ref, wxbc_ref, wdtt_ref, wout_ref, conv0_ref, ssm0_ref,
                   cw_ref, cb_ref, dtb_ref, alog_ref, dskip_ref, gn_ref, spread_ref,
                   o_ref, ssm_ref, conv_ref,
                   ext_scr, state_scr, z_scr, y_scr, *, rb, nsteps):
    c = pl.program_id(1)

    @pl.when(c == 0)
    def _():
        ext_scr[0:SUBLANES, :] = conv0_ref[...]
        state_scr[...] = ssm0_ref[...]

    x = x_ref[...]
    h = _rms(x, g_ref[...]).astype(BF16)
    for lo in range(0, CONV_DIM, PROJ_COL_CHUNK):
        ext_scr[SUBLANES:SUBLANES + rb, lo:lo + PROJ_COL_CHUNK] = jnp.dot(
            h, wxbc_ref[:, lo:lo + PROJ_COL_CHUNK], preferred_element_type=F32)
    for lo in range(0, D_INNER, PROJ_COL_CHUNK):
        z_scr[:, lo:lo + PROJ_COL_CHUNK] = jnp.dot(
            h, wz_ref[:, lo:lo + PROJ_COL_CHUNK], preferred_element_type=F32)
    dtt = lax.dot_general(wdtt_ref[...], h, _NT, preferred_element_type=F32)

    xc, tail = _conv_silu(ext_scr, rb, cw_ref, cb_ref)
    conv_ref[...] = tail
    ext_scr[0:SUBLANES, :] = tail

    for k in range(rb // SSD_CHUNK):
        rows = slice(k * SSD_CHUNK, (k + 1) * SSD_CHUNK)

        def emit(g, y, rows=rows):
            y_scr[rows, g * GROUP_WIDTH:(g + 1) * GROUP_WIDTH] = y.astype(BF16)

        _ssd_chunk(xc[rows], z_scr[rows, :], dtt[:, rows], state_scr, dtb_ref, alog_ref,
                   dskip_ref, gn_ref, spread_ref, (), SSD_CHUNK, SSD_CHUNK, emit)

    o_ref[...] = x + jnp.dot(y_scr[...], wout_ref[...], preferred_element_type=F32)

    @pl.when(c == nsteps - 1)
    def _():
        ssm_ref[...] = state_scr[...]


def _spread_matrix():
    k = np.arange(LANES)[:, None]
    head = np.arange(D_INNER)[None, :] // SSM_HEAD_DIM
    one = np.concatenate([k == SSM_HEADS + head, k == 2 * SSM_HEADS + head], axis=1)
    return jnp.asarray(np.concatenate([one, one, one], axis=0), dtype=BF16)


def _ssd_decode(xbc, z, dt, conv0, ssm0, p, bsz, seq):
    lt, ls, nseq = seq, LANES, DECODE_SEQS
    xbc = xbc.reshape(bsz, seq, CONV_DIM)
    z = z.reshape(bsz, seq, D_INNER)
    dtt = jnp.swapaxes(dt.reshape(bsz, seq, LANES)[:, :, :SSM_HEADS], 1, 2)
    dtt = jnp.pad(dtt, ((0, 0), (0, 0), (0, ls - lt)))
    conv0 = jnp.pad(conv0, ((0, 0), (SUBLANES - (CONV_WIDTH - 1), 0), (0, 0)))
    ssm0 = ssm0.reshape(bsz, D_INNER, SSM_D_STATE)

    def head_rows(v):
        return jnp.broadcast_to(v.astype(F32)[:, None], (SSM_HEADS, ls))

    def seqs(*dims):
        return pl.BlockSpec((nseq,) + dims, lambda b: (b,) + (0,) * len(dims))

    y, ssm, conv = pl.pallas_call(
        functools.partial(_ssd_decode_kernel, lt=lt, ls=ls, nseq=nseq),
        grid=(bsz // nseq,),
        in_specs=[seqs(lt, CONV_DIM), seqs(lt, D_INNER), seqs(SSM_HEADS, ls),
                  seqs(SUBLANES, CONV_DIM), seqs(D_INNER, SSM_D_STATE),
                  _full((CONV_WIDTH, CONV_DIM)), _full((1, CONV_DIM)),
                  _full((SSM_HEADS, ls)), _full((SSM_HEADS, ls)),
                  _full((1, D_INNER)), _full((1, D_INNER)),
                  _full((3 * LANES, 2 * D_INNER))],
        out_specs=[seqs(lt, D_INNER), seqs(D_INNER, SSM_D_STATE), seqs(SUBLANES, CONV_DIM)],
        out_shape=[jax.ShapeDtypeStruct((bsz, seq, D_INNER), F32),
                   jax.ShapeDtypeStruct((bsz, D_INNER, SSM_D_STATE), F32),
                   jax.ShapeDtypeStruct((bsz, SUBLANES, CONV_DIM), F32)],
        scratch_shapes=[pltpu.VMEM((nseq, SUBLANES + lt, CONV_DIM), F32),
                        pltpu.VMEM((nseq, ls, D_INNER), F32), pltpu.VMEM((nseq, ls, GROUP_WIDTH), F32),
                        pltpu.VMEM((nseq, ls, D_INNER), F32)],
        compiler_params=_params("parallel"),
        name="ssd_decode",
    )(xbc, z, dtt, conv0, ssm0,
      p["conv_w"], p["conv_b"].reshape(1, CONV_DIM),
      head_rows(p["dt_bias"]), head_rows(p["a_log"]),
      jnp.repeat(p["d_skip"].astype(F32), SSM_HEAD_DIM).reshape(1, D_INNER),
      p["gate_norm"].reshape(1, D_INNER), _spread_matrix())
    return (y.reshape(bsz * seq, D_INNER),
            ssm.reshape(bsz, SSM_HEADS, SSM_HEAD_DIM, SSM_D_STATE),
            conv[:, SUBLANES - (CONV_WIDTH - 1):, :])


def _mixer0(x, conv0, ssm0, w, bsz, seq):
    rb = MIX_ROWS
    nsteps = seq // rb
    p = w["ssm"]
    conv0 = jnp.pad(conv0, ((0, 0), (SUBLANES - (CONV_WIDTH - 1), 0), (0, 0)))
    ssm0 = ssm0.reshape(bsz, D_INNER, SSM_D_STATE)

    def head_rows(v):
        return jnp.broadcast_to(v.astype(F32)[:, None], (SSM_HEADS, SSD_CHUNK))

    row_spec = pl.BlockSpec((rb, D_MODEL), lambda b, c: (b * nsteps + c, 0))
    out, ssm, conv = pl.pallas_call(
        functools.partial(_mixer0_kernel, rb=rb, nsteps=nsteps),
        grid=(bsz, nsteps),
        in_specs=[row_spec,
                  _full((1, D_MODEL)), _full((D_MODEL, D_INNER)), _full((D_MODEL, CONV_DIM)),
                  _full((SSM_HEADS, D_MODEL)), _full((D_INNER, D_MODEL)),
                  pl.BlockSpec((None, SUBLANES, CONV_DIM), lambda b, c: (b, 0, 0)),
                  pl.BlockSpec((None, D_INNER, SSM_D_STATE), lambda b, c: (b, 0, 0)),
                  _full((CONV_WIDTH, CONV_DIM)), _full((1, CONV_DIM)),
                  _full((SSM_HEADS, SSD_CHUNK)), _full((SSM_HEADS, SSD_CHUNK)),
                  _full((1, D_INNER)), _full((1, D_INNER)),
                  _full((3 * LANES, 2 * D_INNER))],
        out_specs=[row_spec,
                   pl.BlockSpec((None, D_INNER, SSM_D_STATE), lambda b, c: (b, 0, 0)),
                   pl.BlockSpec((None, SUBLANES, CONV_DIM), lambda b, c: (b, 0, 0))],
        out_shape=[jax.ShapeDtypeStruct((bsz * seq, D_MODEL), F32),
                   jax.ShapeDtypeStruct((bsz, D_INNER, SSM_D_STATE), F32),
                   jax.ShapeDtypeStruct((bsz, SUBLANES, CONV_DIM), F32)],
        scratch_shapes=[pltpu.VMEM((SUBLANES + rb, CONV_DIM), F32),
                        pltpu.VMEM((D_INNER, SSM_D_STATE), F32),
                        pltpu.VMEM((rb, D_INNER), F32),
                        pltpu.VMEM((rb, D_INNER), BF16)],
        compiler_params=_params("parallel", "arbitrary"),
        name="mixer0",
    )(x, w["mix_norm"][0].reshape(1, D_MODEL), w["ssm_w_z"], w["ssm_w_xbc"], w["ssm_w_dtt"],
      w["ssm_w_out"], conv0, ssm0,
      p["conv_w"], p["conv_b"].reshape(1, CONV_DIM),
      head_rows(p["dt_bias"]), head_rows(p["a_log"]),
      jnp.repeat(p["d_skip"].astype(F32), SSM_HEAD_DIM).reshape(1, D_INNER),
      p["gate_norm"].reshape(1, D_INNER), _spread_matrix())
    return (out, ssm.reshape(bsz, SSM_HEADS, SSM_HEAD_DIM, SSM_D_STATE),
            conv[:, SUBLANES - (CONV_WIDTH - 1):, :])


def _alibi_slope(head):
    return 2.0 ** (-8.0 * (head + 1) / N_Q_HEADS)


def _attn_block(q, keys, vals, sink_ref, first_key_pos, tq):
    nkeys = 2 * WINDOW
    rows = lax.broadcasted_iota(jnp.int32, (Q_PER_KV * tq, nkeys), 0)
    kj = lax.broadcasted_iota(jnp.int32, (Q_PER_KV * tq, nkeys), 1)
    t = rows & (tq - 1)
    sub = lax.shift_right_logical(rows, int(math.log2(tq)))
    dist = WINDOW + t - kj
    valid = (dist >= 0) & (dist < WINDOW) & (first_key_pos + kj >= 0)
    distf = dist.astype(F32)
    low_half = lax.broadcasted_iota(jnp.int32, (nkeys, LANES), 1) < ATT_HEAD_DIM

    def pick(choices):
        return jnp.where(sub == 0, choices[0], jnp.where(sub == 1, choices[1],
                                                         jnp.where(sub == 2, choices[2], choices[3])))

    out = []
    for j in range(N_KV_HEADS):
        cols = slice((j // 2) * LANES, (j // 2 + 1) * LANES)
        kt = keys[:, cols]
        vt = vals[:, cols]
        if j % 2 == 0:
            k_lo = jnp.where(low_half, kt, 0.0)
            v_lo = jnp.where(low_half, vt, 0.0)
        else:
            k_lo = pltpu.roll(jnp.where(low_half, 0.0, kt), ATT_HEAD_DIM, 1)
            v_lo = pltpu.roll(jnp.where(low_half, 0.0, vt), ATT_HEAD_DIM, 1)
        q0 = q[:, (2 * j) * LANES:(2 * j + 1) * LANES] * (ATT_HEAD_DIM ** -0.5)
        q1 = q[:, (2 * j + 1) * LANES:(2 * j + 2) * LANES] * (ATT_HEAD_DIM ** -0.5)
        qs = jnp.concatenate([q0, pltpu.roll(q0, ATT_HEAD_DIM, 1),
                              q1, pltpu.roll(q1, ATT_HEAD_DIM, 1)], axis=0)
        s = lax.dot_general(qs.astype(BF16), k_lo.astype(BF16), _NT, preferred_element_type=F32)
        slope = pick([_alibi_slope(Q_PER_KV * j + g) for g in range(Q_PER_KV)])
        sink = pick([sink_ref[Q_PER_KV * j + g] for g in range(Q_PER_KV)])[:, 0:1]
        s = jnp.where(valid, s - slope * distf, -jnp.inf)
        mx = jnp.maximum(jnp.max(s, axis=-1, keepdims=True), sink)
        e = jnp.exp(s - mx)
        den = jnp.sum(e, axis=-1, keepdims=True) + jnp.exp(sink - mx)
        prob = (e * (1.0 / den)).astype(BF16)
        o = jnp.dot(prob, v_lo.astype(BF16), preferred_element_type=F32)
        out.append(o[0:tq] + pltpu.roll(o[tq:2 * tq], ATT_HEAD_DIM, 1))
        out.append(o[2 * tq:3 * tq] + pltpu.roll(o[3 * tq:4 * tq], ATT_HEAD_DIM, 1))
    return out


def _attn_decode_kernel(q_ref, kp_ref, kc_ref, vp_ref, vc_ref, bias_ref, sink_ref,
                        o_ref, kwin_ref, vwin_ref, k_scr, v_scr, *, tq, nseq):
    k_scr[...] = jnp.zeros(k_scr.shape, F32)
    v_scr[...] = jnp.zeros(v_scr.shape, F32)
    low_half = lax.broadcasted_iota(jnp.int32, (tq, LANES), 1) < ATT_HEAD_DIM
    zeros = jnp.zeros((tq, LANES), F32)

    def half(tile, upper):
        return jnp.where(low_half, 0.0, tile) if upper else jnp.where(low_half, tile, 0.0)

    for i in range(nseq):
        k_scr[i, 0:WINDOW, :] = kp_ref[i]
        v_scr[i, 0:WINDOW, :] = vp_ref[i]
        k_scr[i, WINDOW:WINDOW + tq, :] = kc_ref[i]
        v_scr[i, WINDOW:WINDOW + tq, :] = vc_ref[i]
        kwin_ref[i] = k_scr[i, tq:tq + WINDOW, :]
        vwin_ref[i] = v_scr[i, tq:tq + WINDOW, :]
        q = q_ref[i] * (ATT_HEAD_DIM ** -0.5)
        blocks = []
        for h in range(N_Q_HEADS):
            j = h // Q_PER_KV
            piece = half(q[:, (h // 2) * LANES:(h // 2 + 1) * LANES], h % 2 == 1)
            if h % 2 != j % 2:
                piece = pltpu.roll(piece, ATT_HEAD_DIM, 1)
            blocks.append(jnp.concatenate([piece, zeros] if j < 2 else [zeros, piece], axis=1))
        q_all = jnp.concatenate(blocks, axis=0).astype(BF16)
        s = lax.dot_general(q_all, k_scr[i].astype(BF16), _NT,
                            preferred_element_type=F32) + bias_ref[...]
        sink = sink_ref[:, 0:1]
        mx = jnp.maximum(jnp.max(s, axis=-1, keepdims=True), sink)
        e = jnp.exp(s - mx)
        den = jnp.sum(e, axis=-1, keepdims=True) + jnp.exp(sink - mx)
        prob = (e * (1.0 / den)).astype(BF16)
        o_all = jnp.dot(prob, v_scr[i].astype(BF16), preferred_element_type=F32)
        for m in range(N_Q_HEADS // 2):
            parts = []
            for h in (2 * m, 2 * m + 1):
                j = h // Q_PER_KV
                piece = half(o_all[h * tq:(h + 1) * tq, (j // 2) * LANES:(j // 2 + 1) * LANES],
                             j % 2 == 1)
                if h % 2 != j % 2:
                    piece = pltpu.roll(piece, ATT_HEAD_DIM, 1)
                parts.append(piece)
            o_ref[i, :, m * LANES:(m + 1) * LANES] = parts[0] + parts[1]


def _mixer1_kernel(sink_ref, x_ref, xkv_ref, gq_ref, gkv_ref, wq_ref, wk_ref, wv_ref, wo_ref,
                   o_ref, kwin_ref, vwin_ref, k_scr, v_scr, att_scr, *, rb):
    c = pl.program_id(1)

    @pl.when(c == 0)
    def _():
        k_scr[0:WINDOW, :] = jnp.zeros((WINDOW, KV_WIDTH), F32)
        v_scr[0:WINDOW, :] = jnp.zeros((WINDOW, KV_WIDTH), F32)

    hkv = _rms(xkv_ref[...], gkv_ref[...]).astype(BF16)
    k_scr[WINDOW:WINDOW + rb, :] = jnp.dot(hkv, wk_ref[...], preferred_element_type=F32)
    v_scr[WINDOW:WINDOW + rb, :] = jnp.dot(hkv, wv_ref[...], preferred_element_type=F32)
    x = x_ref[...]
    q = jnp.dot(_rms(x, gq_ref[...]).astype(BF16), wq_ref[...], preferred_element_type=F32)
    for blk in range(rb // WINDOW):
        lo = blk * WINDOW
        tiles = _attn_block(q[lo:lo + WINDOW], k_scr[lo:lo + 2 * WINDOW, :],
                            v_scr[lo:lo + 2 * WINDOW, :], sink_ref, c * rb + lo - WINDOW, WINDOW)
        for m, tile in enumerate(tiles):
            att_scr[lo:lo + WINDOW, m * LANES:(m + 1) * LANES] = tile.astype(BF16)
    o_ref[...] = x + jnp.dot(att_scr[...], wo_ref[...], preferred_element_type=F32)
    k_last = k_scr[rb:rb + WINDOW, :]
    v_last = v_scr[rb:rb + WINDOW, :]
    kwin_ref[...] = k_last
    vwin_ref[...] = v_last
    k_scr[0:WINDOW, :] = k_last
    v_scr[0:WINDOW, :] = v_last


def _mixer1(x, xkv, w, bsz, seq):
    rb = MIX_ROWS
    nsteps = seq // rb
    row_spec = pl.BlockSpec((rb, D_MODEL), lambda b, c: (b * nsteps + c, 0))
    win_spec = pl.BlockSpec((None, WINDOW, KV_WIDTH), lambda b, c: (b, 0, 0))
    return pl.pallas_call(
        functools.partial(_mixer1_kernel, rb=rb),
        grid=(bsz, nsteps),
        in_specs=[pl.BlockSpec(memory_space=pltpu.SMEM), row_spec, row_spec,
                  _full((1, D_MODEL)), _full((1, D_MODEL)),
                  _full((D_MODEL, D_MODEL)), _full((D_MODEL, KV_WIDTH)), _full((D_MODEL, KV_WIDTH)),
                  _full((D_MODEL, D_MODEL))],
        out_specs=[row_spec, win_spec, win_spec],
        out_shape=[jax.ShapeDtypeStruct((bsz * seq, D_MODEL), F32),
                   jax.ShapeDtypeStruct((bsz, WINDOW, KV_WIDTH), F32),
                   jax.ShapeDtypeStruct((bsz, WINDOW, KV_WIDTH), F32)],
        scratch_shapes=[pltpu.VMEM((WINDOW + rb, KV_WIDTH), F32), pltpu.VMEM((WINDOW + rb, KV_WIDTH), F32),
                        pltpu.VMEM((rb, D_MODEL), BF16)],
        compiler_params=_params("parallel", "arbitrary"),
        name="mixer1",
    )(w["attn_sinks"].astype(F32), x, xkv, w["mix_norm"][1].reshape(1, D_MODEL),
      w["kv_norm"].reshape(1, D_MODEL), w["attn_w_q"], w["w_k"], w["w_v"], w["attn_w_o"])


def _attention_decode(q, k_cache, k_new, v_cache, v_new, sinks):
    bsz, seq = q.shape[0], q.shape[1]
    nseq = DECODE_SEQS

    def seqs(*dims):
        return pl.BlockSpec((nseq,) + dims, lambda b: (b,) + (0,) * len(dims))

    t = np.tile(np.arange(seq), N_Q_HEADS)[:, None]
    kj = np.arange(2 * WINDOW)[None, :]
    dist = WINDOW + t - kj
    valid = (dist >= 0) & (dist < WINDOW) & (PAST_LEN - WINDOW + kj >= 0)
    slope = np.repeat([_alibi_slope(h) for h in range(N_Q_HEADS)], seq)[:, None]
    bias = jnp.asarray(np.where(valid, -slope * dist, -np.inf), dtype=F32)
    sink_rows = jnp.broadcast_to(jnp.repeat(sinks.astype(F32), seq)[:, None], (N_Q_HEADS * seq, LANES))

    return pl.pallas_call(
        functools.partial(_attn_decode_kernel, tq=seq, nseq=nseq),
        grid=(bsz // nseq,),
        in_specs=[seqs(seq, D_MODEL),
                  seqs(WINDOW, KV_WIDTH), seqs(seq, KV_WIDTH), seqs(WINDOW, KV_WIDTH), seqs(seq, KV_WIDTH),
                  _full((N_Q_HEADS * seq, 2 * WINDOW)), _full((N_Q_HEADS * seq, LANES))],
        out_specs=[seqs(seq, D_MODEL), seqs(WINDOW, KV_WIDTH), seqs(WINDOW, KV_WIDTH)],
        out_shape=[jax.ShapeDtypeStruct((bsz, seq, D_MODEL), F32),
                   jax.ShapeDtypeStruct((bsz, WINDOW, KV_WIDTH), F32),
                   jax.ShapeDtypeStruct((bsz, WINDOW, KV_WIDTH), F32)],
        scratch_shapes=[pltpu.VMEM((nseq, 2 * WINDOW, KV_WIDTH), F32),
                        pltpu.VMEM((nseq, 2 * WINDOW, KV_WIDTH), F32)],
        compiler_params=_params("parallel"),
        name="swa_decode",
    )(q, k_cache, k_new, v_cache, v_new, bias, sink_rows)


def _trunk_prompt(x, ssm_in, conv_in, w):
    bsz, seq = x.shape[0], x.shape[1]
    x = x.reshape(bsz * seq, D_MODEL)
    x = _ffn(x, w["ffn1_norm"], w["ffn1_w_gu"], w["ffn1_w_down"], 0)
    x, ssm_out, conv_out = _mixer0(x, conv_in, ssm_in, w, bsz, seq)
    x_kv = _ffn(x, w["ffn2_norm"], w["ffn2_w_gu"], w["ffn2_w_down"], 0)
    x = _ffn(x_kv, w["ffn1_norm"], w["ffn1_w_gu"], w["ffn1_w_down"], 1)
    x, k_win, v_win = _mixer1(x, x_kv, w, bsz, seq)
    y = _ffn(x, w["ffn2_norm"], w["ffn2_w_gu"], w["ffn2_w_down"], 1, fg=w["final_norm"])
    return (y.reshape(bsz, seq, D_MODEL), ssm_out[None], conv_out[None],
            k_win.reshape(bsz, WINDOW, N_KV_HEADS, ATT_HEAD_DIM),
            v_win.reshape(bsz, WINDOW, N_KV_HEADS, ATT_HEAD_DIM))


def _trunk_sample(x, ssm_in, conv_in, k_buf, v_buf, w):
    bsz, seq = x.shape[0], x.shape[1]
    x = x.reshape(bsz * seq, D_MODEL)

    x = _ffn(x, w["ffn1_norm"], w["ffn1_w_gu"], w["ffn1_w_down"], 0)
    z, xbc, dt = _norm_proj(x, w["mix_norm"][0], [w["ssm_w_z"], w["ssm_w_xbc"], w["ssm_w_dt"]])
    y, ssm_out, conv_out = _ssd_decode(xbc, z, dt, conv_in, ssm_in, w["ssm"], bsz, seq)
    x = _proj_res(y, w["ssm_w_out"], x)
    x = _ffn(x, w["ffn2_norm"], w["ffn2_w_gu"], w["ffn2_w_down"], 0)

    k_new, v_new = _norm_proj(x, w["kv_norm"], [w["w_k"], w["w_v"]])
    k_new = k_new.reshape(bsz, seq, KV_WIDTH)
    v_new = v_new.reshape(bsz, seq, KV_WIDTH)

    x = _ffn(x, w["ffn1_norm"], w["ffn1_w_gu"], w["ffn1_w_down"], 1)
    (q,) = _norm_proj(x, w["mix_norm"][1], [w["attn_w_q"]])
    k_buf = k_buf.reshape(bsz, WINDOW, KV_WIDTH)
    v_buf = v_buf.reshape(bsz, WINDOW, KV_WIDTH)
    o, k_win, v_win = _attention_decode(q.reshape(bsz, seq, D_MODEL), k_buf, k_new, v_buf, v_new,
                                        w["attn_sinks"])
    x = _proj_res(o.reshape(bsz * seq, D_MODEL), w["attn_w_o"], x)
    y = _ffn(x, w["ffn2_norm"], w["ffn2_w_gu"], w["ffn2_w_down"], 1, fg=w["final_norm"])

    return (y.reshape(bsz, seq, D_MODEL), ssm_out[None], conv_out[None],
            k_win.reshape(bsz, WINDOW, N_KV_HEADS, ATT_HEAD_DIM),
            v_win.reshape(bsz, WINDOW, N_KV_HEADS, ATT_HEAD_DIM))


def kernel(x_prompt, x_sample, state_ssm, state_conv, cache_k_win, cache_v_win,
           ffn1_norm, ffn1_w_gu, ffn1_w_down, mix_norm, ffn2_norm, ffn2_w_gu, ffn2_w_down,
           ssm_w_in, ssm_conv_w, ssm_conv_b, ssm_dt_bias, ssm_a_log, ssm_d, ssm_gate_norm, ssm_w_out,
           kv_norm, w_kv, attn_w_q, attn_sinks, attn_w_o, final_norm):
    w_in = ssm_w_in[0]
    w_dt = jnp.pad(w_in[:, D_INNER + CONV_DIM:], ((0, 0), (0, LANES - SSM_HEADS)))
    w = dict(
        ffn1_norm=ffn1_norm, ffn2_norm=ffn2_norm, mix_norm=mix_norm, kv_norm=kv_norm,
        final_norm=final_norm, attn_sinks=attn_sinks[0],
        ffn1_w_gu=ffn1_w_gu.astype(BF16), ffn1_w_down=ffn1_w_down.astype(BF16),
        ffn2_w_gu=ffn2_w_gu.astype(BF16), ffn2_w_down=ffn2_w_down.astype(BF16),
        ssm_w_z=w_in[:, :D_INNER].astype(BF16),
        ssm_w_xbc=w_in[:, D_INNER:D_INNER + CONV_DIM].astype(BF16),
        ssm_w_dt=w_dt.astype(BF16),
        ssm_w_dtt=w_in[:, D_INNER + CONV_DIM:].T.astype(BF16),
        ssm_w_out=ssm_w_out[0].astype(BF16),
        w_k=w_kv[:, :KV_WIDTH].astype(BF16), w_v=w_kv[:, KV_WIDTH:].astype(BF16),
        attn_w_q=attn_w_q[0].astype(BF16), attn_w_o=attn_w_o[0].astype(BF16),
        ssm=dict(conv_w=ssm_conv_w[0], conv_b=ssm_conv_b[0], dt_bias=ssm_dt_bias[0],
                 a_log=ssm_a_log[0], d_skip=ssm_d[0], gate_norm=ssm_gate_norm[0]),
    )
    bp = x_prompt.shape[0]
    ssm0 = jnp.zeros((bp, SSM_HEADS, SSM_HEAD_DIM, SSM_D_STATE), F32)
    conv0 = jnp.zeros((bp, CONV_WIDTH - 1, CONV_DIM), F32)
    y_p, ssm_p, conv_p, kw_p, vw_p = _trunk_prompt(x_prompt, ssm0, conv0, w)
    y_s, ssm_s, conv_s, kw_s, vw_s = _trunk_sample(x_sample, state_ssm[0], state_conv[0],
                                                   cache_k_win, cache_v_win, w)
    return (y_p, y_s, ssm_p, conv_p, kw_p, vw_p, ssm_s, conv_s, kw_s, vw_s)
```

```python
import functools
import math

import jax
import jax.numpy as jnp
import numpy as np
from jax import lax
from jax.experimental import pallas as pl
from jax.experimental.pallas import tpu as pltpu

F32 = jnp.float32
BF16 = jnp.bfloat16

D_MODEL = 1024
D_FF = 2816
D_INNER = 2048
SSM_HEAD_DIM = 64
SSM_HEADS = 32
SSM_GROUPS = 4
HEADS_PER_GROUP = SSM_HEADS // SSM_GROUPS
SSM_D_STATE = 128
GROUP_WIDTH = D_INNER // SSM_GROUPS
CONV_WIDTH = 4
CONV_DIM = D_INNER + 2 * SSM_GROUPS * SSM_D_STATE
SSD_CHUNK = 128
WINDOW = 128
ATT_HEAD_DIM = 64
N_Q_HEADS = 16
N_KV_HEADS = 4
Q_PER_KV = N_Q_HEADS // N_KV_HEADS
KV_WIDTH = N_KV_HEADS * ATT_HEAD_DIM
PAST_LEN = 8192
NORM_EPS = 1e-5
LOG2E = 1.0 / math.log(2.0)

LANES = 128
SUBLANES = 8
VMEM_LIMIT_BYTES = 56 * 1024 * 1024
FF_CHUNK = 256
ROW_TILE = 1024
PROJ_ROW_TILE = 512
PROJ_COL_CHUNK = 512
MIX_ROWS = 256
DECODE_SEQS = 4

_NT = (((1,), (1,)), ((), ()))
_TN = (((0,), (0,)), ((), ()))


def _rms(x, g):
    ms = jnp.mean(x * x, axis=-1, keepdims=True)
    return x * lax.rsqrt(ms + NORM_EPS) * g


def _silu(x):
    return x * (1.0 / (1.0 + jnp.exp(-x)))


def _softplus(x):
    return jnp.maximum(x, 0.0) + jnp.log1p(jnp.exp(-jnp.abs(x)))


def _params(*sem):
    return pltpu.CompilerParams(dimension_semantics=sem, vmem_limit_bytes=VMEM_LIMIT_BYTES)


def _full(shape):
    return pl.BlockSpec(shape, lambda *_: (0,) * len(shape))


def _ffn_kernel(xa_ref, xb_ref, g_ref, wgu_ref, wd_ref, fg_ref, oa_ref, ob_ref, *, final_norm, na):
    def run(x_ref, o_ref):
        x = x_ref[...]
        h = _rms(x, g_ref[...]).astype(BF16)
        for j in range(D_FF // FF_CHUNK):
            lo = j * FF_CHUNK
            gate = jnp.dot(h, wgu_ref[:, lo:lo + FF_CHUNK], preferred_element_type=F32)
            up = jnp.dot(h, wgu_ref[:, D_FF + lo:D_FF + lo + FF_CHUNK], preferred_element_type=F32)
            act = (_silu(gate) * up).astype(BF16)
            part = jnp.dot(act, wd_ref[lo:lo + FF_CHUNK, :], preferred_element_type=F32)
            if j == 0:
                o_ref[...] = part
            else:
                o_ref[...] += part
        out = x + 0.5 * o_ref[...]
        if final_norm:
            out = _rms(out, fg_ref[...])
        o_ref[...] = out

    i = pl.program_id(0)

    @pl.when(i < na)
    def _():
        run(xa_ref, oa_ref)

    @pl.when(i >= na)
    def _():
        run(xb_ref, ob_ref)


def _ffn(xa, xb, g, wgu, wd, layer, fg=None):
    ta, tb = min(ROW_TILE, xa.shape[0]), min(PROJ_ROW_TILE, xb.shape[0])
    na, nb = xa.shape[0] // ta, xb.shape[0] // tb
    final_norm = fg is not None
    g = g[layer]
    if fg is None:
        fg = g

    def layer_weights(*dims):
        return pl.BlockSpec((None,) + dims, lambda i: (layer, 0, 0), pipeline_mode=pl.Buffered(1))

    spec_a = pl.BlockSpec((ta, D_MODEL), lambda i: (jnp.minimum(i, na - 1), 0))
    spec_b = pl.BlockSpec((tb, D_MODEL), lambda i: (jnp.maximum(i - na, 0), 0))
    return pl.pallas_call(
        functools.partial(_ffn_kernel, final_norm=final_norm, na=na),
        grid=(na + nb,),
        in_specs=[spec_a, spec_b,
                  _full((1, D_MODEL)), layer_weights(D_MODEL, 2 * D_FF), layer_weights(D_FF, D_MODEL),
                  _full((1, D_MODEL))],
        out_specs=[spec_a, spec_b],
        out_shape=[jax.ShapeDtypeStruct(xa.shape, F32), jax.ShapeDtypeStruct(xb.shape, F32)],
        compiler_params=_params("arbitrary"),
        name="ffn",
    )(xa, xb, g.reshape(1, D_MODEL), wgu, wd, fg.reshape(1, D_MODEL))


def _norm_proj_kernel(x_ref, g_ref, *refs):
    n = len(refs) // 2
    h = _rms(x_ref[...], g_ref[...]).astype(BF16)
    for w_ref, o_ref in zip(refs[:n], refs[n:]):
        width = w_ref.shape[1]
        step = min(PROJ_COL_CHUNK, width)
        for lo in range(0, width, step):
            o_ref[:, lo:lo + step] = jnp.dot(h, w_ref[:, lo:lo + step], preferred_element_type=F32)


def _norm_proj(x, g, weights):
    t = x.shape[0]
    tm = min(PROJ_ROW_TILE, t)
    return pl.pallas_call(
        _norm_proj_kernel,
        grid=(t // tm,),
        in_specs=[pl.BlockSpec((tm, D_MODEL), lambda i: (i, 0)), _full((1, D_MODEL))]
        + [_full(w.shape) for w in weights],
        out_specs=[pl.BlockSpec((tm, w.shape[1]), lambda i: (i, 0)) for w in weights],
        out_shape=[jax.ShapeDtypeStruct((t, w.shape[1]), F32) for w in weights],
        compiler_params=_params("parallel"),
        name="norm_proj",
    )(x, g.reshape(1, D_MODEL), *weights)


def _proj_res_kernel(y_ref, w_ref, x_ref, o_ref):
    o_ref[...] = x_ref[...] + jnp.dot(y_ref[...].astype(BF16), w_ref[...], preferred_element_type=F32)


def _proj_res(y, w, x):
    t, k = y.shape
    tm = min(PROJ_ROW_TILE, t)
    return pl.pallas_call(
        _proj_res_kernel,
        grid=(t // tm,),
        in_specs=[pl.BlockSpec((tm, k), lambda i: (i, 0)), _full(w.shape),
                  pl.BlockSpec((tm, D_MODEL), lambda i: (i, 0))],
        out_specs=pl.BlockSpec((tm, D_MODEL), lambda i: (i, 0)),
        out_shape=jax.ShapeDtypeStruct((t, D_MODEL), F32),
        compiler_params=_params("parallel"),
        name="proj_res",
    )(y, w, x)


def _split3(x):
    hi = x.astype(BF16)
    r1 = x - hi.astype(F32)
    mid = r1.astype(BF16)
    lo = (r1 - mid.astype(F32)).astype(BF16)
    return hi, mid, lo


def _conv_silu(ext_scr, n, cw_ref, cb_ref):
    ext = ext_scr[...]
    x0 = ext.reshape(n // SUBLANES + 1, SUBLANES, CONV_DIM)
    sub = lax.broadcasted_iota(jnp.int32, (1, SUBLANES, CONV_DIM), 1)

    def down(a, d):
        rot = pltpu.roll(a, d, 1)
        return jnp.where(sub < d, jnp.concatenate([rot[:1], rot[:-1]], axis=0), rot)

    w = [cw_ref[k:k + 1, :].reshape(1, 1, CONV_DIM) for k in range(CONV_WIDTH)]
    x1 = down(x0, 1)
    near = x0 * w[3] + x1 * w[2]
    far = down(x0 * w[1] + x1 * w[0], 2)
    xc = _silu(((near + far)[1:]).reshape(n, CONV_DIM) + cb_ref[...])
    return xc, ext[n:n + SUBLANES]


def _ssd_chunk(xc, zg, dtt, state_scr, dtb_ref, alog_ref, dskip_ref, gn_ref, spread_ref,
               pad_scr, lt, ls, emit):
    def pad_rows(v, k):
        if ls == lt:
            return v
        pad_scr[k][...] = jnp.zeros(pad_scr[k].shape, F32)
        pad_scr[k][0:lt, :] = v
        return pad_scr[k][...]

    xs = xc[:, :D_INNER]
    bm = xc[:, D_INNER:D_INNER + GROUP_WIDTH]
    cm = xc[:, D_INNER + GROUP_WIDTH:]

    dt_r = _softplus(dtt + dtb_ref[...])
    da_r = dt_r * -jnp.exp(alog_ref[...])
    upper = (lax.broadcasted_iota(jnp.int32, (ls, ls), 0)
             <= lax.broadcasted_iota(jnp.int32, (ls, ls), 1)).astype(F32).astype(BF16)
    parts = jnp.dot(jnp.concatenate(_split3(da_r), axis=0), upper, preferred_element_type=F32)
    acs_r = (parts[0:SSM_HEADS] + parts[SSM_HEADS:2 * SSM_HEADS]) + parts[2 * SSM_HEADS:]
    last_r = acs_r[:, lt - 1:lt]
    c2_r = acs_r * LOG2E
    r2_r = jnp.log(dt_r) * LOG2E - c2_r
    w_r = dt_r * jnp.exp(last_r - acs_r)
    e_r = jnp.exp(acs_r)
    chunk_decay = jnp.exp(jnp.broadcast_to(last_r, (SSM_HEADS, LANES)))
    col = jnp.concatenate([c2_r, w_r, e_r, jnp.zeros((LANES - 3 * SSM_HEADS, ls), F32)],
                          axis=0).T[0:lt]
    wide = jnp.dot(jnp.concatenate(_split3(col), axis=1), spread_ref[...],
                   preferred_element_type=F32)
    w_x = wide[:, :D_INNER]
    e_x = wide[:, D_INNER:]

    causal = (lax.broadcasted_iota(jnp.int32, (lt, ls), 1)
              <= lax.broadcasted_iota(jnp.int32, (lt, ls), 0))
    low_half = lax.broadcasted_iota(jnp.int32, (lt, LANES), 1) < SSM_HEAD_DIM
    xs_s = pad_rows(xs, 0)
    bm_s = pad_rows(bm, 1)
    xd_s = pad_rows(xs * w_x, 2)

    for g in range(SSM_GROUPS):
        cm_g = cm[:, g * SSM_D_STATE:(g + 1) * SSM_D_STATE].astype(BF16)
        bm_g = bm_s[:, g * SSM_D_STATE:(g + 1) * SSM_D_STATE].astype(BF16)
        cb_g = lax.dot_general(cm_g, bm_g, _NT, preferred_element_type=F32)
        rows = slice(g * GROUP_WIDTH, (g + 1) * GROUP_WIDTH)
        y_off = lax.dot_general(cm_g, state_scr[rows, :].astype(BF16), _NT,
                                preferred_element_type=F32)
        tiles = []
        for jj in range(HEADS_PER_GROUP // 2):
            tile = g * (HEADS_PER_GROUP // 2) + jj
            cols = slice(tile * LANES, (tile + 1) * LANES)
            xp = xs_s[:, cols].astype(BF16)
            pair = []
            for h in (2 * tile, 2 * tile + 1):
                expo = col[:, h:h + 1] + r2_r[h:h + 1, :]
                mat = (cb_g * jnp.exp2(jnp.where(causal, expo, -jnp.inf))).astype(BF16)
                pair.append(jnp.dot(mat, xp, preferred_element_type=F32))
            y_diag = jnp.where(low_half, pair[0], pair[1])
            tiles.append(y_diag + y_off[:, jj * LANES:(jj + 1) * LANES] * e_x[:, cols]
                         + dskip_ref[:, cols] * xs[:, cols])
        yg = jnp.concatenate(tiles, axis=1) * _silu(zg[:, rows])
        ms = jnp.mean(yg * yg, axis=-1, keepdims=True)
        emit(g, yg * lax.rsqrt(ms + NORM_EPS) * gn_ref[:, rows])

        upd = lax.dot_general(xd_s[:, rows].astype(BF16), bm_g, _TN, preferred_element_type=F32)
        for hh in range(HEADS_PER_GROUP):
            h = g * HEADS_PER_GROUP + hh
            hrows = slice(h * SSM_HEAD_DIM, (h + 1) * SSM_HEAD_DIM)
            state_scr[hrows, :] = (state_scr[hrows, :] * chunk_decay[h:h + 1, :]
                                   + upd[hh * SSM_HEAD_DIM:(hh + 1) * SSM_HEAD_DIM, :])


def _ssd_decode_kernel(xbc_ref, z_ref, dtt_ref, conv0_ref, ssm0_ref,
                       cw_ref, cb_ref, dtb_ref, alog_ref, dskip_ref, gn_ref, spread_ref,
                       y_ref, ssm_ref, conv_ref, ext_scr, *pad_scr, lt, ls, nseq):
    for i in range(nseq):
        ext_i = ext_scr.at[i]
        ext_i[0:SUBLANES, :] = conv0_ref[i]
        ext_i[SUBLANES:SUBLANES + lt, :] = xbc_ref[i]
        xc, tail = _conv_silu(ext_i, lt, cw_ref, cb_ref)
        conv_ref[i] = tail
        ssm_ref[i] = ssm0_ref[i]

        def emit(g, y, i=i):
            y_ref[i, :, g * GROUP_WIDTH:(g + 1) * GROUP_WIDTH] = y

        _ssd_chunk(xc, z_ref[i], dtt_ref[i], ssm_ref.at[i], dtb_ref, alog_ref, dskip_ref, gn_ref,
                   spread_ref, [p.at[i] for p in pad_scr], lt, ls, emit)


def _mixer0_kernel(x_ref, g_ref, wz_ref, wxbc_ref, wdtt_ref, wout_ref, conv0_ref, ssm0_ref,
                   cw_ref, cb_ref, dtb_ref, alog_ref, dskip_ref, gn_ref, spread_ref,
                   o_ref, ssm_ref, conv_ref,
                   ext_scr, state_scr, z_scr, y_scr, *, rb, nsteps):
    c = pl.program_id(1)

    @pl.when(c == 0)
    def _():
        ext_scr[0:SUBLANES, :] = conv0_ref[...]
        state_scr[...] = ssm0_ref[...]

    x = x_ref[...]
    h = _rms(x, g_ref[...]).astype(BF16)
    for lo in range(0, CONV_DIM, PROJ_COL_CHUNK):
        ext_scr[SUBLANES:SUBLANES + rb, lo:lo + PROJ_COL_CHUNK] = jnp.dot(
            h, wxbc_ref[:, lo:lo + PROJ_COL_CHUNK], preferred_element_type=F32)
    for lo in range(0, D_INNER, PROJ_COL_CHUNK):
        z_scr[:, lo:lo + PROJ_COL_CHUNK] = jnp.dot(
            h, wz_ref[:, lo:lo + PROJ_COL_CHUNK], preferred_element_type=F32)
    dtt = lax.dot_general(wdtt_ref[...], h, _NT, preferred_element_type=F32)

    xc, tail = _conv_silu(ext_scr, rb, cw_ref, cb_ref)
    conv_ref[...] = tail
    ext_scr[0:SUBLANES, :] = tail

    for k in range(rb // SSD_CHUNK):
        rows = slice(k * SSD_CHUNK, (k + 1) * SSD_CHUNK)

        def emit(g, y, rows=rows):
            y_scr[rows, g * GROUP_WIDTH:(g + 1) * GROUP_WIDTH] = y.astype(BF16)

        _ssd_chunk(xc[rows], z_scr[rows, :], dtt[:, rows], state_scr, dtb_ref, alog_ref,
                   dskip_ref, gn_ref, spread_ref, (), SSD_CHUNK, SSD_CHUNK, emit)

    o_ref[...] = x + jnp.dot(y_scr[...], wout_ref[...], preferred_element_type=F32)

    @pl.when(c == nsteps - 1)
    def _():
        ssm_ref[...] = state_scr[...]


def _spread_matrix():
    k = np.arange(LANES)[:, None]
    head = np.arange(D_INNER)[None, :] // SSM_HEAD_DIM
    one = np.concatenate([k == SSM_HEADS + head, k == 2 * SSM_HEADS + head], axis=1)
    return jnp.asarray(np.concatenate([one, one, one], axis=0), dtype=BF16)


def _ssd_decode(xbc, z, dt, conv0, ssm0, p, bsz, seq):
    lt, ls, nseq = seq, LANES, DECODE_SEQS
    xbc = xbc.reshape(bsz, seq, CONV_DIM)
    z = z.reshape(bsz, seq, D_INNER)
    dtt = jnp.swapaxes(dt.reshape(bsz, seq, LANES)[:, :, :SSM_HEADS], 1, 2)
    dtt = jnp.pad(dtt, ((0, 0), (0, 0), (0, ls - lt)))
    conv0 = jnp.pad(conv0, ((0, 0), (SUBLANES - (CONV_WIDTH - 1), 0), (0, 0)))
    ssm0 = ssm0.reshape(bsz, D_INNER, SSM_D_STATE)

    def head_rows(v):
        return jnp.broadcast_to(v.astype(F32)[:, None], (SSM_HEADS, ls))

    def seqs(*dims):
        return pl.BlockSpec((nseq,) + dims, lambda b: (b,) + (0,) * len(dims))

    y, ssm, conv = pl.pallas_call(
        functools.partial(_ssd_decode_kernel, lt=lt, ls=ls, nseq=nseq),
        grid=(bsz // nseq,),
        in_specs=[seqs(lt, CONV_DIM), seqs(lt, D_INNER), seqs(SSM_HEADS, ls),
                  seqs(SUBLANES, CONV_DIM), seqs(D_INNER, SSM_D_STATE),
                  _full((CONV_WIDTH, CONV_DIM)), _full((1, CONV_DIM)),
                  _full((SSM_HEADS, ls)), _full((SSM_HEADS, ls)),
                  _full((1, D_INNER)), _full((1, D_INNER)),
                  _full((3 * LANES, 2 * D_INNER))],
        out_specs=[seqs(lt, D_INNER), seqs(D_INNER, SSM_D_STATE), seqs(SUBLANES, CONV_DIM)],
        out_shape=[jax.ShapeDtypeStruct((bsz, seq, D_INNER), F32),
                   jax.ShapeDtypeStruct((bsz, D_INNER, SSM_D_STATE), F32),
                   jax.ShapeDtypeStruct((bsz, SUBLANES, CONV_DIM), F32)],
        scratch_shapes=[pltpu.VMEM((nseq, SUBLANES + lt, CONV_DIM), F32),
                        pltpu.VMEM((nseq, ls, D_INNER), F32), pltpu.VMEM((nseq, ls, GROUP_WIDTH), F32),
                        pltpu.VMEM((nseq, ls, D_INNER), F32)],
        compiler_params=_params("parallel"),
        name="ssd_decode",
    )(xbc, z, dtt, conv0, ssm0,
      p["conv_w"], p["conv_b"].reshape(1, CONV_DIM),
      head_rows(p["dt_bias"]), head_rows(p["a_log"]),
      jnp.repeat(p["d_skip"].astype(F32), SSM_HEAD_DIM).reshape(1, D_INNER),
      p["gate_norm"].reshape(1, D_INNER), _spread_matrix())
    return (y.reshape(bsz * seq, D_INNER),
            ssm.reshape(bsz, SSM_HEADS, SSM_HEAD_DIM, SSM_D_STATE),
            conv[:, SUBLANES - (CONV_WIDTH - 1):, :])


def _mixer0(x, conv0, ssm0, w, bsz, seq):
    rb = MIX_ROWS
    nsteps = seq // rb
    p = w["ssm"]
    conv0 = jnp.pad(conv0, ((0, 0), (SUBLANES - (CONV_WIDTH - 1), 0), (0, 0)))
    ssm0 = ssm0.reshape(bsz, D_INNER, SSM_D_STATE)

    def head_rows(v):
        return jnp.broadcast_to(v.astype(F32)[:, None], (SSM_HEADS, SSD_CHUNK))

    row_spec = pl.BlockSpec((rb, D_MODEL), lambda b, c: (b * nsteps + c, 0))
    out, ssm, conv = pl.pallas_call(
        functools.partial(_mixer0_kernel, rb=rb, nsteps=nsteps),
        grid=(bsz, nsteps),
        in_specs=[row_spec,
                  _full((1, D_MODEL)), _full((D_MODEL, D_INNER)), _full((D_MODEL, CONV_DIM)),
                  _full((SSM_HEADS, D_MODEL)), _full((D_INNER, D_MODEL)),
                  pl.BlockSpec((None, SUBLANES, CONV_DIM), lambda b, c: (b, 0, 0)),
                  pl.BlockSpec((None, D_INNER, SSM_D_STATE), lambda b, c: (b, 0, 0)),
                  _full((CONV_WIDTH, CONV_DIM)), _full((1, CONV_DIM)),
                  _full((SSM_HEADS, SSD_CHUNK)), _full((SSM_HEADS, SSD_CHUNK)),
                  _full((1, D_INNER)), _full((1, D_INNER)),
                  _full((3 * LANES, 2 * D_INNER))],
        out_specs=[row_spec,
                   pl.BlockSpec((None, D_INNER, SSM_D_STATE), lambda b, c: (b, 0, 0)),
                   pl.BlockSpec((None, SUBLANES, CONV_DIM), lambda b, c: (b, 0, 0))],
        out_shape=[jax.ShapeDtypeStruct((bsz * seq, D_MODEL), F32),
                   jax.ShapeDtypeStruct((bsz, D_INNER, SSM_D_STATE), F32),
                   jax.ShapeDtypeStruct((bsz, SUBLANES, CONV_DIM), F32)],
        scratch_shapes=[pltpu.VMEM((SUBLANES + rb, CONV_DIM), F32),
                        pltpu.VMEM((D_INNER, SSM_D_STATE), F32),
                        pltpu.VMEM((rb, D_INNER), F32),
                        pltpu.VMEM((rb, D_INNER), BF16)],
        compiler_params=_params("parallel", "arbitrary"),
        name="mixer0",
    )(x, w["mix_norm"][0].reshape(1, D_MODEL), w["ssm_w_z"], w["ssm_w_xbc"], w["ssm_w_dtt"],
      w["ssm_w_out"], conv0, ssm0,
      p["conv_w"], p["conv_b"].reshape(1, CONV_DIM),
      head_rows(p["dt_bias"]), head_rows(p["a_log"]),
      jnp.repeat(p["d_skip"].astype(F32), SSM_HEAD_DIM).reshape(1, D_INNER),
      p["gate_norm"].reshape(1, D_INNER), _spread_matrix())
    return (out, ssm.reshape(bsz, SSM_HEADS, SSM_HEAD_DIM, SSM_D_STATE),
            conv[:, SUBLANES - (CONV_WIDTH - 1):, :])


def _alibi_slope(head):
    return 2.0 ** (-8.0 * (head + 1) / N_Q_HEADS)


def _attn_block(q, keys, vals, sink_ref, first_key_pos, tq):
    nkeys = 2 * WINDOW
    rows = lax.broadcasted_iota(jnp.int32, (Q_PER_KV * tq, nkeys), 0)
    kj = lax.broadcasted_iota(jnp.int32, (Q_PER_KV * tq, nkeys), 1)
    t = rows & (tq - 1)
    sub = lax.shift_right_logical(rows, int(math.log2(tq)))
    dist = WINDOW + t - kj
    valid = (dist >= 0) & (dist < WINDOW) & (first_key_pos + kj >= 0)
    distf = dist.astype(F32)
    low_half = lax.broadcasted_iota(jnp.int32, (nkeys, LANES), 1) < ATT_HEAD_DIM

    def pick(choices):
        return jnp.where(sub == 0, choices[0], jnp.where(sub == 1, choices[1],
                                                         jnp.where(sub == 2, choices[2], choices[3])))

    out = []
    for j in range(N_KV_HEADS):
        cols = slice((j // 2) * LANES, (j // 2 + 1) * LANES)
        kt = keys[:, cols]
        vt = vals[:, cols]
        if j % 2 == 0:
            k_lo = jnp.where(low_half, kt, 0.0)
            v_lo = jnp.where(low_half, vt, 0.0)
        else:
            k_lo = pltpu.roll(jnp.where(low_half, 0.0, kt), ATT_HEAD_DIM, 1)
            v_lo = pltpu.roll(jnp.where(low_half, 0.0, vt), ATT_HEAD_DIM, 1)
        q0 = q[:, (2 * j) * LANES:(2 * j + 1) * LANES] * (ATT_HEAD_DIM ** -0.5)
        q1 = q[:, (2 * j + 1) * LANES:(2 * j + 2) * LANES] * (ATT_HEAD_DIM ** -0.5)
        qs = jnp.concatenate([q0, pltpu.roll(q0, ATT_HEAD_DIM, 1),
                              q1, pltpu.roll(q1, ATT_HEAD_DIM, 1)], axis=0)
        s = lax.dot_general(qs.astype(BF16), k_lo.astype(BF16), _NT, preferred_element_type=F32)
        slope = pick([_alibi_slope(Q_PER_KV * j + g) for g in range(Q_PER_KV)])
        sink = pick([sink_ref[Q_PER_KV * j + g] for g in range(Q_PER_KV)])[:, 0:1]
        s = jnp.where(valid, s - slope * distf, -jnp.inf)
        mx = jnp.maximum(jnp.max(s, axis=-1, keepdims=True), sink)
        e = jnp.exp(s - mx)
        den = jnp.sum(e, axis=-1, keepdims=True) + jnp.exp(sink - mx)
        prob = (e * (1.0 / den)).astype(BF16)
        o = jnp.dot(prob, v_lo.astype(BF16), preferred_element_type=F32)
        out.append(o[0:tq] + pltpu.roll(o[tq:2 * tq], ATT_HEAD_DIM, 1))
        out.append(o[2 * tq:3 * tq] + pltpu.roll(o[3 * tq:4 * tq], ATT_HEAD_DIM, 1))
    return out


def _attn_decode_kernel(q_ref, kp_ref, kc_ref, vp_ref, vc_ref, bias_ref, sink_ref,
                        o_ref, kwin_ref, vwin_ref, k_scr, v_scr, s_scr, *, tq, nseq):
    k_scr[...] = jnp.zeros(k_scr.shape, F32)
    v_scr[...] = jnp.zeros(v_scr.shape, F32)
    low_half = lax.broadcasted_iota(jnp.int32, (tq, LANES), 1) < ATT_HEAD_DIM
    zeros = jnp.zeros((tq, LANES), F32)

    def half(tile, upper):
        return jnp.where(low_half, 0.0, tile) if upper else jnp.where(low_half, tile, 0.0)

    for i in range(nseq):
        k_scr[i, 0:WINDOW, :] = kp_ref[i]
        v_scr[i, 0:WINDOW, :] = vp_ref[i]
        k_scr[i, WINDOW:WINDOW + tq, :] = kc_ref[i]
        v_scr[i, WINDOW:WINDOW + tq, :] = vc_ref[i]
        kwin_ref[i] = k_scr[i, tq:tq + WINDOW, :]
        vwin_ref[i] = v_scr[i, tq:tq + WINDOW, :]
        q = q_ref[i] * (ATT_HEAD_DIM ** -0.5)
        blocks = []
        for h in range(N_Q_HEADS):
            j = h // Q_PER_KV
            piece = half(q[:, (h // 2) * LANES:(h // 2 + 1) * LANES], h % 2 == 1)
            if h % 2 != j % 2:
                piece = pltpu.roll(piece, ATT_HEAD_DIM, 1)
            blocks.append(jnp.concatenate([piece, zeros] if j < 2 else [zeros, piece], axis=1))
        q_all = jnp.concatenate(blocks, axis=0).astype(BF16)
        s_scr[i] = lax.dot_general(q_all, k_scr[i].astype(BF16), _NT,
                                   preferred_element_type=F32)
        s = s_scr[i] - bias_ref[...]
        sink = jnp.concatenate([jnp.full((tq, 1), sink_ref[h], F32) for h in range(N_Q_HEADS)], axis=0)
        mx = jnp.maximum(jnp.max(s, axis=-1, keepdims=True), sink)
        e = jnp.exp(s - mx)
        den = jnp.sum(e, axis=-1, keepdims=True) + jnp.exp(sink - mx)
        prob = (e * (1.0 / den)).astype(BF16)
        o_all = jnp.dot(prob, v_scr[i].astype(BF16), preferred_element_type=F32)
        for m in range(N_Q_HEADS // 2):
            parts = []
            for h in (2 * m, 2 * m + 1):
                j = h // Q_PER_KV
                piece = half(o_all[h * tq:(h + 1) * tq, (j // 2) * LANES:(j // 2 + 1) * LANES],
                             j % 2 == 1)
                if h % 2 != j % 2:
                    piece = pltpu.roll(piece, ATT_HEAD_DIM, 1)
                parts.append(piece)
            o_ref[i, :, m * LANES:(m + 1) * LANES] = parts[0] + parts[1]


def _mixer1_kernel(sink_ref, x_ref, xkv_ref, gq_ref, gkv_ref, wq_ref, wk_ref, wv_ref, wo_ref,
                   o_ref, kwin_ref, vwin_ref, k_scr, v_scr, att_scr, *, rb):
    c = pl.program_id(1)

    @pl.when(c == 0)
    def _():
        k_scr[0:WINDOW, :] = jnp.zeros((WINDOW, KV_WIDTH), F32)
        v_scr[0:WINDOW, :] = jnp.zeros((WINDOW, KV_WIDTH), F32)

    hkv = _rms(xkv_ref[...], gkv_ref[...]).astype(BF16)
    k_scr[WINDOW:WINDOW + rb, :] = jnp.dot(hkv, wk_ref[...], preferred_element_type=F32)
    v_scr[WINDOW:WINDOW + rb, :] = jnp.dot(hkv, wv_ref[...], preferred_element_type=F32)
    x = x_ref[...]
    q = jnp.dot(_rms(x, gq_ref[...]).astype(BF16), wq_ref[...], preferred_element_type=F32)
    for blk in range(rb // WINDOW):
        lo = blk * WINDOW
        tiles = _attn_block(q[lo:lo + WINDOW], k_scr[lo:lo + 2 * WINDOW, :],
                            v_scr[lo:lo + 2 * WINDOW, :], sink_ref, c * rb + lo - WINDOW, WINDOW)
        for m, tile in enumerate(tiles):
            att_scr[lo:lo + WINDOW, m * LANES:(m + 1) * LANES] = tile.astype(BF16)
    o_ref[...] = x + jnp.dot(att_scr[...], wo_ref[...], preferred_element_type=F32)
    k_last = k_scr[rb:rb + WINDOW, :]
    v_last = v_scr[rb:rb + WINDOW, :]
    kwin_ref[...] = k_last
    vwin_ref[...] = v_last
    k_scr[0:WINDOW, :] = k_last
    v_scr[0:WINDOW, :] = v_last


def _mixer1(x, xkv, w, bsz, seq):
    rb = MIX_ROWS
    nsteps = seq // rb
    row_spec = pl.BlockSpec((rb, D_MODEL), lambda b, c: (b * nsteps + c, 0))
    win_spec = pl.BlockSpec((None, WINDOW, KV_WIDTH), lambda b, c: (b, 0, 0))
    return pl.pallas_call(
        functools.partial(_mixer1_kernel, rb=rb),
        grid=(bsz, nsteps),
        in_specs=[pl.BlockSpec(memory_space=pltpu.SMEM), row_spec, row_spec,
                  _full((1, D_MODEL)), _full((1, D_MODEL)),
                  _full((D_MODEL, D_MODEL)), _full((D_MODEL, KV_WIDTH)), _full((D_MODEL, KV_WIDTH)),
                  _full((D_MODEL, D_MODEL))],
        out_specs=[row_spec, win_spec, win_spec],
        out_shape=[jax.ShapeDtypeStruct((bsz * seq, D_MODEL), F32),
                   jax.ShapeDtypeStruct((bsz, WINDOW, KV_WIDTH), F32),
                   jax.ShapeDtypeStruct((bsz, WINDOW, KV_WIDTH), F32)],
        scratch_shapes=[pltpu.VMEM((WINDOW + rb, KV_WIDTH), F32), pltpu.VMEM((WINDOW + rb, KV_WIDTH), F32),
                        pltpu.VMEM((rb, D_MODEL), BF16)],
        compiler_params=_params("parallel", "arbitrary"),
        name="mixer1",
    )(w["attn_sinks"].astype(F32), x, xkv, w["mix_norm"][1].reshape(1, D_MODEL),
      w["kv_norm"].reshape(1, D_MODEL), w["attn_w_q"], w["w_k"], w["w_v"], w["attn_w_o"])


def _attention_decode(q, k_cache, k_new, v_cache, v_new, sinks):
    bsz, seq = q.shape[0], q.shape[1]
    nseq = DECODE_SEQS

    def seqs(*dims):
        return pl.BlockSpec((nseq,) + dims, lambda b: (b,) + (0,) * len(dims))

    t = np.tile(np.arange(seq), N_Q_HEADS)[:, None]
    kj = np.arange(2 * WINDOW)[None, :]
    dist = WINDOW + t - kj
    valid = (dist >= 0) & (dist < WINDOW) & (PAST_LEN - WINDOW + kj >= 0)
    slope = np.repeat([_alibi_slope(h) for h in range(N_Q_HEADS)], seq)[:, None]
    bias = jnp.asarray(np.where(valid, slope * dist, np.inf), dtype=F32)

    return pl.pallas_call(
        functools.partial(_attn_decode_kernel, tq=seq, nseq=nseq),
        grid=(bsz // nseq,),
        in_specs=[seqs(seq, D_MODEL),
                  seqs(WINDOW, KV_WIDTH), seqs(seq, KV_WIDTH), seqs(WINDOW, KV_WIDTH), seqs(seq, KV_WIDTH),
                  _full((N_Q_HEADS * seq, 2 * WINDOW)), pl.BlockSpec(memory_space=pltpu.SMEM)],
        out_specs=[seqs(seq, D_MODEL), seqs(WINDOW, KV_WIDTH), seqs(WINDOW, KV_WIDTH)],
        out_shape=[jax.ShapeDtypeStruct((bsz, seq, D_MODEL), F32),
                   jax.ShapeDtypeStruct((bsz, WINDOW, KV_WIDTH), F32),
                   jax.ShapeDtypeStruct((bsz, WINDOW, KV_WIDTH), F32)],
        scratch_shapes=[pltpu.VMEM((nseq, 2 * WINDOW, KV_WIDTH), F32),
                        pltpu.VMEM((nseq, 2 * WINDOW, KV_WIDTH), F32),
                        pltpu.VMEM((nseq, N_Q_HEADS * seq, 2 * WINDOW), F32)],
        compiler_params=_params("parallel"),
        name="swa_decode",
    )(q, k_cache, k_new, v_cache, v_new, bias, sinks.astype(F32))


def _trunk(xp, xs, ssm_p, conv_p, ssm_s, conv_s, k_buf, v_buf, w):
    bp, lp = xp.shape[0], xp.shape[1]
    bs, ls = xs.shape[0], xs.shape[1]
    xp = xp.reshape(bp * lp, D_MODEL)
    xs = xs.reshape(bs * ls, D_MODEL)

    xp, xs = _ffn(xp, xs, w["ffn1_norm"], w["ffn1_w_gu"], w["ffn1_w_down"], 0)
    xp, ssm_p, conv_p = _mixer0(xp, conv_p, ssm_p, w, bp, lp)
    z, xbc, dt = _norm_proj(xs, w["mix_norm"][0], [w["ssm_w_z"], w["ssm_w_xbc"], w["ssm_w_dt"]])
    y, ssm_s, conv_s = _ssd_decode(xbc, z, dt, conv_s, ssm_s, w["ssm"], bs, ls)
    xs = _proj_res(y, w["ssm_w_out"], xs)
    xp_kv, xs_kv = _ffn(xp, xs, w["ffn2_norm"], w["ffn2_w_gu"], w["ffn2_w_down"], 0)

    xp, xs = _ffn(xp_kv, xs_kv, w["ffn1_norm"], w["ffn1_w_gu"], w["ffn1_w_down"], 1)
    xp, kw_p, vw_p = _mixer1(xp, xp_kv, w, bp, lp)
    k_new, v_new = _norm_proj(xs_kv, w["kv_norm"], [w["w_k"], w["w_v"]])
    (q,) = _norm_proj(xs, w["mix_norm"][1], [w["attn_w_q"]])
    o, kw_s, vw_s = _attention_decode(
        q.reshape(bs, ls, D_MODEL), k_buf.reshape(bs, WINDOW, KV_WIDTH), k_new.reshape(bs, ls, KV_WIDTH),
        v_buf.reshape(bs, WINDOW, KV_WIDTH), v_new.reshape(bs, ls, KV_WIDTH), w["attn_sinks"])
    xs = _proj_res(o.reshape(bs * ls, D_MODEL), w["attn_w_o"], xs)
    yp, ys = _ffn(xp, xs, w["ffn2_norm"], w["ffn2_w_gu"], w["ffn2_w_down"], 1, fg=w["final_norm"])

    def heads(t, bsz):
        return t.reshape(bsz, WINDOW, N_KV_HEADS, ATT_HEAD_DIM)

    return (yp.reshape(bp, lp, D_MODEL), ys.reshape(bs, ls, D_MODEL), ssm_p[None], conv_p[None],
            heads(kw_p, bp), heads(vw_p, bp), ssm_s[None], conv_s[None], heads(kw_s, bs), heads(vw_s, bs))


def kernel(x_prompt, x_sample, state_ssm, state_conv, cache_k_win, cache_v_win,
           ffn1_norm, ffn1_w_gu, ffn1_w_down, mix_norm, ffn2_norm, ffn2_w_gu, ffn2_w_down,
           ssm_w_in, ssm_conv_w, ssm_conv_b, ssm_dt_bias, ssm_a_log, ssm_d, ssm_gate_norm, ssm_w_out,
           kv_norm, w_kv, attn_w_q, attn_sinks, attn_w_o, final_norm):
    w_in = ssm_w_in[0]
    w_dt = jnp.pad(w_in[:, D_INNER + CONV_DIM:], ((0, 0), (0, LANES - SSM_HEADS)))
    w = dict(
        ffn1_norm=ffn1_norm, ffn2_norm=ffn2_norm, mix_norm=mix_norm, kv_norm=kv_norm,
        final_norm=final_norm, attn_sinks=attn_sinks[0],
        ffn1_w_gu=ffn1_w_gu.astype(BF16), ffn1_w_down=ffn1_w_down.astype(BF16),
        ffn2_w_gu=ffn2_w_gu.astype(BF16), ffn2_w_down=ffn2_w_down.astype(BF16),
        ssm_w_z=w_in[:, :D_INNER].astype(BF16),
        ssm_w_xbc=w_in[:, D_INNER:D_INNER + CONV_DIM].astype(BF16),
        ssm_w_dt=w_dt.astype(BF16),
        ssm_w_dtt=w_in[:, D_INNER + CONV_DIM:].T.astype(BF16),
        ssm_w_out=ssm_w_out[0].astype(BF16),
        w_k=w_kv[:, :KV_WIDTH].astype(BF16), w_v=w_kv[:, KV_WIDTH:].astype(BF16),
        attn_w_q=attn_w_q[0].astype(BF16), attn_w_o=attn_w_o[0].astype(BF16),
        ssm=dict(conv_w=ssm_conv_w[0], conv_b=ssm_conv_b[0], dt_bias=ssm_dt_bias[0],
                 a_log=ssm_a_log[0], d_skip=ssm_d[0], gate_norm=ssm_gate_norm[0]),
    )
    bp = x_prompt.shape[0]
    ssm0 = jnp.zeros((bp, SSM_HEADS, SSM_HEAD_DIM, SSM_D_STATE), F32)
    conv0 = jnp.zeros((bp, CONV_WIDTH - 1, CONV_DIM), F32)
    return _trunk(x_prompt, x_sample, ssm0, conv0, state_ssm[0], state_conv[0],
                  cache_k_win, cache_v_win, w)
```

```python
import functools
import math

import jax
import jax.numpy as jnp
import numpy as np
from jax import lax
from jax.experimental import pallas as pl
from jax.experimental.pallas import tpu as pltpu

F32 = jnp.float32
BF16 = jnp.bfloat16

D_MODEL = 1024
D_FF = 2816
D_INNER = 2048
SSM_HEAD_DIM = 64
SSM_HEADS = 32
SSM_GROUPS = 4
HEADS_PER_GROUP = SSM_HEADS // SSM_GROUPS
SSM_D_STATE = 128
GROUP_WIDTH = D_INNER // SSM_GROUPS
CONV_WIDTH = 4
CONV_DIM = D_INNER + 2 * SSM_GROUPS * SSM_D_STATE
SSD_CHUNK = 128
WINDOW = 128
ATT_HEAD_DIM = 64
N_Q_HEADS = 16
N_KV_HEADS = 4
Q_PER_KV = N_Q_HEADS // N_KV_HEADS
KV_WIDTH = N_KV_HEADS * ATT_HEAD_DIM
PAST_LEN = 8192
NORM_EPS = 1e-5
LOG2E = 1.0 / math.log(2.0)

LANES = 128
SUBLANES = 8
VMEM_LIMIT_BYTES = 56 * 1024 * 1024
FF_CHUNK = 256
ROW_TILE = 1024
PROJ_ROW_TILE = 512
PROJ_COL_CHUNK = 512
MIX_ROWS = 256
DECODE_SEQS = 4

_NT = (((1,), (1,)), ((), ()))
_TN = (((0,), (0,)), ((), ()))


def _rms(x, g):
    ms = jnp.mean(x * x, axis=-1, keepdims=True)
    return x * lax.rsqrt(ms + NORM_EPS) * g


def _silu(x):
    return x * (1.0 / (1.0 + jnp.exp(-x)))


def _softplus(x):
    return jnp.maximum(x, 0.0) + jnp.log1p(jnp.exp(-jnp.abs(x)))


def _params(*sem):
    return pltpu.CompilerParams(dimension_semantics=sem, vmem_limit_bytes=VMEM_LIMIT_BYTES)


def _full(shape):
    return pl.BlockSpec(shape, lambda *_: (0,) * len(shape))


def _ffn_kernel(xa_ref, xb_ref, g_ref, wgu_ref, wd_ref, fg_ref, oa_ref, ob_ref, *, final_norm, na):
    def run(x_ref, o_ref):
        x = x_ref[...]
        h = _rms(x, g_ref[...]).astype(BF16)
        for j in range(D_FF // FF_CHUNK):
            lo = j * FF_CHUNK
            gate = jnp.dot(h, wgu_ref[:, lo:lo + FF_CHUNK], preferred_element_type=F32)
            up = jnp.dot(h, wgu_ref[:, D_FF + lo:D_FF + lo + FF_CHUNK], preferred_element_type=F32)
            act = (_silu(gate) * up).astype(BF16)
            part = jnp.dot(act, wd_ref[lo:lo + FF_CHUNK, :], preferred_element_type=F32)
            if j == 0:
                o_ref[...] = part
            else:
                o_ref[...] += part
        out = x + 0.5 * o_ref[...]
        if final_norm:
            out = _rms(out, fg_ref[...])
        o_ref[...] = out

    i = pl.program_id(0)

    @pl.when(i < na)
    def _():
        run(xa_ref, oa_ref)

    @pl.when(i >= na)
    def _():
        run(xb_ref, ob_ref)


def _ffn(xa, xb, g, wgu, wd, layer, fg=None):
    ta, tb = min(ROW_TILE, xa.shape[0]), min(PROJ_ROW_TILE, xb.shape[0])
    na, nb = xa.shape[0] // ta, xb.shape[0] // tb
    final_norm = fg is not None
    g = g[layer]
    if fg is None:
        fg = g

    def layer_weights(*dims):
        return pl.BlockSpec((None,) + dims, lambda i: (layer, 0, 0), pipeline_mode=pl.Buffered(1))

    spec_a = pl.BlockSpec((ta, D_MODEL), lambda i: (jnp.minimum(i, na - 1), 0))
    spec_b = pl.BlockSpec((tb, D_MODEL), lambda i: (jnp.maximum(i - na, 0), 0))
    return pl.pallas_call(
        functools.partial(_ffn_kernel, final_norm=final_norm, na=na),
        grid=(na + nb,),
        in_specs=[spec_a, spec_b,
                  _full((1, D_MODEL)), layer_weights(D_MODEL, 2 * D_FF), layer_weights(D_FF, D_MODEL),
                  _full((1, D_MODEL))],
        out_specs=[spec_a, spec_b],
        out_shape=[jax.ShapeDtypeStruct(xa.shape, F32), jax.ShapeDtypeStruct(xb.shape, F32)],
        compiler_params=_params("arbitrary"),
        name="ffn",
    )(xa, xb, g.reshape(1, D_MODEL), wgu, wd, fg.reshape(1, D_MODEL))


def _norm_proj_kernel(x_ref, g_ref, *refs):
    n = len(refs) // 2
    h = _rms(x_ref[...], g_ref[...]).astype(BF16)
    for w_ref, o_ref in zip(refs[:n], refs[n:]):
        width = w_ref.shape[1]
        step = min(PROJ_COL_CHUNK, width)
        for lo in range(0, width, step):
            o_ref[:, lo:lo + step] = jnp.dot(h, w_ref[:, lo:lo + step], preferred_element_type=F32)


def _norm_proj(x, g, weights):
    t = x.shape[0]
    tm = min(PROJ_ROW_TILE, t)
    return pl.pallas_call(
        _norm_proj_kernel,
        grid=(t // tm,),
        in_specs=[pl.BlockSpec((tm, D_MODEL), lambda i: (i, 0)), _full((1, D_MODEL))]
        + [_full(w.shape) for w in weights],
        out_specs=[pl.BlockSpec((tm, w.shape[1]), lambda i: (i, 0)) for w in weights],
        out_shape=[jax.ShapeDtypeStruct((t, w.shape[1]), F32) for w in weights],
        compiler_params=_params("parallel"),
        name="norm_proj",
    )(x, g.reshape(1, D_MODEL), *weights)


def _proj_res_kernel(y_ref, w_ref, x_ref, o_ref):
    o_ref[...] = x_ref[...] + jnp.dot(y_ref[...].astype(BF16), w_ref[...], preferred_element_type=F32)


def _proj_res(y, w, x):
    t, k = y.shape
    tm = min(PROJ_ROW_TILE, t)
    return pl.pallas_call(
        _proj_res_kernel,
        grid=(t // tm,),
        in_specs=[pl.BlockSpec((tm, k), lambda i: (i, 0)), _full(w.shape),
                  pl.BlockSpec((tm, D_MODEL), lambda i: (i, 0))],
        out_specs=pl.BlockSpec((tm, D_MODEL), lambda i: (i, 0)),
        out_shape=jax.ShapeDtypeStruct((t, D_MODEL), F32),
        compiler_params=_params("parallel"),
        name="proj_res",
    )(y, w, x)


def _split3(x):
    hi = x.astype(BF16)
    r1 = x - hi.astype(F32)
    mid = r1.astype(BF16)
    lo = (r1 - mid.astype(F32)).astype(BF16)
    return hi, mid, lo


def _conv_silu(ext_scr, n, cw_ref, cb_ref):
    ext = ext_scr[...]
    x0 = ext.reshape(n // SUBLANES + 1, SUBLANES, CONV_DIM)
    sub = lax.broadcasted_iota(jnp.int32, (1, SUBLANES, CONV_DIM), 1)

    def down(a, d):
        rot = pltpu.roll(a, d, 1)
        return jnp.where(sub < d, jnp.concatenate([rot[:1], rot[:-1]], axis=0), rot)

    w = [cw_ref[k:k + 1, :].reshape(1, 1, CONV_DIM) for k in range(CONV_WIDTH)]
    x1 = down(x0, 1)
    near = x0 * w[3] + x1 * w[2]
    far = down(x0 * w[1] + x1 * w[0], 2)
    xc = _silu(((near + far)[1:]).reshape(n, CONV_DIM) + cb_ref[...])
    return xc, ext[n:n + SUBLANES]


def _ssd_scalars(dtt, dtb_ref, alog_ref, lt, ls):
    dt_r = _softplus(dtt + dtb_ref[...])
    da_r = dt_r * -jnp.exp(alog_ref[...])
    upper = (lax.broadcasted_iota(jnp.int32, (ls, ls), 0)
             <= lax.broadcasted_iota(jnp.int32, (ls, ls), 1)).astype(F32).astype(BF16)
    parts = jnp.dot(jnp.concatenate(_split3(da_r), axis=0), upper, preferred_element_type=F32)
    acs_r = (parts[0:SSM_HEADS] + parts[SSM_HEADS:2 * SSM_HEADS]) + parts[2 * SSM_HEADS:]
    last_r = acs_r[:, lt - 1:lt]
    c2_r = acs_r * LOG2E
    r2_r = jnp.log(dt_r) * LOG2E - c2_r
    w_r = dt_r * jnp.exp(last_r - acs_r)
    e_r = jnp.exp(acs_r)
    chunk_decay = jnp.exp(jnp.broadcast_to(last_r, (SSM_HEADS, LANES)))
    col = jnp.concatenate([c2_r, w_r, e_r, jnp.zeros((LANES - 3 * SSM_HEADS, ls), F32)],
                          axis=0).T[0:lt]
    return col, r2_r, chunk_decay


def _ssd_spread(col, spread_ref):
    return jnp.dot(jnp.concatenate(_split3(col), axis=1), spread_ref[...],
                   preferred_element_type=F32)


def _ssd_core(xc, zg, col, r2_r, chunk_decay, wide, state_scr, dskip_ref, gn_ref, pad_scr, lt, ls, emit):
    def pad_rows(v, k):
        if ls == lt:
            return v
        pad_scr[k][...] = jnp.zeros(pad_scr[k].shape, F32)
        pad_scr[k][0:lt, :] = v
        return pad_scr[k][...]

    xs = xc[:, :D_INNER]
    bm = xc[:, D_INNER:D_INNER + GROUP_WIDTH]
    cm = xc[:, D_INNER + GROUP_WIDTH:]
    w_x = wide[:, :D_INNER]
    e_x = wide[:, D_INNER:]

    causal = (lax.broadcasted_iota(jnp.int32, (lt, ls), 1)
              <= lax.broadcasted_iota(jnp.int32, (lt, ls), 0))
    low_half = lax.broadcasted_iota(jnp.int32, (lt, LANES), 1) < SSM_HEAD_DIM
    xs_s = pad_rows(xs, 0)
    bm_s = pad_rows(bm, 1)
    xd_s = pad_rows(xs * w_x, 2)

    for g in range(SSM_GROUPS):
        cm_g = cm[:, g * SSM_D_STATE:(g + 1) * SSM_D_STATE].astype(BF16)
        bm_g = bm_s[:, g * SSM_D_STATE:(g + 1) * SSM_D_STATE].astype(BF16)
        cb_g = lax.dot_general(cm_g, bm_g, _NT, preferred_element_type=F32)
        rows = slice(g * GROUP_WIDTH, (g + 1) * GROUP_WIDTH)
        y_off = lax.dot_general(cm_g, state_scr[rows, :].astype(BF16), _NT,
                                preferred_element_type=F32)
        tiles = []
        for jj in range(HEADS_PER_GROUP // 2):
            tile = g * (HEADS_PER_GROUP // 2) + jj
            cols = slice(tile * LANES, (tile + 1) * LANES)
            xp = xs_s[:, cols].astype(BF16)
            pair = []
            for h in (2 * tile, 2 * tile + 1):
                expo = col[:, h:h + 1] + r2_r[h:h + 1, :]
                mat = (cb_g * jnp.exp2(jnp.where(causal, expo, -jnp.inf))).astype(BF16)
                pair.append(jnp.dot(mat, xp, preferred_element_type=F32))
            y_diag = jnp.where(low_half, pair[0], pair[1])
            tiles.append(y_diag + y_off[:, jj * LANES:(jj + 1) * LANES] * e_x[:, cols]
                         + dskip_ref[:, cols] * xs[:, cols])
        yg = jnp.concatenate(tiles, axis=1) * _silu(zg[:, rows])
        ms = jnp.mean(yg * yg, axis=-1, keepdims=True)
        emit(g, yg * lax.rsqrt(ms + NORM_EPS) * gn_ref[:, rows])

        upd = lax.dot_general(xd_s[:, rows].astype(BF16), bm_g, _TN, preferred_element_type=F32)
        for hh in range(HEADS_PER_GROUP):
            h = g * HEADS_PER_GROUP + hh
            hrows = slice(h * SSM_HEAD_DIM, (h + 1) * SSM_HEAD_DIM)
            state_scr[hrows, :] = (state_scr[hrows, :] * chunk_decay[h:h + 1, :]
                                   + upd[hh * SSM_HEAD_DIM:(hh + 1) * SSM_HEAD_DIM, :])


def _ssd_chunk(xc, zg, dtt, state_scr, dtb_ref, alog_ref, dskip_ref, gn_ref, spread_ref, lt, emit):
    col, r2_r, chunk_decay = _ssd_scalars(dtt, dtb_ref, alog_ref, lt, lt)
    _ssd_core(xc, zg, col, r2_r, chunk_decay, _ssd_spread(col, spread_ref), state_scr,
              dskip_ref, gn_ref, (), lt, lt, emit)


def _ssd_decode_kernel(xbc_ref, z_ref, dtt_ref, conv0_ref, ssm0_ref,
                       cw_ref, cb_ref, dtb_ref, alog_ref, dskip_ref, gn_ref, spread_ref,
                       y_ref, ssm_ref, conv_ref, ext_scr, *pad_scr, lt, ls, nseq):
    staged = []
    for i in range(nseq):
        ext_i = ext_scr.at[i]
        ext_i[0:SUBLANES, :] = conv0_ref[i]
        ext_i[SUBLANES:SUBLANES + lt, :] = xbc_ref[i]
        xc, tail = _conv_silu(ext_i, lt, cw_ref, cb_ref)
        conv_ref[i] = tail
        ssm_ref[i] = ssm0_ref[i]
        staged.append((xc,) + _ssd_scalars(dtt_ref[i], dtb_ref, alog_ref, lt, ls))
    wide = _ssd_spread(jnp.concatenate([st[1] for st in staged], axis=0), spread_ref)
    for i, (xc, col, r2_r, chunk_decay) in enumerate(staged):

        def emit(g, y, i=i):
            y_ref[i, :, g * GROUP_WIDTH:(g + 1) * GROUP_WIDTH] = y

        _ssd_core(xc, z_ref[i], col, r2_r, chunk_decay, wide[i * lt:(i + 1) * lt], ssm_ref.at[i],
                  dskip_ref, gn_ref, [p.at[i] for p in pad_scr], lt, ls, emit)


def _mixer0_kernel(x_ref, g_ref, wz_ref, wxbc_ref, wdtt_ref, wout_ref, conv0_ref, ssm0_ref,
                   cw_ref, cb_ref, dtb_ref, alog_ref, dskip_ref, gn_ref, spread_ref,
                   o_ref, ssm_ref, conv_ref,
                   ext_scr, state_scr, z_scr, y_scr, *, rb, nsteps):
    c = pl.program_id(1)

    @pl.when(c == 0)
    def _():
        ext_scr[0:SUBLANES, :] = conv0_ref[...]
        state_scr[...] = ssm0_ref[...]

    x = x_ref[...]
    h = _rms(x, g_ref[...]).astype(BF16)
    for lo in range(0, CONV_DIM, PROJ_COL_CHUNK):
        ext_scr[SUBLANES:SUBLANES + rb, lo:lo + PROJ_COL_CHUNK] = jnp.dot(
            h, wxbc_ref[:, lo:lo + PROJ_COL_CHUNK], preferred_element_type=F32)
    for lo in range(0, D_INNER, PROJ_COL_CHUNK):
        z_scr[:, lo:lo + PROJ_COL_CHUNK] = jnp.dot(
            h, wz_ref[:, lo:lo + PROJ_COL_CHUNK], preferred_element_type=F32)
    dtt = lax.dot_general(wdtt_ref[...], h, _NT, preferred_element_type=F32)

    xc, tail = _conv_silu(ext_scr, rb, cw_ref, cb_ref)
    conv_ref[...] = tail
    ext_scr[0:SUBLANES, :] = tail

    for k in range(rb // SSD_CHUNK):
        rows = slice(k * SSD_CHUNK, (k + 1) * SSD_CHUNK)

        def emit(g, y, rows=rows):
            y_scr[rows, g * GROUP_WIDTH:(g + 1) * GROUP_WIDTH] = y.astype(BF16)

        _ssd_chunk(xc[rows], z_scr[rows, :], dtt[:, rows], state_scr, dtb_ref, alog_ref,
                   dskip_ref, gn_ref, spread_ref, SSD_CHUNK, emit)

    o_ref[...] = x + jnp.dot(y_scr[...], wout_ref[...], preferred_element_type=F32)

    @pl.when(c == nsteps - 1)
    def _():
        ssm_ref[...] = state_scr[...]


def _spread_matrix():
    k = np.arange(LANES)[:, None]
    head = np.arange(D_INNER)[None, :] // SSM_HEAD_DIM
    one = np.concatenate([k == SSM_HEADS + head, k == 2 * SSM_HEADS + head], axis=1)
    return jnp.asarray(np.concatenate([one, one, one], axis=0), dtype=BF16)


def _ssd_decode(xbc, z, dt, conv0, ssm0, p, bsz, seq):
    lt, ls, nseq = seq, LANES, DECODE_SEQS
    xbc = xbc.reshape(bsz, seq, CONV_DIM)
    z = z.reshape(bsz, seq, D_INNER)
    dtt = jnp.swapaxes(dt.reshape(bsz, seq, LANES)[:, :, :SSM_HEADS], 1, 2)
    dtt = jnp.pad(dtt, ((0, 0), (0, 0), (0, ls - lt)))
    conv0 = jnp.pad(conv0, ((0, 0), (SUBLANES - (CONV_WIDTH - 1), 0), (0, 0)))
    ssm0 = ssm0.reshape(bsz, D_INNER, SSM_D_STATE)

    def head_rows(v):
        return jnp.broadcast_to(v.astype(F32)[:, None], (SSM_HEADS, ls))

    def seqs(*dims):
        return pl.BlockSpec((nseq,) + dims, lambda b: (b,) + (0,) * len(dims))

    y, ssm, conv = pl.pallas_call(
        functools.partial(_ssd_decode_kernel, lt=lt, ls=ls, nseq=nseq),
        grid=(bsz // nseq,),
        in_specs=[seqs(lt, CONV_DIM), seqs(lt, D_INNER), seqs(SSM_HEADS, ls),
                  seqs(SUBLANES, CONV_DIM), seqs(D_INNER, SSM_D_STATE),
                  _full((CONV_WIDTH, CONV_DIM)), _full((1, CONV_DIM)),
                  _full((SSM_HEADS, ls)), _full((SSM_HEADS, ls)),
                  _full((1, D_INNER)), _full((1, D_INNER)),
                  _full((3 * LANES, 2 * D_INNER))],
        out_specs=[seqs(lt, D_INNER), seqs(D_INNER, SSM_D_STATE), seqs(SUBLANES, CONV_DIM)],
        out_shape=[jax.ShapeDtypeStruct((bsz, seq, D_INNER), F32),
                   jax.ShapeDtypeStruct((bsz, D_INNER, SSM_D_STATE), F32),
                   jax.ShapeDtypeStruct((bsz, SUBLANES, CONV_DIM), F32)],
        scratch_shapes=[pltpu.VMEM((nseq, SUBLANES + lt, CONV_DIM), F32),
                        pltpu.VMEM((nseq, ls, D_INNER), F32), pltpu.VMEM((nseq, ls, GROUP_WIDTH), F32),
                        pltpu.VMEM((nseq, ls, D_INNER), F32)],
        compiler_params=_params("parallel"),
        name="ssd_decode",
    )(xbc, z, dtt, conv0, ssm0,
      p["conv_w"], p["conv_b"].reshape(1, CONV_DIM),
      head_rows(p["dt_bias"]), head_rows(p["a_log"]),
      jnp.repeat(p["d_skip"].astype(F32), SSM_HEAD_DIM).reshape(1, D_INNER),
      p["gate_norm"].reshape(1, D_INNER), _spread_matrix())
    return (y.reshape(bsz * seq, D_INNER),
            ssm.reshape(bsz, SSM_HEADS, SSM_HEAD_DIM, SSM_D_STATE),
            conv[:, SUBLANES - (CONV_WIDTH - 1):, :])


def _mixer0(x, conv0, ssm0, w, bsz, seq):
    rb = MIX_ROWS
    nsteps = seq // rb
    p = w["ssm"]
    conv0 = jnp.pad(conv0, ((0, 0), (SUBLANES - (CONV_WIDTH - 1), 0), (0, 0)))
    ssm0 = ssm0.reshape(bsz, D_INNER, SSM_D_STATE)

    def head_rows(v):
        return jnp.broadcast_to(v.astype(F32)[:, None], (SSM_HEADS, SSD_CHUNK))

    row_spec = pl.BlockSpec((rb, D_MODEL), lambda b, c: (b * nsteps + c, 0))
    out, ssm, conv = pl.pallas_call(
        functools.partial(_mixer0_kernel, rb=rb, nsteps=nsteps),
        grid=(bsz, nsteps),
        in_specs=[row_spec,
                  _full((1, D_MODEL)), _full((D_MODEL, D_INNER)), _full((D_MODEL, CONV_DIM)),
                  _full((SSM_HEADS, D_MODEL)), _full((D_INNER, D_MODEL)),
                  pl.BlockSpec((None, SUBLANES, CONV_DIM), lambda b, c: (b, 0, 0)),
                  pl.BlockSpec((None, D_INNER, SSM_D_STATE), lambda b, c: (b, 0, 0)),
                  _full((CONV_WIDTH, CONV_DIM)), _full((1, CONV_DIM)),
                  _full((SSM_HEADS, SSD_CHUNK)), _full((SSM_HEADS, SSD_CHUNK)),
                  _full((1, D_INNER)), _full((1, D_INNER)),
                  _full((3 * LANES, 2 * D_INNER))],
        out_specs=[row_spec,
                   pl.BlockSpec((None, D_INNER, SSM_D_STATE), lambda b, c: (b, 0, 0)),
                   pl.BlockSpec((None, SUBLANES, CONV_DIM), lambda b, c: (b, 0, 0))],
        out_shape=[jax.ShapeDtypeStruct((bsz * seq, D_MODEL), F32),
                   jax.ShapeDtypeStruct((bsz, D_INNER, SSM_D_STATE), F32),
                   jax.ShapeDtypeStruct((bsz, SUBLANES, CONV_DIM), F32)],
        scratch_shapes=[pltpu.VMEM((SUBLANES + rb, CONV_DIM), F32),
                        pltpu.VMEM((D_INNER, SSM_D_STATE), F32),
                        pltpu.VMEM((rb, D_INNER), F32),
                        pltpu.VMEM((rb, D_INNER), BF16)],
        compiler_params=_params("parallel", "arbitrary"),
        name="mixer0",
    )(x, w["mix_norm"][0].reshape(1, D_MODEL), w["ssm_w_z"], w["ssm_w_xbc"], w["ssm_w_dtt"],
      w["ssm_w_out"], conv0, ssm0,
      p["conv_w"], p["conv_b"].reshape(1, CONV_DIM),
      head_rows(p["dt_bias"]), head_rows(p["a_log"]),
      jnp.repeat(p["d_skip"].astype(F32), SSM_HEAD_DIM).reshape(1, D_INNER),
      p["gate_norm"].reshape(1, D_INNER), _spread_matrix())
    return (out, ssm.reshape(bsz, SSM_HEADS, SSM_HEAD_DIM, SSM_D_STATE),
            conv[:, SUBLANES - (CONV_WIDTH - 1):, :])


def _alibi_slope(head):
    return 2.0 ** (-8.0 * (head + 1) / N_Q_HEADS)


def _attn_block(q, keys, vals, sink_ref, first_key_pos, tq):
    nkeys = 2 * WINDOW
    rows = lax.broadcasted_iota(jnp.int32, (Q_PER_KV * tq, nkeys), 0)
    kj = lax.broadcasted_iota(jnp.int32, (Q_PER_KV * tq, nkeys), 1)
    t = rows & (tq - 1)
    sub = lax.shift_right_logical(rows, int(math.log2(tq)))
    dist = WINDOW + t - kj
    valid = (dist >= 0) & (dist < WINDOW) & (first_key_pos + kj >= 0)
    distf = dist.astype(F32)
    low_half = lax.broadcasted_iota(jnp.int32, (nkeys, LANES), 1) < ATT_HEAD_DIM

    def pick(choices):
        return jnp.where(sub == 0, choices[0], jnp.where(sub == 1, choices[1],
                                                         jnp.where(sub == 2, choices[2], choices[3])))

    out = []
    for j in range(N_KV_HEADS):
        cols = slice((j // 2) * LANES, (j // 2 + 1) * LANES)
        kt = keys[:, cols]
        vt = vals[:, cols]
        if j % 2 == 0:
            k_lo = jnp.where(low_half, kt, 0.0)
            v_lo = jnp.where(low_half, vt, 0.0)
        else:
            k_lo = pltpu.roll(jnp.where(low_half, 0.0, kt), ATT_HEAD_DIM, 1)
            v_lo = pltpu.roll(jnp.where(low_half, 0.0, vt), ATT_HEAD_DIM, 1)
        q0 = q[:, (2 * j) * LANES:(2 * j + 1) * LANES] * (ATT_HEAD_DIM ** -0.5)
        q1 = q[:, (2 * j + 1) * LANES:(2 * j + 2) * LANES] * (ATT_HEAD_DIM ** -0.5)
        qs = jnp.concatenate([q0, pltpu.roll(q0, ATT_HEAD_DIM, 1),
                              q1, pltpu.roll(q1, ATT_HEAD_DIM, 1)], axis=0)
        s = lax.dot_general(qs.astype(BF16), k_lo.astype(BF16), _NT, preferred_element_type=F32)
        slope = pick([_alibi_slope(Q_PER_KV * j + g) for g in range(Q_PER_KV)])
        sink = pick([sink_ref[Q_PER_KV * j + g] for g in range(Q_PER_KV)])[:, 0:1]
        s = jnp.where(valid, s - slope * distf, -jnp.inf)
        mx = jnp.maximum(jnp.max(s, axis=-1, keepdims=True), sink)
        e = jnp.exp(s - mx)
        den = jnp.sum(e, axis=-1, keepdims=True) + jnp.exp(sink - mx)
        prob = (e * (1.0 / den)).astype(BF16)
        o = jnp.dot(prob, v_lo.astype(BF16), preferred_element_type=F32)
        out.append(o[0:tq] + pltpu.roll(o[tq:2 * tq], ATT_HEAD_DIM, 1))
        out.append(o[2 * tq:3 * tq] + pltpu.roll(o[3 * tq:4 * tq], ATT_HEAD_DIM, 1))
    return out


def _attn_decode_kernel(q_ref, kp_ref, kc_ref, vp_ref, vc_ref, bias_ref, sink_ref,
                        o_ref, kwin_ref, vwin_ref, k_scr, v_scr, s_scr, *, tq, nseq):
    k_scr[...] = jnp.zeros(k_scr.shape, F32)
    v_scr[...] = jnp.zeros(v_scr.shape, F32)
    low_half = lax.broadcasted_iota(jnp.int32, (tq, LANES), 1) < ATT_HEAD_DIM
    zeros = jnp.zeros((tq, LANES), F32)

    def half(tile, upper):
        return jnp.where(low_half, 0.0, tile) if upper else jnp.where(low_half, tile, 0.0)

    for i in range(nseq):
        k_scr[i, 0:WINDOW, :] = kp_ref[i]
        v_scr[i, 0:WINDOW, :] = vp_ref[i]
        k_scr[i, WINDOW:WINDOW + tq, :] = kc_ref[i]
        v_scr[i, WINDOW:WINDOW + tq, :] = vc_ref[i]
        kwin_ref[i] = k_scr[i, tq:tq + WINDOW, :]
        vwin_ref[i] = v_scr[i, tq:tq + WINDOW, :]
        q = q_ref[i] * (ATT_HEAD_DIM ** -0.5)
        blocks = []
        for h in range(N_Q_HEADS):
            j = h // Q_PER_KV
            piece = half(q[:, (h // 2) * LANES:(h // 2 + 1) * LANES], h % 2 == 1)
            if h % 2 != j % 2:
                piece = pltpu.roll(piece, ATT_HEAD_DIM, 1)
            blocks.append(jnp.concatenate([piece, zeros] if j < 2 else [zeros, piece], axis=1))
        q_all = jnp.concatenate(blocks, axis=0).astype(BF16)
        s_scr[i] = lax.dot_general(q_all, k_scr[i].astype(BF16), _NT,
                                   preferred_element_type=F32)
        s = s_scr[i] - bias_ref[...]
        sink = jnp.concatenate([jnp.full((tq, 1), sink_ref[h], F32) for h in range(N_Q_HEADS)], axis=0)
        mx = jnp.maximum(jnp.max(s, axis=-1, keepdims=True), sink)
        e = jnp.exp(s - mx)
        den = jnp.sum(e, axis=-1, keepdims=True) + jnp.exp(sink - mx)
        prob = (e * (1.0 / den)).astype(BF16)
        o_all = jnp.dot(prob, v_scr[i].astype(BF16), preferred_element_type=F32)
        for m in range(N_Q_HEADS // 2):
            parts = []
            for h in (2 * m, 2 * m + 1):
                j = h // Q_PER_KV
                piece = half(o_all[h * tq:(h + 1) * tq, (j // 2) * LANES:(j // 2 + 1) * LANES],
                             j % 2 == 1)
                if h % 2 != j % 2:
                    piece = pltpu.roll(piece, ATT_HEAD_DIM, 1)
                parts.append(piece)
            o_ref[i, :, m * LANES:(m + 1) * LANES] = parts[0] + parts[1]


def _mixer1_kernel(sink_ref, x_ref, xkv_ref, gq_ref, gkv_ref, wq_ref, wk_ref, wv_ref, wo_ref,
                   o_ref, kwin_ref, vwin_ref, k_scr, v_scr, att_scr, *, rb):
    c = pl.program_id(1)

    @pl.when(c == 0)
    def _():
        k_scr[0:WINDOW, :] = jnp.zeros((WINDOW, KV_WIDTH), F32)
        v_scr[0:WINDOW, :] = jnp.zeros((WINDOW, KV_WIDTH), F32)

    hkv = _rms(xkv_ref[...], gkv_ref[...]).astype(BF16)
    k_scr[WINDOW:WINDOW + rb, :] = jnp.dot(hkv, wk_ref[...], preferred_element_type=F32)
    v_scr[WINDOW:WINDOW + rb, :] = jnp.dot(hkv, wv_ref[...], preferred_element_type=F32)
    x = x_ref[...]
    q = jnp.dot(_rms(x, gq_ref[...]).astype(BF16), wq_ref[...], preferred_element_type=F32)
    for blk in range(rb // WINDOW):
        lo = blk * WINDOW
        tiles = _attn_block(q[lo:lo + WINDOW], k_scr[lo:lo + 2 * WINDOW, :],
                            v_scr[lo:lo + 2 * WINDOW, :], sink_ref, c * rb + lo - WINDOW, WINDOW)
        for m, tile in enumerate(tiles):
            att_scr[lo:lo + WINDOW, m * LANES:(m + 1) * LANES] = tile.astype(BF16)
    o_ref[...] = x + jnp.dot(att_scr[...], wo_ref[...], preferred_element_type=F32)
    k_last = k_scr[rb:rb + WINDOW, :]
    v_last = v_scr[rb:rb + WINDOW, :]
    kwin_ref[...] = k_last
    vwin_ref[...] = v_last
    k_scr[0:WINDOW, :] = k_last
    v_scr[0:WINDOW, :] = v_last


def _mixer1(x, xkv, w, bsz, seq):
    rb = MIX_ROWS
    nsteps = seq // rb
    row_spec = pl.BlockSpec((rb, D_MODEL), lambda b, c: (b * nsteps + c, 0))
    win_spec = pl.BlockSpec((None, WINDOW, KV_WIDTH), lambda b, c: (b, 0, 0))
    return pl.pallas_call(
        functools.partial(_mixer1_kernel, rb=rb),
        grid=(bsz, nsteps),
        in_specs=[pl.BlockSpec(memory_space=pltpu.SMEM), row_spec, row_spec,
                  _full((1, D_MODEL)), _full((1, D_MODEL)),
                  _full((D_MODEL, D_MODEL)), _full((D_MODEL, KV_WIDTH)), _full((D_MODEL, KV_WIDTH)),
                  _full((D_MODEL, D_MODEL))],
        out_specs=[row_spec, win_spec, win_spec],
        out_shape=[jax.ShapeDtypeStruct((bsz * seq, D_MODEL), F32),
                   jax.ShapeDtypeStruct((bsz, WINDOW, KV_WIDTH), F32),
                   jax.ShapeDtypeStruct((bsz, WINDOW, KV_WIDTH), F32)],
        scratch_shapes=[pltpu.VMEM((WINDOW + rb, KV_WIDTH), F32), pltpu.VMEM((WINDOW + rb, KV_WIDTH), F32),
                        pltpu.VMEM((rb, D_MODEL), BF16)],
        compiler_params=_params("parallel", "arbitrary"),
        name="mixer1",
    )(w["attn_sinks"].astype(F32), x, xkv, w["mix_norm"][1].reshape(1, D_MODEL),
      w["kv_norm"].reshape(1, D_MODEL), w["attn_w_q"], w["w_k"], w["w_v"], w["attn_w_o"])


def _attention_decode(q, k_cache, k_new, v_cache, v_new, sinks):
    bsz, seq = q.shape[0], q.shape[1]
    nseq = DECODE_SEQS

    def seqs(*dims):
        return pl.BlockSpec((nseq,) + dims, lambda b: (b,) + (0,) * len(dims))

    t = np.tile(np.arange(seq), N_Q_HEADS)[:, None]
    kj = np.arange(2 * WINDOW)[None, :]
    dist = WINDOW + t - kj
    valid = (dist >= 0) & (dist < WINDOW) & (PAST_LEN - WINDOW + kj >= 0)
    slope = np.repeat([_alibi_slope(h) for h in range(N_Q_HEADS)], seq)[:, None]
    bias = jnp.asarray(np.where(valid, slope * dist, np.inf), dtype=F32)

    return pl.pallas_call(
        functools.partial(_attn_decode_kernel, tq=seq, nseq=nseq),
        grid=(bsz // nseq,),
        in_specs=[seqs(seq, D_MODEL),
                  seqs(WINDOW, KV_WIDTH), seqs(seq, KV_WIDTH), seqs(WINDOW, KV_WIDTH), seqs(seq, KV_WIDTH),
                  _full((N_Q_HEADS * seq, 2 * WINDOW)), pl.BlockSpec(memory_space=pltpu.SMEM)],
        out_specs=[seqs(seq, D_MODEL), seqs(WINDOW, KV_WIDTH), seqs(WINDOW, KV_WIDTH)],
        out_shape=[jax.ShapeDtypeStruct((bsz, seq, D_MODEL), F32),
                   jax.ShapeDtypeStruct((bsz, WINDOW, KV_WIDTH), F32),
                   jax.ShapeDtypeStruct((bsz, WINDOW, KV_WIDTH), F32)],
        scratch_shapes=[pltpu.VMEM((nseq, 2 * WINDOW, KV_WIDTH), F32),
                        pltpu.VMEM((nseq, 2 * WINDOW, KV_WIDTH), F32),
                        pltpu.VMEM((nseq, N_Q_HEADS * seq, 2 * WINDOW), F32)],
        compiler_params=_params("parallel"),
        name="swa_decode",
    )(q, k_cache, k_new, v_cache, v_new, bias, sinks.astype(F32))


def _trunk(xp, xs, ssm_p, conv_p, ssm_s, conv_s, k_buf, v_buf, w):
    bp, lp = xp.shape[0], xp.shape[1]
    bs, ls = xs.shape[0], xs.shape[1]
    xp = xp.reshape(bp * lp, D_MODEL)
    xs = xs.reshape(bs * ls, D_MODEL)

    xp, xs = _ffn(xp, xs, w["ffn1_norm"], w["ffn1_w_gu"], w["ffn1_w_down"], 0)
    xp, ssm_p, conv_p = _mixer0(xp, conv_p, ssm_p, w, bp, lp)
    z, xbc, dt = _norm_proj(xs, w["mix_norm"][0], [w["ssm_w_z"], w["ssm_w_xbc"], w["ssm_w_dt"]])
    y, ssm_s, conv_s = _ssd_decode(xbc, z, dt, conv_s, ssm_s, w["ssm"], bs, ls)
    xs = _proj_res(y, w["ssm_w_out"], xs)
    xp_kv, xs_kv = _ffn(xp, xs, w["ffn2_norm"], w["ffn2_w_gu"], w["ffn2_w_down"], 0)

    xp, xs = _ffn(xp_kv, xs_kv, w["ffn1_norm"], w["ffn1_w_gu"], w["ffn1_w_down"], 1)
    xp, kw_p, vw_p = _mixer1(xp, xp_kv, w, bp, lp)
    k_new, v_new = _norm_proj(xs_kv, w["kv_norm"], [w["w_k"], w["w_v"]])
    (q,) = _norm_proj(xs, w["mix_norm"][1], [w["attn_w_q"]])
    o, kw_s, vw_s = _attention_decode(
        q.reshape(bs, ls, D_MODEL), k_buf.reshape(bs, WINDOW, KV_WIDTH), k_new.reshape(bs, ls, KV_WIDTH),
        v_buf.reshape(bs, WINDOW, KV_WIDTH), v_new.reshape(bs, ls, KV_WIDTH), w["attn_sinks"])
    xs = _proj_res(o.reshape(bs * ls, D_MODEL), w["attn_w_o"], xs)
    yp, ys = _ffn(xp, xs, w["ffn2_norm"], w["ffn2_w_gu"], w["ffn2_w_down"], 1, fg=w["final_norm"])

    def heads(t, bsz):
        return t.reshape(bsz, WINDOW, N_KV_HEADS, ATT_HEAD_DIM)

    return (yp.reshape(bp, lp, D_MODEL), ys.reshape(bs, ls, D_MODEL), ssm_p[None], conv_p[None],
            heads(kw_p, bp), heads(vw_p, bp), ssm_s[None], conv_s[None], heads(kw_s, bs), heads(vw_s, bs))


def kernel(x_prompt, x_sample, state_ssm, state_conv, cache_k_win, cache_v_win,
           ffn1_norm, ffn1_w_gu, ffn1_w_down, mix_norm, ffn2_norm, ffn2_w_gu, ffn2_w_down,
           ssm_w_in, ssm_conv_w, ssm_conv_b, ssm_dt_bias, ssm_a_log, ssm_d, ssm_gate_norm, ssm_w_out,
           kv_norm, w_kv, attn_w_q, attn_sinks, attn_w_o, final_norm):
    w_in = ssm_w_in[0]
    w_dt = jnp.pad(w_in[:, D_INNER + CONV_DIM:], ((0, 0), (0, LANES - SSM_HEADS)))
    w = dict(
        ffn1_norm=ffn1_norm, ffn2_norm=ffn2_norm, mix_norm=mix_norm, kv_norm=kv_norm,
        final_norm=final_norm, attn_sinks=attn_sinks[0],
        ffn1_w_gu=ffn1_w_gu.astype(BF16), ffn1_w_down=ffn1_w_down.astype(BF16),
        ffn2_w_gu=ffn2_w_gu.astype(BF16), ffn2_w_down=ffn2_w_down.astype(BF16),
        ssm_w_z=w_in[:, :D_INNER].astype(BF16),
        ssm_w_xbc=w_in[:, D_INNER:D_INNER + CONV_DIM].astype(BF16),
        ssm_w_dt=w_dt.astype(BF16),
        ssm_w_dtt=w_in[:, D_INNER + CONV_DIM:].T.astype(BF16),
        ssm_w_out=ssm_w_out[0].astype(BF16),
        w_k=w_kv[:, :KV_WIDTH].astype(BF16), w_v=w_kv[:, KV_WIDTH:].astype(BF16),
        attn_w_q=attn_w_q[0].astype(BF16), attn_w_o=attn_w_o[0].astype(BF16),
        ssm=dict(conv_w=ssm_conv_w[0], conv_b=ssm_conv_b[0], dt_bias=ssm_dt_bias[0],
                 a_log=ssm_a_log[0], d_skip=ssm_d[0], gate_norm=ssm_gate_norm[0]),
    )
    bp = x_prompt.shape[0]
    ssm0 = jnp.zeros((bp, SSM_HEADS, SSM_HEAD_DIM, SSM_D_STATE), F32)
    conv0 = jnp.zeros((bp, CONV_WIDTH - 1, CONV_DIM), F32)
    return _trunk(x_prompt, x_sample, ssm0, conv0, state_ssm[0], state_conv[0],
                  cache_k_win, cache_v_win, w)
```

```python
import functools
import math

import jax
import jax.numpy as jnp
import numpy as np
from jax import lax
from jax.experimental import pallas as pl
from jax.experimental.pallas import tpu as pltpu

F32 = jnp.float32
BF16 = jnp.bfloat16

D_MODEL = 1024
D_FF = 2816
D_INNER = 2048
SSM_HEAD_DIM = 64
SSM_HEADS = 32
SSM_GROUPS = 4
HEADS_PER_GROUP = SSM_HEADS // SSM_GROUPS
SSM_D_STATE = 128
GROUP_WIDTH = D_INNER // SSM_GROUPS
CONV_WIDTH = 4
CONV_DIM = D_INNER + 2 * SSM_GROUPS * SSM_D_STATE
SSD_CHUNK = 128
WINDOW = 128
ATT_HEAD_DIM = 64
N_Q_HEADS = 16
N_KV_HEADS = 4
Q_PER_KV = N_Q_HEADS // N_KV_HEADS
KV_WIDTH = N_KV_HEADS * ATT_HEAD_DIM
PAST_LEN = 8192
NORM_EPS = 1e-5
LOG2E = 1.0 / math.log(2.0)

LANES = 128
SUBLANES = 8
VMEM_LIMIT_BYTES = 56 * 1024 * 1024
FF_CHUNK = 256
ROW_TILE = 1024
PROJ_ROW_TILE = 512
PROJ_COL_CHUNK = 512
MIX_ROWS = 256
DECODE_SEQS = 4

_NT = (((1,), (1,)), ((), ()))
_TN = (((0,), (0,)), ((), ()))


def _rms(x, g):
    ms = jnp.mean(x * x, axis=-1, keepdims=True)
    return x * lax.rsqrt(ms + NORM_EPS) * g


def _silu(x):
    return x * (1.0 / (1.0 + jnp.exp(-x)))


def _softplus(x):
    return jnp.maximum(x, 0.0) + jnp.log1p(jnp.exp(-jnp.abs(x)))


def _params(*sem):
    return pltpu.CompilerParams(dimension_semantics=sem, vmem_limit_bytes=VMEM_LIMIT_BYTES)


def _full(shape):
    return pl.BlockSpec(shape, lambda *_: (0,) * len(shape))


def _ffn_kernel(xa_ref, xb_ref, g_ref, wgu_ref, wd_ref, fg_ref, oa_ref, ob_ref, *, final_norm, na):
    def run(x_ref, o_ref):
        x = x_ref[...]
        h = _rms(x, g_ref[...]).astype(BF16)
        for j in range(D_FF // FF_CHUNK):
            lo = j * FF_CHUNK
            gate = jnp.dot(h, wgu_ref[:, lo:lo + FF_CHUNK], preferred_element_type=F32)
            up = jnp.dot(h, wgu_ref[:, D_FF + lo:D_FF + lo + FF_CHUNK], preferred_element_type=F32)
            act = (_silu(gate) * up).astype(BF16)
            part = jnp.dot(act, wd_ref[lo:lo + FF_CHUNK, :], preferred_element_type=F32)
            if j == 0:
                o_ref[...] = part
            else:
                o_ref[...] += part
        out = x + 0.5 * o_ref[...]
        if final_norm:
            out = _rms(out, fg_ref[...])
        o_ref[...] = out

    i = pl.program_id(0)

    @pl.when(i < na)
    def _():
        run(xa_ref, oa_ref)

    @pl.when(i >= na)
    def _():
        run(xb_ref, ob_ref)


def _ffn(xa, xb, g, wgu, wd, layer, fg=None):
    ta, tb = min(ROW_TILE, xa.shape[0]), min(PROJ_ROW_TILE, xb.shape[0])
    na, nb = xa.shape[0] // ta, xb.shape[0] // tb
    final_norm = fg is not None
    g = g[layer]
    if fg is None:
        fg = g

    def layer_weights(*dims):
        return pl.BlockSpec((None,) + dims, lambda i: (layer, 0, 0), pipeline_mode=pl.Buffered(1))

    spec_a = pl.BlockSpec((ta, D_MODEL), lambda i: (jnp.minimum(i, na - 1), 0))
    spec_b = pl.BlockSpec((tb, D_MODEL), lambda i: (jnp.maximum(i - na, 0), 0))
    return pl.pallas_call(
        functools.partial(_ffn_kernel, final_norm=final_norm, na=na),
        grid=(na + nb,),
        in_specs=[spec_a, spec_b,
                  _full((1, D_MODEL)), layer_weights(D_MODEL, 2 * D_FF), layer_weights(D_FF, D_MODEL),
                  _full((1, D_MODEL))],
        out_specs=[spec_a, spec_b],
        out_shape=[jax.ShapeDtypeStruct(xa.shape, F32), jax.ShapeDtypeStruct(xb.shape, F32)],
        compiler_params=_params("arbitrary"),
        name="ffn",
    )(xa, xb, g.reshape(1, D_MODEL), wgu, wd, fg.reshape(1, D_MODEL))


def _norm_proj_kernel(x_ref, g_ref, *refs):
    n = len(refs) // 2
    h = _rms(x_ref[...], g_ref[...]).astype(BF16)
    for w_ref, o_ref in zip(refs[:n], refs[n:]):
        width = w_ref.shape[1]
        step = min(PROJ_COL_CHUNK, width)
        for lo in range(0, width, step):
            o_ref[:, lo:lo + step] = jnp.dot(h, w_ref[:, lo:lo + step], preferred_element_type=F32)


def _norm_proj(x, g, weights):
    t = x.shape[0]
    tm = min(PROJ_ROW_TILE, t)
    return pl.pallas_call(
        _norm_proj_kernel,
        grid=(t // tm,),
        in_specs=[pl.BlockSpec((tm, D_MODEL), lambda i: (i, 0)), _full((1, D_MODEL))]
        + [_full(w.shape) for w in weights],
        out_specs=[pl.BlockSpec((tm, w.shape[1]), lambda i: (i, 0)) for w in weights],
        out_shape=[jax.ShapeDtypeStruct((t, w.shape[1]), F32) for w in weights],
        compiler_params=_params("parallel"),
        name="norm_proj",
    )(x, g.reshape(1, D_MODEL), *weights)


def _proj_res_kernel(y_ref, w_ref, x_ref, o_ref):
    o_ref[...] = x_ref[...] + jnp.dot(y_ref[...].astype(BF16), w_ref[...], preferred_element_type=F32)


def _proj_res(y, w, x):
    t, k = y.shape
    tm = min(PROJ_ROW_TILE, t)
    return pl.pallas_call(
        _proj_res_kernel,
        grid=(t // tm,),
        in_specs=[pl.BlockSpec((tm, k), lambda i: (i, 0)), _full(w.shape),
                  pl.BlockSpec((tm, D_MODEL), lambda i: (i, 0))],
        out_specs=pl.BlockSpec((tm, D_MODEL), lambda i: (i, 0)),
        out_shape=jax.ShapeDtypeStruct((t, D_MODEL), F32),
        compiler_params=_params("parallel"),
        name="proj_res",
    )(y, w, x)


def _split3(x):
    hi = x.astype(BF16)
    r1 = x - hi.astype(F32)
    mid = r1.astype(BF16)
    lo = (r1 - mid.astype(F32)).astype(BF16)
    return hi, mid, lo


def _conv_silu(ext_scr, n, cw_ref, cb_ref):
    ext = ext_scr[...]
    x0 = ext.reshape(n // SUBLANES + 1, SUBLANES, CONV_DIM)
    sub = lax.broadcasted_iota(jnp.int32, (1, SUBLANES, CONV_DIM), 1)

    def down(a, d):
        rot = pltpu.roll(a, d, 1)
        return jnp.where(sub < d, jnp.concatenate([rot[:1], rot[:-1]], axis=0), rot)

    w = [cw_ref[k:k + 1, :].reshape(1, 1, CONV_DIM) for k in range(CONV_WIDTH)]
    x1 = down(x0, 1)
    near = x0 * w[3] + x1 * w[2]
    far = down(x0 * w[1] + x1 * w[0], 2)
    xc = _silu(((near + far)[1:]).reshape(n, CONV_DIM) + cb_ref[...])
    return xc, ext[n:n + SUBLANES]


def _ssd_scalars(dtt, dtb_ref, alog_ref, lt, ls):
    dt_r = _softplus(dtt + dtb_ref[...])
    da_r = dt_r * -jnp.exp(alog_ref[...])
    upper = (lax.broadcasted_iota(jnp.int32, (ls, ls), 0)
             <= lax.broadcasted_iota(jnp.int32, (ls, ls), 1)).astype(F32).astype(BF16)
    parts = jnp.dot(jnp.concatenate(_split3(da_r), axis=0), upper, preferred_element_type=F32)
    acs_r = (parts[0:SSM_HEADS] + parts[SSM_HEADS:2 * SSM_HEADS]) + parts[2 * SSM_HEADS:]
    last_r = acs_r[:, lt - 1:lt]
    c2_r = acs_r * LOG2E
    r2_r = jnp.log(dt_r) * LOG2E - c2_r
    w_r = dt_r * jnp.exp(last_r - acs_r)
    e_r = jnp.exp(acs_r)
    chunk_decay = jnp.exp(jnp.broadcast_to(last_r, (SSM_HEADS, LANES)))
    col = jnp.concatenate([c2_r, w_r, e_r, jnp.zeros((LANES - 3 * SSM_HEADS, ls), F32)],
                          axis=0).T[0:lt]
    return col, r2_r, chunk_decay


def _decay_rows(col, r2_r, lt, ls):
    r2_c = jnp.concatenate([r2_r, jnp.zeros((LANES - SSM_HEADS, ls), F32)], axis=0).T[0:lt]
    t_idx = lax.broadcasted_iota(jnp.int32, (lt, LANES), 0)
    head_lane = lax.broadcasted_iota(jnp.int32, (lt, LANES), 1) < SSM_HEADS
    return jnp.concatenate(
        [jnp.exp2(jnp.where((t_idx >= s) & head_lane, col + r2_c[s:s + 1, :], -jnp.inf))
         for s in range(lt)], axis=0)


def _ssd_spread(col, spread_ref):
    return jnp.dot(jnp.concatenate(_split3(col), axis=1), spread_ref[...],
                   preferred_element_type=F32)


def _ssd_core(xc, zg, col, r2_r, chunk_decay, wide, state_scr, dskip_ref, gn_ref, pad_scr, lt, ls, emit,
              decay_x=None):
    def pad_rows(v, k):
        if ls == lt:
            return v
        pad_scr[k][...] = jnp.zeros(pad_scr[k].shape, F32)
        pad_scr[k][0:lt, :] = v
        return pad_scr[k][...]

    xs = xc[:, :D_INNER]
    bm = xc[:, D_INNER:D_INNER + GROUP_WIDTH]
    cm = xc[:, D_INNER + GROUP_WIDTH:]
    w_x = wide[:, :D_INNER]
    e_x = wide[:, D_INNER:]

    causal = (lax.broadcasted_iota(jnp.int32, (lt, ls), 1)
              <= lax.broadcasted_iota(jnp.int32, (lt, ls), 0))
    low_half = lax.broadcasted_iota(jnp.int32, (lt, LANES), 1) < SSM_HEAD_DIM
    xs_s = pad_rows(xs, 0) if decay_x is None else None
    bm_s = pad_rows(bm, 1)
    xd_s = pad_rows(xs * w_x, 2)

    for g in range(SSM_GROUPS):
        cm_g = cm[:, g * SSM_D_STATE:(g + 1) * SSM_D_STATE].astype(BF16)
        bm_g = bm_s[:, g * SSM_D_STATE:(g + 1) * SSM_D_STATE].astype(BF16)
        cb_g = lax.dot_general(cm_g, bm_g, _NT, preferred_element_type=F32)
        rows = slice(g * GROUP_WIDTH, (g + 1) * GROUP_WIDTH)
        y_off = lax.dot_general(cm_g, state_scr[rows, :].astype(BF16), _NT,
                                preferred_element_type=F32)
        if decay_x is not None:
            y_diag_g = sum(jnp.broadcast_to(cb_g[:, s:s + 1], (lt, GROUP_WIDTH))
                           * decay_x[s * lt:(s + 1) * lt, rows] * xs[s:s + 1, rows] for s in range(lt))
        tiles = []
        for jj in range(HEADS_PER_GROUP // 2):
            tile = g * (HEADS_PER_GROUP // 2) + jj
            cols = slice(tile * LANES, (tile + 1) * LANES)
            if decay_x is not None:
                y_diag = y_diag_g[:, jj * LANES:(jj + 1) * LANES]
            else:
                xp = xs_s[:, cols].astype(BF16)
                pair = []
                for h in (2 * tile, 2 * tile + 1):
                    expo = col[:, h:h + 1] + r2_r[h:h + 1, :]
                    mat = (cb_g * jnp.exp2(jnp.where(causal, expo, -jnp.inf))).astype(BF16)
                    pair.append(jnp.dot(mat, xp, preferred_element_type=F32))
                y_diag = jnp.where(low_half, pair[0], pair[1])
            tiles.append(y_diag + y_off[:, jj * LANES:(jj + 1) * LANES] * e_x[:, cols]
                         + dskip_ref[:, cols] * xs[:, cols])
        yg = jnp.concatenate(tiles, axis=1) * _silu(zg[:, rows])
        ms = jnp.mean(yg * yg, axis=-1, keepdims=True)
        emit(g, yg * lax.rsqrt(ms + NORM_EPS) * gn_ref[:, rows])

        upd = lax.dot_general(xd_s[:, rows].astype(BF16), bm_g, _TN, preferred_element_type=F32)
        for hh in range(HEADS_PER_GROUP):
            h = g * HEADS_PER_GROUP + hh
            hrows = slice(h * SSM_HEAD_DIM, (h + 1) * SSM_HEAD_DIM)
            state_scr[hrows, :] = (state_scr[hrows, :] * chunk_decay[h:h + 1, :]
                                   + upd[hh * SSM_HEAD_DIM:(hh + 1) * SSM_HEAD_DIM, :])


def _ssd_decode_kernel(xbc_ref, z_ref, dtt_ref, conv0_ref, ssm0_ref,
                       cw_ref, cb_ref, dtb_ref, alog_ref, dskip_ref, gn_ref, spread_ref, headmat_ref,
                       y_ref, ssm_ref, conv_ref, ext_scr, *pad_scr, lt, ls, nseq):
    staged = []
    for i in range(nseq):
        ext_i = ext_scr.at[i]
        ext_i[0:SUBLANES, :] = conv0_ref[i]
        ext_i[SUBLANES:SUBLANES + lt, :] = xbc_ref[i]
        xc, tail = _conv_silu(ext_i, lt, cw_ref, cb_ref)
        conv_ref[i] = tail
        ssm_ref[i] = ssm0_ref[i]
        staged.append((xc,) + _ssd_scalars(dtt_ref[i], dtb_ref, alog_ref, lt, ls))
    wide = _ssd_spread(jnp.concatenate([st[1] for st in staged], axis=0), spread_ref)
    decay_x = jnp.dot(
        jnp.concatenate([_decay_rows(st[1], st[2], lt, ls) for st in staged], axis=0).astype(BF16),
        headmat_ref[...], preferred_element_type=F32)
    for i, (xc, col, r2_r, chunk_decay) in enumerate(staged):

        def emit(g, y, i=i):
            y_ref[i, :, g * GROUP_WIDTH:(g + 1) * GROUP_WIDTH] = y

        _ssd_core(xc, z_ref[i], col, r2_r, chunk_decay, wide[i * lt:(i + 1) * lt], ssm_ref.at[i],
                  dskip_ref, gn_ref, [None] + [p.at[i] for p in pad_scr], lt, ls, emit,
                  decay_x[i * lt * lt:(i + 1) * lt * lt])


def _mixer0_kernel(x_ref, g_ref, wz_ref, wxbc_ref, wdtt_ref, wout_ref, conv0_ref, ssm0_ref,
                   cw_ref, cb_ref, dtb_ref, alog_ref, dskip_ref, gn_ref, spread_ref,
                   o_ref, ssm_ref, conv_ref,
                   ext_scr, state_scr, z_scr, y_scr, *, rb, nsteps):
    c = pl.program_id(1)

    @pl.when(c == 0)
    def _():
        ext_scr[0:SUBLANES, :] = conv0_ref[...]
        state_scr[...] = ssm0_ref[...]

    x = x_ref[...]
    h = _rms(x, g_ref[...]).astype(BF16)
    for lo in range(0, CONV_DIM, PROJ_COL_CHUNK):
        ext_scr[SUBLANES:SUBLANES + rb, lo:lo + PROJ_COL_CHUNK] = jnp.dot(
            h, wxbc_ref[:, lo:lo + PROJ_COL_CHUNK], preferred_element_type=F32)
    for lo in range(0, D_INNER, PROJ_COL_CHUNK):
        z_scr[:, lo:lo + PROJ_COL_CHUNK] = jnp.dot(
            h, wz_ref[:, lo:lo + PROJ_COL_CHUNK], preferred_element_type=F32)
    dtt = lax.dot_general(wdtt_ref[...], h, _NT, preferred_element_type=F32)

    xc, tail = _conv_silu(ext_scr, rb, cw_ref, cb_ref)
    conv_ref[...] = tail
    ext_scr[0:SUBLANES, :] = tail

    chunks = [slice(k * SSD_CHUNK, (k + 1) * SSD_CHUNK) for k in range(rb // SSD_CHUNK)]
    scalars = [_ssd_scalars(dtt[:, rows], dtb_ref, alog_ref, SSD_CHUNK, SSD_CHUNK) for rows in chunks]
    wide = _ssd_spread(jnp.concatenate([sc[0] for sc in scalars], axis=0), spread_ref)
    for rows, (col, r2_r, chunk_decay) in zip(chunks, scalars):

        def emit(g, y, rows=rows):
            y_scr[rows, g * GROUP_WIDTH:(g + 1) * GROUP_WIDTH] = y.astype(BF16)

        _ssd_core(xc[rows], z_scr[rows, :], col, r2_r, chunk_decay, wide[rows], state_scr,
                  dskip_ref, gn_ref, (), SSD_CHUNK, SSD_CHUNK, emit)

    o_ref[...] = x + jnp.dot(y_scr[...], wout_ref[...], preferred_element_type=F32)

    @pl.when(c == nsteps - 1)
    def _():
        ssm_ref[...] = state_scr[...]


def _spread_matrix():
    k = np.arange(LANES)[:, None]
    head = np.arange(D_INNER)[None, :] // SSM_HEAD_DIM
    one = np.concatenate([k == SSM_HEADS + head, k == 2 * SSM_HEADS + head], axis=1)
    return jnp.asarray(np.concatenate([one, one, one], axis=0), dtype=BF16)


def _head_matrix():
    k = np.arange(LANES)[:, None]
    head = np.arange(D_INNER)[None, :] // SSM_HEAD_DIM
    return jnp.asarray(k == head, dtype=BF16)


def _ssd_decode(xbc, z, dt, conv0, ssm0, p, bsz, seq):
    lt, ls, nseq = seq, LANES, DECODE_SEQS
    xbc = xbc.reshape(bsz, seq, CONV_DIM)
    z = z.reshape(bsz, seq, D_INNER)
    dtt = jnp.swapaxes(dt.reshape(bsz, seq, LANES)[:, :, :SSM_HEADS], 1, 2)
    dtt = jnp.pad(dtt, ((0, 0), (0, 0), (0, ls - lt)))
    conv0 = jnp.pad(conv0, ((0, 0), (SUBLANES - (CONV_WIDTH - 1), 0), (0, 0)))
    ssm0 = ssm0.reshape(bsz, D_INNER, SSM_D_STATE)

    def head_rows(v):
        return jnp.broadcast_to(v.astype(F32)[:, None], (SSM_HEADS, ls))

    def seqs(*dims):
        return pl.BlockSpec((nseq,) + dims, lambda b: (b,) + (0,) * len(dims))

    y, ssm, conv = pl.pallas_call(
        functools.partial(_ssd_decode_kernel, lt=lt, ls=ls, nseq=nseq),
        grid=(bsz // nseq,),
        in_specs=[seqs(lt, CONV_DIM), seqs(lt, D_INNER), seqs(SSM_HEADS, ls),
                  seqs(SUBLANES, CONV_DIM), seqs(D_INNER, SSM_D_STATE),
                  _full((CONV_WIDTH, CONV_DIM)), _full((1, CONV_DIM)),
                  _full((SSM_HEADS, ls)), _full((SSM_HEADS, ls)),
                  _full((1, D_INNER)), _full((1, D_INNER)),
                  _full((3 * LANES, 2 * D_INNER)), _full((LANES, D_INNER))],
        out_specs=[seqs(lt, D_INNER), seqs(D_INNER, SSM_D_STATE), seqs(SUBLANES, CONV_DIM)],
        out_shape=[jax.ShapeDtypeStruct((bsz, seq, D_INNER), F32),
                   jax.ShapeDtypeStruct((bsz, D_INNER, SSM_D_STATE), F32),
                   jax.ShapeDtypeStruct((bsz, SUBLANES, CONV_DIM), F32)],
        scratch_shapes=[pltpu.VMEM((nseq, SUBLANES + lt, CONV_DIM), F32),
                        pltpu.VMEM((nseq, ls, GROUP_WIDTH), F32), pltpu.VMEM((nseq, ls, D_INNER), F32)],
        compiler_params=_params("parallel"),
        name="ssd_decode",
    )(xbc, z, dtt, conv0, ssm0,
      p["conv_w"], p["conv_b"].reshape(1, CONV_DIM),
      head_rows(p["dt_bias"]), head_rows(p["a_log"]),
      jnp.repeat(p["d_skip"].astype(F32), SSM_HEAD_DIM).reshape(1, D_INNER),
      p["gate_norm"].reshape(1, D_INNER), _spread_matrix(), _head_matrix())
    return (y.reshape(bsz * seq, D_INNER),
            ssm.reshape(bsz, SSM_HEADS, SSM_HEAD_DIM, SSM_D_STATE),
            conv[:, SUBLANES - (CONV_WIDTH - 1):, :])


def _mixer0(x, conv0, ssm0, w, bsz, seq):
    rb = MIX_ROWS
    nsteps = seq // rb
    p = w["ssm"]
    conv0 = jnp.pad(conv0, ((0, 0), (SUBLANES - (CONV_WIDTH - 1), 0), (0, 0)))
    ssm0 = ssm0.reshape(bsz, D_INNER, SSM_D_STATE)

    def head_rows(v):
        return jnp.broadcast_to(v.astype(F32)[:, None], (SSM_HEADS, SSD_CHUNK))

    row_spec = pl.BlockSpec((rb, D_MODEL), lambda b, c: (b * nsteps + c, 0))
    out, ssm, conv = pl.pallas_call(
        functools.partial(_mixer0_kernel, rb=rb, nsteps=nsteps),
        grid=(bsz, nsteps),
        in_specs=[row_spec,
                  _full((1, D_MODEL)), _full((D_MODEL, D_INNER)), _full((D_MODEL, CONV_DIM)),
                  _full((SSM_HEADS, D_MODEL)), _full((D_INNER, D_MODEL)),
                  pl.BlockSpec((None, SUBLANES, CONV_DIM), lambda b, c: (b, 0, 0)),
                  pl.BlockSpec((None, D_INNER, SSM_D_STATE), lambda b, c: (b, 0, 0)),
                  _full((CONV_WIDTH, CONV_DIM)), _full((1, CONV_DIM)),
                  _full((SSM_HEADS, SSD_CHUNK)), _full((SSM_HEADS, SSD_CHUNK)),
                  _full((1, D_INNER)), _full((1, D_INNER)),
                  _full((3 * LANES, 2 * D_INNER))],
        out_specs=[row_spec,
                   pl.BlockSpec((None, D_INNER, SSM_D_STATE), lambda b, c: (b, 0, 0)),
                   pl.BlockSpec((None, SUBLANES, CONV_DIM), lambda b, c: (b, 0, 0))],
        out_shape=[jax.ShapeDtypeStruct((bsz * seq, D_MODEL), F32),
                   jax.ShapeDtypeStruct((bsz, D_INNER, SSM_D_STATE), F32),
                   jax.ShapeDtypeStruct((bsz, SUBLANES, CONV_DIM), F32)],
        scratch_shapes=[pltpu.VMEM((SUBLANES + rb, CONV_DIM), F32),
                        pltpu.VMEM((D_INNER, SSM_D_STATE), F32),
                        pltpu.VMEM((rb, D_INNER), F32),
                        pltpu.VMEM((rb, D_INNER), BF16)],
        compiler_params=_params("parallel", "arbitrary"),
        name="mixer0",
    )(x, w["mix_norm"][0].reshape(1, D_MODEL), w["ssm_w_z"], w["ssm_w_xbc"], w["ssm_w_dtt"],
      w["ssm_w_out"], conv0, ssm0,
      p["conv_w"], p["conv_b"].reshape(1, CONV_DIM),
      head_rows(p["dt_bias"]), head_rows(p["a_log"]),
      jnp.repeat(p["d_skip"].astype(F32), SSM_HEAD_DIM).reshape(1, D_INNER),
      p["gate_norm"].reshape(1, D_INNER), _spread_matrix())
    return (out, ssm.reshape(bsz, SSM_HEADS, SSM_HEAD_DIM, SSM_D_STATE),
            conv[:, SUBLANES - (CONV_WIDTH - 1):, :])


def _alibi_slope(head):
    return 2.0 ** (-8.0 * (head + 1) / N_Q_HEADS)


def _attn_block(q, keys, vals, sink_ref, first_key_pos, tq):
    nkeys = 2 * WINDOW
    rows = lax.broadcasted_iota(jnp.int32, (Q_PER_KV * tq, nkeys), 0)
    kj = lax.broadcasted_iota(jnp.int32, (Q_PER_KV * tq, nkeys), 1)
    t = rows & (tq - 1)
    sub = lax.shift_right_logical(rows, int(math.log2(tq)))
    dist = WINDOW + t - kj
    valid = (dist >= 0) & (dist < WINDOW) & (first_key_pos + kj >= 0)
    distf = dist.astype(F32)
    low_half = lax.broadcasted_iota(jnp.int32, (nkeys, LANES), 1) < ATT_HEAD_DIM

    def pick(choices):
        return jnp.where(sub == 0, choices[0], jnp.where(sub == 1, choices[1],
                                                         jnp.where(sub == 2, choices[2], choices[3])))

    out = []
    for j in range(N_KV_HEADS):
        cols = slice((j // 2) * LANES, (j // 2 + 1) * LANES)
        kt = keys[:, cols]
        vt = vals[:, cols]
        if j % 2 == 0:
            k_lo = jnp.where(low_half, kt, 0.0)
            v_lo = jnp.where(low_half, vt, 0.0)
        else:
            k_lo = pltpu.roll(jnp.where(low_half, 0.0, kt), ATT_HEAD_DIM, 1)
            v_lo = pltpu.roll(jnp.where(low_half, 0.0, vt), ATT_HEAD_DIM, 1)
        q0 = q[:, (2 * j) * LANES:(2 * j + 1) * LANES] * (ATT_HEAD_DIM ** -0.5)
        q1 = q[:, (2 * j + 1) * LANES:(2 * j + 2) * LANES] * (ATT_HEAD_DIM ** -0.5)
        qs = jnp.concatenate([q0, pltpu.roll(q0, ATT_HEAD_DIM, 1),
                              q1, pltpu.roll(q1, ATT_HEAD_DIM, 1)], axis=0)
        s = lax.dot_general(qs.astype(BF16), k_lo.astype(BF16), _NT, preferred_element_type=F32)
        slope = pick([_alibi_slope(Q_PER_KV * j + g) for g in range(Q_PER_KV)])
        sink = pick([sink_ref[Q_PER_KV * j + g] for g in range(Q_PER_KV)])[:, 0:1]
        s = jnp.where(valid, s - slope * distf, -jnp.inf)
        mx = jnp.maximum(jnp.max(s, axis=-1, keepdims=True), sink)
        e = jnp.exp(s - mx)
        den = jnp.sum(e, axis=-1, keepdims=True) + jnp.exp(sink - mx)
        prob = (e * (1.0 / den)).astype(BF16)
        o = jnp.dot(prob, v_lo.astype(BF16), preferred_element_type=F32)
        out.append(o[0:tq] + pltpu.roll(o[tq:2 * tq], ATT_HEAD_DIM, 1))
        out.append(o[2 * tq:3 * tq] + pltpu.roll(o[3 * tq:4 * tq], ATT_HEAD_DIM, 1))
    return out


def _attn_decode_kernel(q_ref, kp_ref, kc_ref, vp_ref, vc_ref, bias_ref, sink_ref,
                        o_ref, kwin_ref, vwin_ref, k_scr, v_scr, s_scr, *, tq, nseq):
    k_scr[...] = jnp.zeros(k_scr.shape, F32)
    v_scr[...] = jnp.zeros(v_scr.shape, F32)
    low_half = lax.broadcasted_iota(jnp.int32, (tq, LANES), 1) < ATT_HEAD_DIM
    zeros = jnp.zeros((tq, LANES), F32)

    def half(tile, upper):
        return jnp.where(low_half, 0.0, tile) if upper else jnp.where(low_half, tile, 0.0)

    for i in range(nseq):
        k_scr[i, 0:WINDOW, :] = kp_ref[i]
        v_scr[i, 0:WINDOW, :] = vp_ref[i]
        k_scr[i, WINDOW:WINDOW + tq, :] = kc_ref[i]
        v_scr[i, WINDOW:WINDOW + tq, :] = vc_ref[i]
        kwin_ref[i] = k_scr[i, tq:tq + WINDOW, :]
        vwin_ref[i] = v_scr[i, tq:tq + WINDOW, :]
        q = q_ref[i] * (ATT_HEAD_DIM ** -0.5)
        blocks = []
        for h in range(N_Q_HEADS):
            j = h // Q_PER_KV
            piece = half(q[:, (h // 2) * LANES:(h // 2 + 1) * LANES], h % 2 == 1)
            if h % 2 != j % 2:
                piece = pltpu.roll(piece, ATT_HEAD_DIM, 1)
            blocks.append(jnp.concatenate([piece, zeros] if j < 2 else [zeros, piece], axis=1))
        q_all = jnp.concatenate(blocks, axis=0).astype(BF16)
        s_scr[i] = lax.dot_general(q_all, k_scr[i].astype(BF16), _NT,
                                   preferred_element_type=F32)
        s = s_scr[i] - bias_ref[...]
        sink = jnp.concatenate([jnp.full((tq, 1), sink_ref[h], F32) for h in range(N_Q_HEADS)], axis=0)
        mx = jnp.maximum(jnp.max(s, axis=-1, keepdims=True), sink)
        e = jnp.exp(s - mx)
        den = jnp.sum(e, axis=-1, keepdims=True) + jnp.exp(sink - mx)
        prob = (e * (1.0 / den)).astype(BF16)
        o_all = jnp.dot(prob, v_scr[i].astype(BF16), preferred_element_type=F32)
        for m in range(N_Q_HEADS // 2):
            parts = []
            for h in (2 * m, 2 * m + 1):
                j = h // Q_PER_KV
                piece = half(o_all[h * tq:(h + 1) * tq, (j // 2) * LANES:(j // 2 + 1) * LANES],
                             j % 2 == 1)
                if h % 2 != j % 2:
                    piece = pltpu.roll(piece, ATT_HEAD_DIM, 1)
                parts.append(piece)
            o_ref[i, :, m * LANES:(m + 1) * LANES] = parts[0] + parts[1]


def _mixer1_kernel(sink_ref, x_ref, xkv_ref, gq_ref, gkv_ref, wq_ref, wk_ref, wv_ref, wo_ref,
                   o_ref, kwin_ref, vwin_ref, k_scr, v_scr, att_scr, *, rb):
    c = pl.program_id(1)

    @pl.when(c == 0)
    def _():
        k_scr[0:WINDOW, :] = jnp.zeros((WINDOW, KV_WIDTH), F32)
        v_scr[0:WINDOW, :] = jnp.zeros((WINDOW, KV_WIDTH), F32)

    hkv = _rms(xkv_ref[...], gkv_ref[...]).astype(BF16)
    k_scr[WINDOW:WINDOW + rb, :] = jnp.dot(hkv, wk_ref[...], preferred_element_type=F32)
    v_scr[WINDOW:WINDOW + rb, :] = jnp.dot(hkv, wv_ref[...], preferred_element_type=F32)
    x = x_ref[...]
    q = jnp.dot(_rms(x, gq_ref[...]).astype(BF16), wq_ref[...], preferred_element_type=F32)
    for blk in range(rb // WINDOW):
        lo = blk * WINDOW
        tiles = _attn_block(q[lo:lo + WINDOW], k_scr[lo:lo + 2 * WINDOW, :],
                            v_scr[lo:lo + 2 * WINDOW, :], sink_ref, c * rb + lo - WINDOW, WINDOW)
        for m, tile in enumerate(tiles):
            att_scr[lo:lo + WINDOW, m * LANES:(m + 1) * LANES] = tile.astype(BF16)
    o_ref[...] = x + jnp.dot(att_scr[...], wo_ref[...], preferred_element_type=F32)
    k_last = k_scr[rb:rb + WINDOW, :]
    v_last = v_scr[rb:rb + WINDOW, :]
    kwin_ref[...] = k_last
    vwin_ref[...] = v_last
    k_scr[0:WINDOW, :] = k_last
    v_scr[0:WINDOW, :] = v_last


def _mixer1(x, xkv, w, bsz, seq):
    rb = MIX_ROWS
    nsteps = seq // rb
    row_spec = pl.BlockSpec((rb, D_MODEL), lambda b, c: (b * nsteps + c, 0))
    win_spec = pl.BlockSpec((None, WINDOW, KV_WIDTH), lambda b, c: (b, 0, 0))
    return pl.pallas_call(
        functools.partial(_mixer1_kernel, rb=rb),
        grid=(bsz, nsteps),
        in_specs=[pl.BlockSpec(memory_space=pltpu.SMEM), row_spec, row_spec,
                  _full((1, D_MODEL)), _full((1, D_MODEL)),
                  _full((D_MODEL, D_MODEL)), _full((D_MODEL, KV_WIDTH)), _full((D_MODEL, KV_WIDTH)),
                  _full((D_MODEL, D_MODEL))],
        out_specs=[row_spec, win_spec, win_spec],
        out_shape=[jax.ShapeDtypeStruct((bsz * seq, D_MODEL), F32),
                   jax.ShapeDtypeStruct((bsz, WINDOW, KV_WIDTH), F32),
                   jax.ShapeDtypeStruct((bsz, WINDOW, KV_WIDTH), F32)],
        scratch_shapes=[pltpu.VMEM((WINDOW + rb, KV_WIDTH), F32), pltpu.VMEM((WINDOW + rb, KV_WIDTH), F32),
                        pltpu.VMEM((rb, D_MODEL), BF16)],
        compiler_params=_params("parallel", "arbitrary"),
        name="mixer1",
    )(w["attn_sinks"].astype(F32), x, xkv, w["mix_norm"][1].reshape(1, D_MODEL),
      w["kv_norm"].reshape(1, D_MODEL), w["attn_w_q"], w["w_k"], w["w_v"], w["attn_w_o"])


def _attention_decode(q, k_cache, k_new, v_cache, v_new, sinks):
    bsz, seq = q.shape[0], q.shape[1]
    nseq = DECODE_SEQS

    def seqs(*dims):
        return pl.BlockSpec((nseq,) + dims, lambda b: (b,) + (0,) * len(dims))

    t = np.tile(np.arange(seq), N_Q_HEADS)[:, None]
    kj = np.arange(2 * WINDOW)[None, :]
    dist = WINDOW + t - kj
    valid = (dist >= 0) & (dist < WINDOW) & (PAST_LEN - WINDOW + kj >= 0)
    slope = np.repeat([_alibi_slope(h) for h in range(N_Q_HEADS)], seq)[:, None]
    bias = jnp.asarray(np.where(valid, slope * dist, np.inf), dtype=F32)

    return pl.pallas_call(
        functools.partial(_attn_decode_kernel, tq=seq, nseq=nseq),
        grid=(bsz // nseq,),
        in_specs=[seqs(seq, D_MODEL),
                  seqs(WINDOW, KV_WIDTH), seqs(seq, KV_WIDTH), seqs(WINDOW, KV_WIDTH), seqs(seq, KV_WIDTH),
                  _full((N_Q_HEADS * seq, 2 * WINDOW)), pl.BlockSpec(memory_space=pltpu.SMEM)],
        out_specs=[seqs(seq, D_MODEL), seqs(WINDOW, KV_WIDTH), seqs(WINDOW, KV_WIDTH)],
        out_shape=[jax.ShapeDtypeStruct((bsz, seq, D_MODEL), F32),
                   jax.ShapeDtypeStruct((bsz, WINDOW, KV_WIDTH), F32),
                   jax.ShapeDtypeStruct((bsz, WINDOW, KV_WIDTH), F32)],
        scratch_shapes=[pltpu.VMEM((nseq, 2 * WINDOW, KV_WIDTH), F32),
                        pltpu.VMEM((nseq, 2 * WINDOW, KV_WIDTH), F32),
                        pltpu.VMEM((nseq, N_Q_HEADS * seq, 2 * WINDOW), F32)],
        compiler_params=_params("parallel"),
        name="swa_decode",
    )(q, k_cache, k_new, v_cache, v_new, bias, sinks.astype(F32))


def _trunk(xp, xs, ssm_p, conv_p, ssm_s, conv_s, k_buf, v_buf, w):
    bp, lp = xp.shape[0], xp.shape[1]
    bs, ls = xs.shape[0], xs.shape[1]
    xp = xp.reshape(bp * lp, D_MODEL)
    xs = xs.reshape(bs * ls, D_MODEL)

    xp, xs = _ffn(xp, xs, w["ffn1_norm"], w["ffn1_w_gu"], w["ffn1_w_down"], 0)
    xp, ssm_p, conv_p = _mixer0(xp, conv_p, ssm_p, w, bp, lp)
    z, xbc, dt = _norm_proj(xs, w["mix_norm"][0], [w["ssm_w_z"], w["ssm_w_xbc"], w["ssm_w_dt"]])
    y, ssm_s, conv_s = _ssd_decode(xbc, z, dt, conv_s, ssm_s, w["ssm"], bs, ls)
    xs = _proj_res(y, w["ssm_w_out"], xs)
    xp_kv, xs_kv = _ffn(xp, xs, w["ffn2_norm"], w["ffn2_w_gu"], w["ffn2_w_down"], 0)

    xp, xs = _ffn(xp_kv, xs_kv, w["ffn1_norm"], w["ffn1_w_gu"], w["ffn1_w_down"], 1)
    xp, kw_p, vw_p = _mixer1(xp, xp_kv, w, bp, lp)
    k_new, v_new = _norm_proj(xs_kv, w["kv_norm"], [w["w_k"], w["w_v"]])
    (q,) = _norm_proj(xs, w["mix_norm"][1], [w["attn_w_q"]])
    o, kw_s, vw_s = _attention_decode(
        q.reshape(bs, ls, D_MODEL), k_buf.reshape(bs, WINDOW, KV_WIDTH), k_new.reshape(bs, ls, KV_WIDTH),
        v_buf.reshape(bs, WINDOW, KV_WIDTH), v_new.reshape(bs, ls, KV_WIDTH), w["attn_sinks"])
    xs = _proj_res(o.reshape(bs * ls, D_MODEL), w["attn_w_o"], xs)
    yp, ys = _ffn(xp, xs, w["ffn2_norm"], w["ffn2_w_gu"], w["ffn2_w_down"], 1, fg=w["final_norm"])

    def heads(t, bsz):
        return t.reshape(bsz, WINDOW, N_KV_HEADS, ATT_HEAD_DIM)

    return (yp.reshape(bp, lp, D_MODEL), ys.reshape(bs, ls, D_MODEL), ssm_p[None], conv_p[None],
            heads(kw_p, bp), heads(vw_p, bp), ssm_s[None], conv_s[None], heads(kw_s, bs), heads(vw_s, bs))


def kernel(x_prompt, x_sample, state_ssm, state_conv, cache_k_win, cache_v_win,
           ffn1_norm, ffn1_w_gu, ffn1_w_down, mix_norm, ffn2_norm, ffn2_w_gu, ffn2_w_down,
           ssm_w_in, ssm_conv_w, ssm_conv_b, ssm_dt_bias, ssm_a_log, ssm_d, ssm_gate_norm, ssm_w_out,
           kv_norm, w_kv, attn_w_q, attn_sinks, attn_w_o, final_norm):
    w_in = ssm_w_in[0]
    w_dt = jnp.pad(w_in[:, D_INNER + CONV_DIM:], ((0, 0), (0, LANES - SSM_HEADS)))
    w = dict(
        ffn1_norm=ffn1_norm, ffn2_norm=ffn2_norm, mix_norm=mix_norm, kv_norm=kv_norm,
        final_norm=final_norm, attn_sinks=attn_sinks[0],
        ffn1_w_gu=ffn1_w_gu.astype(BF16), ffn1_w_down=ffn1_w_down.astype(BF16),
        ffn2_w_gu=ffn2_w_gu.astype(BF16), ffn2_w_down=ffn2_w_down.astype(BF16),
        ssm_w_z=w_in[:, :D_INNER].astype(BF16),
        ssm_w_xbc=w_in[:, D_INNER:D_INNER + CONV_DIM].astype(BF16),
        ssm_w_dt=w_dt.astype(BF16),
        ssm_w_dtt=w_in[:, D_INNER + CONV_DIM:].T.astype(BF16),
        ssm_w_out=ssm_w_out[0].astype(BF16),
        w_k=w_kv[:, :KV_WIDTH].astype(BF16), w_v=w_kv[:, KV_WIDTH:].astype(BF16),
        attn_w_q=attn_w_q[0].astype(BF16), attn_w_o=attn_w_o[0].astype(BF16),
        ssm=dict(conv_w=ssm_conv_w[0], conv_b=ssm_conv_b[0], dt_bias=ssm_dt_bias[0],
                 a_log=ssm_a_log[0], d_skip=ssm_d[0], gate_norm=ssm_gate_norm[0]),
    )
    bp = x_prompt.shape[0]
    ssm0 = jnp.zeros((bp, SSM_HEADS, SSM_HEAD_DIM, SSM_D_STATE), F32)
    conv0 = jnp.zeros((bp, CONV_WIDTH - 1, CONV_DIM), F32)
    return _trunk(x_prompt, x_sample, ssm0, conv0, state_ssm[0], state_conv[0],
                  cache_k_win, cache_v_win, w)
```

```python
import functools
import math

import jax
import jax.numpy as jnp
import numpy as np
from jax import lax
from jax.experimental import pallas as pl
from jax.experimental.pallas import tpu as pltpu

F32 = jnp.float32
BF16 = jnp.bfloat16

D_MODEL = 1024
D_FF = 2816
D_INNER = 2048
SSM_HEAD_DIM = 64
SSM_HEADS = 32
SSM_GROUPS = 4
HEADS_PER_GROUP = SSM_HEADS // SSM_GROUPS
SSM_D_STATE = 128
GROUP_WIDTH = D_INNER // SSM_GROUPS
CONV_WIDTH = 4
CONV_DIM = D_INNER + 2 * SSM_GROUPS * SSM_D_STATE
SSD_CHUNK = 128
WINDOW = 128
ATT_HEAD_DIM = 64
N_Q_HEADS = 16
N_KV_HEADS = 4
Q_PER_KV = N_Q_HEADS // N_KV_HEADS
KV_WIDTH = N_KV_HEADS * ATT_HEAD_DIM
PAST_LEN = 8192
NORM_EPS = 1e-5
LOG2E = 1.0 / math.log(2.0)

LANES = 128
SUBLANES = 8
VMEM_LIMIT_BYTES = 56 * 1024 * 1024
FF_CHUNK = 256
ROW_TILE = 1024
PROJ_ROW_TILE = 512
PROJ_COL_CHUNK = 512
MIX_ROWS = 256
DECODE_SEQS = 4

_NT = (((1,), (1,)), ((), ()))
_TN = (((0,), (0,)), ((), ()))


def _rms(x, g):
    ms = jnp.mean(x * x, axis=-1, keepdims=True)
    return x * lax.rsqrt(ms + NORM_EPS) * g


def _silu(x):
    return x * (1.0 / (1.0 + jnp.exp(-x)))


def _softplus(x):
    return jnp.maximum(x, 0.0) + jnp.log1p(jnp.exp(-jnp.abs(x)))


def _params(*sem):
    return pltpu.CompilerParams(dimension_semantics=sem, vmem_limit_bytes=VMEM_LIMIT_BYTES)


def _full(shape):
    return pl.BlockSpec(shape, lambda *_: (0,) * len(shape))


def _ffn_kernel(xa_ref, xb_ref, g_ref, wgu_ref, wd_ref, fg_ref, oa_ref, ob_ref, *, final_norm, na):
    def run(x_ref, o_ref):
        x = x_ref[...]
        h = _rms(x, g_ref[...]).astype(BF16)
        for j in range(D_FF // FF_CHUNK):
            lo = j * FF_CHUNK
            gate = jnp.dot(h, wgu_ref[:, lo:lo + FF_CHUNK], preferred_element_type=F32)
            up = jnp.dot(h, wgu_ref[:, D_FF + lo:D_FF + lo + FF_CHUNK], preferred_element_type=F32)
            act = (_silu(gate) * up).astype(BF16)
            part = jnp.dot(act, wd_ref[lo:lo + FF_CHUNK, :], preferred_element_type=F32)
            if j == 0:
                o_ref[...] = part
            else:
                o_ref[...] += part
        out = x + 0.5 * o_ref[...]
        if final_norm:
            out = _rms(out, fg_ref[...])
        o_ref[...] = out

    i = pl.program_id(0)

    @pl.when(i < na)
    def _():
        run(xa_ref, oa_ref)

    @pl.when(i >= na)
    def _():
        run(xb_ref, ob_ref)


def _ffn(xa, xb, g, wgu, wd, layer, fg=None):
    ta, tb = min(ROW_TILE, xa.shape[0]), min(PROJ_ROW_TILE, xb.shape[0])
    na, nb = xa.shape[0] // ta, xb.shape[0] // tb
    final_norm = fg is not None
    g = g[layer]
    if fg is None:
        fg = g

    def layer_weights(*dims):
        return pl.BlockSpec((None,) + dims, lambda i: (layer, 0, 0), pipeline_mode=pl.Buffered(1))

    spec_a = pl.BlockSpec((ta, D_MODEL), lambda i: (jnp.minimum(i, na - 1), 0))
    spec_b = pl.BlockSpec((tb, D_MODEL), lambda i: (jnp.maximum(i - na, 0), 0))
    return pl.pallas_call(
        functools.partial(_ffn_kernel, final_norm=final_norm, na=na),
        grid=(na + nb,),
        in_specs=[spec_a, spec_b,
                  _full((1, D_MODEL)), layer_weights(D_MODEL, 2 * D_FF), layer_weights(D_FF, D_MODEL),
                  _full((1, D_MODEL))],
        out_specs=[spec_a, spec_b],
        out_shape=[jax.ShapeDtypeStruct(xa.shape, F32), jax.ShapeDtypeStruct(xb.shape, F32)],
        compiler_params=_params("arbitrary"),
        name="ffn",
    )(xa, xb, g.reshape(1, D_MODEL), wgu, wd, fg.reshape(1, D_MODEL))


def _norm_proj_kernel(x_ref, g_ref, *refs):
    n = len(refs) // 2
    h = _rms(x_ref[...], g_ref[...]).astype(BF16)
    for w_ref, o_ref in zip(refs[:n], refs[n:]):
        width = w_ref.shape[1]
        step = min(PROJ_COL_CHUNK, width)
        for lo in range(0, width, step):
            o_ref[:, lo:lo + step] = jnp.dot(h, w_ref[:, lo:lo + step], preferred_element_type=F32)


def _norm_proj(x, g, weights):
    t = x.shape[0]
    tm = min(PROJ_ROW_TILE, t)
    return pl.pallas_call(
        _norm_proj_kernel,
        grid=(t // tm,),
        in_specs=[pl.BlockSpec((tm, D_MODEL), lambda i: (i, 0)), _full((1, D_MODEL))]
        + [_full(w.shape) for w in weights],
        out_specs=[pl.BlockSpec((tm, w.shape[1]), lambda i: (i, 0)) for w in weights],
        out_shape=[jax.ShapeDtypeStruct((t, w.shape[1]), F32) for w in weights],
        compiler_params=_params("parallel"),
        name="norm_proj",
    )(x, g.reshape(1, D_MODEL), *weights)


def _proj_res_kernel(y_ref, w_ref, x_ref, o_ref):
    o_ref[...] = x_ref[...] + jnp.dot(y_ref[...].astype(BF16), w_ref[...], preferred_element_type=F32)


def _proj_res(y, w, x):
    t, k = y.shape
    tm = min(PROJ_ROW_TILE, t)
    return pl.pallas_call(
        _proj_res_kernel,
        grid=(t // tm,),
        in_specs=[pl.BlockSpec((tm, k), lambda i: (i, 0)), _full(w.shape),
                  pl.BlockSpec((tm, D_MODEL), lambda i: (i, 0))],
        out_specs=pl.BlockSpec((tm, D_MODEL), lambda i: (i, 0)),
        out_shape=jax.ShapeDtypeStruct((t, D_MODEL), F32),
        compiler_params=_params("parallel"),
        name="proj_res",
    )(y, w, x)


def _split3(x):
    hi = x.astype(BF16)
    r1 = x - hi.astype(F32)
    mid = r1.astype(BF16)
    lo = (r1 - mid.astype(F32)).astype(BF16)
    return hi, mid, lo


def _conv_silu(ext_scr, n, cw_ref, cb_ref):
    ext = ext_scr[...]
    x0 = ext.reshape(n // SUBLANES + 1, SUBLANES, CONV_DIM)
    sub = lax.broadcasted_iota(jnp.int32, (1, SUBLANES, CONV_DIM), 1)

    def down(a, d):
        rot = pltpu.roll(a, d, 1)
        return jnp.where(sub < d, jnp.concatenate([rot[:1], rot[:-1]], axis=0), rot)

    w = [cw_ref[k:k + 1, :].reshape(1, 1, CONV_DIM) for k in range(CONV_WIDTH)]
    x1 = down(x0, 1)
    near = x0 * w[3] + x1 * w[2]
    far = down(x0 * w[1] + x1 * w[0], 2)
    xc = _silu(((near + far)[1:]).reshape(n, CONV_DIM) + cb_ref[...])
    return xc, ext[n:n + SUBLANES]


def _ssd_scalars(dtt, dtb_ref, alog_ref, lt, ls):
    dt_r = _softplus(dtt + dtb_ref[...])
    da_r = dt_r * -jnp.exp(alog_ref[...])
    upper = (lax.broadcasted_iota(jnp.int32, (ls, ls), 0)
             <= lax.broadcasted_iota(jnp.int32, (ls, ls), 1)).astype(F32).astype(BF16)
    parts = jnp.dot(jnp.concatenate(_split3(da_r), axis=0), upper, preferred_element_type=F32)
    acs_r = (parts[0:SSM_HEADS] + parts[SSM_HEADS:2 * SSM_HEADS]) + parts[2 * SSM_HEADS:]
    last_r = acs_r[:, lt - 1:lt]
    c2_r = acs_r * LOG2E
    r2_r = jnp.log(dt_r) * LOG2E - c2_r
    w_r = dt_r * jnp.exp(last_r - acs_r)
    e_r = jnp.exp(acs_r)
    chunk_decay = jnp.exp(jnp.broadcast_to(last_r, (SSM_HEADS, LANES)))
    col = jnp.concatenate([c2_r, w_r, e_r, jnp.zeros((LANES - 3 * SSM_HEADS, ls), F32)],
                          axis=0).T[0:lt]
    return col, r2_r, chunk_decay


def _decay_rows(col, r2_r, lt, ls):
    r2_c = jnp.concatenate([r2_r, jnp.zeros((LANES - SSM_HEADS, ls), F32)], axis=0).T[0:lt]
    t_idx = lax.broadcasted_iota(jnp.int32, (lt, LANES), 0)
    head_lane = lax.broadcasted_iota(jnp.int32, (lt, LANES), 1) < SSM_HEADS
    return jnp.concatenate(
        [jnp.exp2(jnp.where((t_idx >= s) & head_lane, col + r2_c[s:s + 1, :], -jnp.inf))
         for s in range(lt)], axis=0)


def _ssd_spread(col, spread_ref):
    lane = lax.broadcasted_iota(jnp.int32, col.shape, 1)
    used = (lane >= SSM_HEADS) & (lane < 3 * SSM_HEADS)
    hi, mid, lo = _split3(jnp.where(used, col, 0.0))
    first = (hi.astype(F32) + pltpu.roll(mid.astype(F32), 2 * SSM_HEADS, 1)).astype(BF16)
    return jnp.dot(jnp.concatenate([first, lo], axis=1), spread_ref[...], preferred_element_type=F32)


def _ssd_core(xc, zg, col, r2_r, chunk_decay, wide, state_scr, dskip_ref, gn_ref, pad_scr, lt, ls, emit,
              decay_x=None):
    def pad_rows(v, k):
        if ls == lt:
            return v
        pad_scr[k][...] = jnp.zeros(pad_scr[k].shape, F32)
        pad_scr[k][0:lt, :] = v
        return pad_scr[k][...]

    xs = xc[:, :D_INNER]
    bm = xc[:, D_INNER:D_INNER + GROUP_WIDTH]
    cm = xc[:, D_INNER + GROUP_WIDTH:]
    w_x = wide[:, :D_INNER]
    e_x = wide[:, D_INNER:]

    causal = (lax.broadcasted_iota(jnp.int32, (lt, ls), 1)
              <= lax.broadcasted_iota(jnp.int32, (lt, ls), 0))
    low_half = lax.broadcasted_iota(jnp.int32, (lt, LANES), 1) < SSM_HEAD_DIM
    bm_s = pad_rows(bm, 0)
    xd_s = pad_rows(xs * w_x, 1)
    assert decay_x is not None or ls == lt

    for g in range(SSM_GROUPS):
        cm_g = cm[:, g * SSM_D_STATE:(g + 1) * SSM_D_STATE].astype(BF16)
        bm_g = bm_s[:, g * SSM_D_STATE:(g + 1) * SSM_D_STATE].astype(BF16)
        cb_g = lax.dot_general(cm_g, bm_g, _NT, preferred_element_type=F32)
        rows = slice(g * GROUP_WIDTH, (g + 1) * GROUP_WIDTH)
        y_off = lax.dot_general(cm_g, state_scr[rows, :].astype(BF16), _NT,
                                preferred_element_type=F32)
        if decay_x is not None:
            y_diag_g = sum(jnp.broadcast_to(cb_g[:, s:s + 1], (lt, GROUP_WIDTH))
                           * decay_x[s * lt:(s + 1) * lt, rows] * xs[s:s + 1, rows] for s in range(lt))
        tiles = []
        for jj in range(HEADS_PER_GROUP // 2):
            tile = g * (HEADS_PER_GROUP // 2) + jj
            cols = slice(tile * LANES, (tile + 1) * LANES)
            if decay_x is not None:
                y_diag = y_diag_g[:, jj * LANES:(jj + 1) * LANES]
            else:
                mats = []
                for h in (2 * tile, 2 * tile + 1):
                    expo = col[:, h:h + 1] + r2_r[h:h + 1, :]
                    mats.append((cb_g * jnp.exp2(jnp.where(causal, expo, -jnp.inf))).astype(BF16))
                xp = xs[:, cols]
                x_pair = jnp.concatenate([jnp.where(low_half, xp, 0.0), jnp.where(low_half, 0.0, xp)],
                                         axis=0).astype(BF16)
                y_diag = jnp.dot(jnp.concatenate(mats, axis=1), x_pair, preferred_element_type=F32)
            tiles.append(y_diag + y_off[:, jj * LANES:(jj + 1) * LANES] * e_x[:, cols]
                         + dskip_ref[:, cols] * xs[:, cols])
        yg = jnp.concatenate(tiles, axis=1) * _silu(zg[:, rows])
        ms = jnp.mean(yg * yg, axis=-1, keepdims=True)
        emit(g, yg * lax.rsqrt(ms + NORM_EPS) * gn_ref[:, rows])

        upd = lax.dot_general(xd_s[:, rows].astype(BF16), bm_g, _TN, preferred_element_type=F32)
        for hh in range(HEADS_PER_GROUP):
            h = g * HEADS_PER_GROUP + hh
            hrows = slice(h * SSM_HEAD_DIM, (h + 1) * SSM_HEAD_DIM)
            state_scr[hrows, :] = (state_scr[hrows, :] * chunk_decay[h:h + 1, :]
                                   + upd[hh * SSM_HEAD_DIM:(hh + 1) * SSM_HEAD_DIM, :])


def _ssd_decode_kernel(xbc_ref, z_ref, dtt_ref, conv0_ref, ssm0_ref,
                       cw_ref, cb_ref, dtb_ref, alog_ref, dskip_ref, gn_ref, spread_ref, headmat_ref,
                       y_ref, ssm_ref, conv_ref, ext_scr, *pad_scr, lt, ls, nseq):
    staged = []
    for i in range(nseq):
        ext_i = ext_scr.at[i]
        ext_i[0:SUBLANES, :] = conv0_ref[i]
        ext_i[SUBLANES:SUBLANES + lt, :] = xbc_ref[i]
        xc, tail = _conv_silu(ext_i, lt, cw_ref, cb_ref)
        conv_ref[i] = tail
        ssm_ref[i] = ssm0_ref[i]
        staged.append((xc,) + _ssd_scalars(dtt_ref[i], dtb_ref, alog_ref, lt, ls))
    wide = _ssd_spread(jnp.concatenate([st[1] for st in staged], axis=0), spread_ref)
    decay_x = jnp.dot(
        jnp.concatenate([_decay_rows(st[1], st[2], lt, ls) for st in staged], axis=0).astype(BF16),
        headmat_ref[...], preferred_element_type=F32)
    for i, (xc, col, r2_r, chunk_decay) in enumerate(staged):

        def emit(g, y, i=i):
            y_ref[i, :, g * GROUP_WIDTH:(g + 1) * GROUP_WIDTH] = y

        _ssd_core(xc, z_ref[i], col, r2_r, chunk_decay, wide[i * lt:(i + 1) * lt], ssm_ref.at[i],
                  dskip_ref, gn_ref, [p.at[i] for p in pad_scr], lt, ls, emit,
                  decay_x[i * lt * lt:(i + 1) * lt * lt])


def _mixer0_kernel(x_ref, g_ref, wz_ref, wxbc_ref, wdtt_ref, wout_ref, conv0_ref, ssm0_ref,
                   cw_ref, cb_ref, dtb_ref, alog_ref, dskip_ref, gn_ref, spread_ref,
                   o_ref, ssm_ref, conv_ref,
                   ext_scr, state_scr, z_scr, y_scr, *, rb, nsteps):
    c = pl.program_id(1)

    @pl.when(c == 0)
    def _():
        ext_scr[0:SUBLANES, :] = conv0_ref[...]
        state_scr[...] = ssm0_ref[...]

    x = x_ref[...]
    h = _rms(x, g_ref[...]).astype(BF16)
    for lo in range(0, CONV_DIM, PROJ_COL_CHUNK):
        ext_scr[SUBLANES:SUBLANES + rb, lo:lo + PROJ_COL_CHUNK] = jnp.dot(
            h, wxbc_ref[:, lo:lo + PROJ_COL_CHUNK], preferred_element_type=F32)
    for lo in range(0, D_INNER, PROJ_COL_CHUNK):
        z_scr[:, lo:lo + PROJ_COL_CHUNK] = jnp.dot(
            h, wz_ref[:, lo:lo + PROJ_COL_CHUNK], preferred_element_type=F32)
    dtt = lax.dot_general(wdtt_ref[...], h, _NT, preferred_element_type=F32)

    xc, tail = _conv_silu(ext_scr, rb, cw_ref, cb_ref)
    conv_ref[...] = tail
    ext_scr[0:SUBLANES, :] = tail

    chunks = [slice(k * SSD_CHUNK, (k + 1) * SSD_CHUNK) for k in range(rb // SSD_CHUNK)]
    scalars = [_ssd_scalars(dtt[:, rows], dtb_ref, alog_ref, SSD_CHUNK, SSD_CHUNK) for rows in chunks]
    wide = _ssd_spread(jnp.concatenate([sc[0] for sc in scalars], axis=0), spread_ref)
    for rows, (col, r2_r, chunk_decay) in zip(chunks, scalars):

        def emit(g, y, rows=rows):
            y_scr[rows, g * GROUP_WIDTH:(g + 1) * GROUP_WIDTH] = y.astype(BF16)

        _ssd_core(xc[rows], z_scr[rows, :], col, r2_r, chunk_decay, wide[rows], state_scr,
                  dskip_ref, gn_ref, (), SSD_CHUNK, SSD_CHUNK, emit)

    o_ref[...] = x + jnp.dot(y_scr[...], wout_ref[...], preferred_element_type=F32)

    @pl.when(c == nsteps - 1)
    def _():
        ssm_ref[...] = state_scr[...]


def _spread_matrix():
    k = np.arange(2 * LANES)[:, None]
    head = np.arange(D_INNER)[None, :] // SSM_HEAD_DIM
    w_rows = (k == SSM_HEADS + head) | (k == 3 * SSM_HEADS + head) | (k == 5 * SSM_HEADS + head)
    e_rows = (k == 2 * SSM_HEADS + head) | (k == head) | (k == 6 * SSM_HEADS + head)
    return jnp.asarray(np.concatenate([w_rows, e_rows], axis=1), dtype=BF16)


def _head_matrix():
    k = np.arange(LANES)[:, None]
    head = np.arange(D_INNER)[None, :] // SSM_HEAD_DIM
    return jnp.asarray(k == head, dtype=BF16)


def _ssd_decode(xbc, z, dt, conv0, ssm0, p, bsz, seq):
    lt, ls, nseq = seq, LANES, DECODE_SEQS
    xbc = xbc.reshape(bsz, seq, CONV_DIM)
    z = z.reshape(bsz, seq, D_INNER)
    dtt = jnp.swapaxes(dt.reshape(bsz, seq, LANES)[:, :, :SSM_HEADS], 1, 2)
    dtt = jnp.pad(dtt, ((0, 0), (0, 0), (0, ls - lt)))
    conv0 = jnp.pad(conv0, ((0, 0), (SUBLANES - (CONV_WIDTH - 1), 0), (0, 0)))
    ssm0 = ssm0.reshape(bsz, D_INNER, SSM_D_STATE)

    def head_rows(v):
        return jnp.broadcast_to(v.astype(F32)[:, None], (SSM_HEADS, ls))

    def seqs(*dims):
        return pl.BlockSpec((nseq,) + dims, lambda b: (b,) + (0,) * len(dims))

    y, ssm, conv = pl.pallas_call(
        functools.partial(_ssd_decode_kernel, lt=lt, ls=ls, nseq=nseq),
        grid=(bsz // nseq,),
        in_specs=[seqs(lt, CONV_DIM), seqs(lt, D_INNER), seqs(SSM_HEADS, ls),
                  seqs(SUBLANES, CONV_DIM), seqs(D_INNER, SSM_D_STATE),
                  _full((CONV_WIDTH, CONV_DIM)), _full((1, CONV_DIM)),
                  _full((SSM_HEADS, ls)), _full((SSM_HEADS, ls)),
                  _full((1, D_INNER)), _full((1, D_INNER)),
                  _full((2 * LANES, 2 * D_INNER)), _full((LANES, D_INNER))],
        out_specs=[seqs(lt, D_INNER), seqs(D_INNER, SSM_D_STATE), seqs(SUBLANES, CONV_DIM)],
        out_shape=[jax.ShapeDtypeStruct((bsz, seq, D_INNER), F32),
                   jax.ShapeDtypeStruct((bsz, D_INNER, SSM_D_STATE), F32),
                   jax.ShapeDtypeStruct((bsz, SUBLANES, CONV_DIM), F32)],
        scratch_shapes=[pltpu.VMEM((nseq, SUBLANES + lt, CONV_DIM), F32),
                        pltpu.VMEM((nseq, ls, GROUP_WIDTH), F32), pltpu.VMEM((nseq, ls, D_INNER), F32)],
        compiler_params=_params("parallel"),
        name="ssd_decode",
    )(xbc, z, dtt, conv0, ssm0,
      p["conv_w"], p["conv_b"].reshape(1, CONV_DIM),
      head_rows(p["dt_bias"]), head_rows(p["a_log"]),
      jnp.repeat(p["d_skip"].astype(F32), SSM_HEAD_DIM).reshape(1, D_INNER),
      p["gate_norm"].reshape(1, D_INNER), _spread_matrix(), _head_matrix())
    return (y.reshape(bsz * seq, D_INNER),
            ssm.reshape(bsz, SSM_HEADS, SSM_HEAD_DIM, SSM_D_STATE),
            conv[:, SUBLANES - (CONV_WIDTH - 1):, :])


def _mixer0(x, conv0, ssm0, w, bsz, seq):
    rb = MIX_ROWS
    nsteps = seq // rb
    p = w["ssm"]
    conv0 = jnp.pad(conv0, ((0, 0), (SUBLANES - (CONV_WIDTH - 1), 0), (0, 0)))
    ssm0 = ssm0.reshape(bsz, D_INNER, SSM_D_STATE)

    def head_rows(v):
        return jnp.broadcast_to(v.astype(F32)[:, None], (SSM_HEADS, SSD_CHUNK))

    row_spec = pl.BlockSpec((rb, D_MODEL), lambda b, c: (b * nsteps + c, 0))
    out, ssm, conv = pl.pallas_call(
        functools.partial(_mixer0_kernel, rb=rb, nsteps=nsteps),
        grid=(bsz, nsteps),
        in_specs=[row_spec,
                  _full((1, D_MODEL)), _full((D_MODEL, D_INNER)), _full((D_MODEL, CONV_DIM)),
                  _full((SSM_HEADS, D_MODEL)), _full((D_INNER, D_MODEL)),
                  pl.BlockSpec((None, SUBLANES, CONV_DIM), lambda b, c: (b, 0, 0)),
                  pl.BlockSpec((None, D_INNER, SSM_D_STATE), lambda b, c: (b, 0, 0)),
                  _full((CONV_WIDTH, CONV_DIM)), _full((1, CONV_DIM)),
                  _full((SSM_HEADS, SSD_CHUNK)), _full((SSM_HEADS, SSD_CHUNK)),
                  _full((1, D_INNER)), _full((1, D_INNER)),
                  _full((2 * LANES, 2 * D_INNER))],
        out_specs=[row_spec,
                   pl.BlockSpec((None, D_INNER, SSM_D_STATE), lambda b, c: (b, 0, 0)),
                   pl.BlockSpec((None, SUBLANES, CONV_DIM), lambda b, c: (b, 0, 0))],
        out_shape=[jax.ShapeDtypeStruct((bsz * seq, D_MODEL), F32),
                   jax.ShapeDtypeStruct((bsz, D_INNER, SSM_D_STATE), F32),
                   jax.ShapeDtypeStruct((bsz, SUBLANES, CONV_DIM), F32)],
        scratch_shapes=[pltpu.VMEM((SUBLANES + rb, CONV_DIM), F32),
                        pltpu.VMEM((D_INNER, SSM_D_STATE), F32),
                        pltpu.VMEM((rb, D_INNER), F32),
                        pltpu.VMEM((rb, D_INNER), BF16)],
        compiler_params=_params("parallel", "arbitrary"),
        name="mixer0",
    )(x, w["mix_norm"][0].reshape(1, D_MODEL), w["ssm_w_z"], w["ssm_w_xbc"], w["ssm_w_dtt"],
      w["ssm_w_out"], conv0, ssm0,
      p["conv_w"], p["conv_b"].reshape(1, CONV_DIM),
      head_rows(p["dt_bias"]), head_rows(p["a_log"]),
      jnp.repeat(p["d_skip"].astype(F32), SSM_HEAD_DIM).reshape(1, D_INNER),
      p["gate_norm"].reshape(1, D_INNER), _spread_matrix())
    return (out, ssm.reshape(bsz, SSM_HEADS, SSM_HEAD_DIM, SSM_D_STATE),
            conv[:, SUBLANES - (CONV_WIDTH - 1):, :])


def _alibi_slope(head):
    return 2.0 ** (-8.0 * (head + 1) / N_Q_HEADS)


def _attn_block(q, keys, vals, sink_ref, first_key_pos, tq):
    nkeys = 2 * WINDOW
    rows = lax.broadcasted_iota(jnp.int32, (Q_PER_KV * tq, nkeys), 0)
    kj = lax.broadcasted_iota(jnp.int32, (Q_PER_KV * tq, nkeys), 1)
    t = rows & (tq - 1)
    sub = lax.shift_right_logical(rows, int(math.log2(tq)))
    dist = WINDOW + t - kj
    valid = (dist >= 0) & (dist < WINDOW) & (first_key_pos + kj >= 0)
    distf = dist.astype(F32)
    low_half = lax.broadcasted_iota(jnp.int32, (nkeys, LANES), 1) < ATT_HEAD_DIM

    def pick(choices):
        return jnp.where(sub == 0, choices[0], jnp.where(sub == 1, choices[1],
                                                         jnp.where(sub == 2, choices[2], choices[3])))

    out = []
    for j in range(N_KV_HEADS):
        cols = slice((j // 2) * LANES, (j // 2 + 1) * LANES)
        kt = keys[:, cols]
        vt = vals[:, cols]
        if j % 2 == 0:
            k_lo = jnp.where(low_half, kt, 0.0)
            v_lo = jnp.where(low_half, vt, 0.0)
        else:
            k_lo = pltpu.roll(jnp.where(low_half, 0.0, kt), ATT_HEAD_DIM, 1)
            v_lo = pltpu.roll(jnp.where(low_half, 0.0, vt), ATT_HEAD_DIM, 1)
        q0 = q[:, (2 * j) * LANES:(2 * j + 1) * LANES] * (ATT_HEAD_DIM ** -0.5)
        q1 = q[:, (2 * j + 1) * LANES:(2 * j + 2) * LANES] * (ATT_HEAD_DIM ** -0.5)
        qs = jnp.concatenate([q0, pltpu.roll(q0, ATT_HEAD_DIM, 1),
                              q1, pltpu.roll(q1, ATT_HEAD_DIM, 1)], axis=0)
        s = lax.dot_general(qs.astype(BF16), k_lo.astype(BF16), _NT, preferred_element_type=F32)
        slope = pick([_alibi_slope(Q_PER_KV * j + g) for g in range(Q_PER_KV)])
        sink = pick([sink_ref[Q_PER_KV * j + g] for g in range(Q_PER_KV)])[:, 0:1]
        s = jnp.where(valid, s - slope * distf, -jnp.inf)
        mx = jnp.maximum(jnp.max(s, axis=-1, keepdims=True), sink)
        e = jnp.exp(s - mx)
        den = jnp.sum(e, axis=-1, keepdims=True) + jnp.exp(sink - mx)
        prob = (e * (1.0 / den)).astype(BF16)
        o = jnp.dot(prob, v_lo.astype(BF16), preferred_element_type=F32)
        out.append(o[0:tq] + pltpu.roll(o[tq:2 * tq], ATT_HEAD_DIM, 1))
        out.append(o[2 * tq:3 * tq] + pltpu.roll(o[3 * tq:4 * tq], ATT_HEAD_DIM, 1))
    return out


def _attn_decode_kernel(q_ref, kp_ref, kc_ref, vp_ref, vc_ref, bias_ref, sink_ref,
                        o_ref, kwin_ref, vwin_ref, k_scr, v_scr, s_scr, *, tq, nseq):
    k_scr[...] = jnp.zeros(k_scr.shape, F32)
    v_scr[...] = jnp.zeros(v_scr.shape, F32)
    low_half = lax.broadcasted_iota(jnp.int32, (tq, LANES), 1) < ATT_HEAD_DIM
    zeros = jnp.zeros((tq, LANES), F32)

    def half(tile, upper):
        return jnp.where(low_half, 0.0, tile) if upper else jnp.where(low_half, tile, 0.0)

    for i in range(nseq):
        k_scr[i, 0:WINDOW, :] = kp_ref[i]
        v_scr[i, 0:WINDOW, :] = vp_ref[i]
        k_scr[i, WINDOW:WINDOW + tq, :] = kc_ref[i]
        v_scr[i, WINDOW:WINDOW + tq, :] = vc_ref[i]
        kwin_ref[i] = k_scr[i, tq:tq + WINDOW, :]
        vwin_ref[i] = v_scr[i, tq:tq + WINDOW, :]
        q = q_ref[i] * (ATT_HEAD_DIM ** -0.5)
        blocks = []
        for h in range(N_Q_HEADS):
            j = h // Q_PER_KV
            piece = half(q[:, (h // 2) * LANES:(h // 2 + 1) * LANES], h % 2 == 1)
            if h % 2 != j % 2:
                piece = pltpu.roll(piece, ATT_HEAD_DIM, 1)
            blocks.append(jnp.concatenate([piece, zeros] if j < 2 else [zeros, piece], axis=1))
        q_all = jnp.concatenate(blocks, axis=0).astype(BF16)
        s_scr[i] = lax.dot_general(q_all, k_scr[i].astype(BF16), _NT,
                                   preferred_element_type=F32)
        s = s_scr[i] - bias_ref[...]
        sink = jnp.concatenate([jnp.full((tq, 1), sink_ref[h], F32) for h in range(N_Q_HEADS)], axis=0)
        mx = jnp.maximum(jnp.max(s, axis=-1, keepdims=True), sink)
        e = jnp.exp(s - mx)
        den = jnp.sum(e, axis=-1, keepdims=True) + jnp.exp(sink - mx)
        prob = (e * (1.0 / den)).astype(BF16)
        o_all = jnp.dot(prob, v_scr[i].astype(BF16), preferred_element_type=F32)
        for m in range(N_Q_HEADS // 2):
            parts = []
            for h in (2 * m, 2 * m + 1):
                j = h // Q_PER_KV
                piece = half(o_all[h * tq:(h + 1) * tq, (j // 2) * LANES:(j // 2 + 1) * LANES],
                             j % 2 == 1)
                if h % 2 != j % 2:
                    piece = pltpu.roll(piece, ATT_HEAD_DIM, 1)
                parts.append(piece)
            o_ref[i, :, m * LANES:(m + 1) * LANES] = parts[0] + parts[1]


def _mixer1_kernel(sink_ref, x_ref, xkv_ref, gq_ref, gkv_ref, wq_ref, wk_ref, wv_ref, wo_ref,
                   o_ref, kwin_ref, vwin_ref, k_scr, v_scr, att_scr, *, rb):
    c = pl.program_id(1)

    @pl.when(c == 0)
    def _():
        k_scr[0:WINDOW, :] = jnp.zeros((WINDOW, KV_WIDTH), F32)
        v_scr[0:WINDOW, :] = jnp.zeros((WINDOW, KV_WIDTH), F32)

    hkv = _rms(xkv_ref[...], gkv_ref[...]).astype(BF16)
    k_scr[WINDOW:WINDOW + rb, :] = jnp.dot(hkv, wk_ref[...], preferred_element_type=F32)
    v_scr[WINDOW:WINDOW + rb, :] = jnp.dot(hkv, wv_ref[...], preferred_element_type=F32)
    x = x_ref[...]
    q = jnp.dot(_rms(x, gq_ref[...]).astype(BF16), wq_ref[...], preferred_element_type=F32)
    for blk in range(rb // WINDOW):
        lo = blk * WINDOW
        tiles = _attn_block(q[lo:lo + WINDOW], k_scr[lo:lo + 2 * WINDOW, :],
                            v_scr[lo:lo + 2 * WINDOW, :], sink_ref, c * rb + lo - WINDOW, WINDOW)
        for m, tile in enumerate(tiles):
            att_scr[lo:lo + WINDOW, m * LANES:(m + 1) * LANES] = tile.astype(BF16)
    o_ref[...] = x + jnp.dot(att_scr[...], wo_ref[...], preferred_element_type=F32)
    k_last = k_scr[rb:rb + WINDOW, :]
    v_last = v_scr[rb:rb + WINDOW, :]
    kwin_ref[...] = k_last
    vwin_ref[...] = v_last
    k_scr[0:WINDOW, :] = k_last
    v_scr[0:WINDOW, :] = v_last


def _mixer1(x, xkv, w, bsz, seq):
    rb = MIX_ROWS
    nsteps = seq // rb
    row_spec = pl.BlockSpec((rb, D_MODEL), lambda b, c: (b * nsteps + c, 0))
    win_spec = pl.BlockSpec((None, WINDOW, KV_WIDTH), lambda b, c: (b, 0, 0))
    return pl.pallas_call(
        functools.partial(_mixer1_kernel, rb=rb),
        grid=(bsz, nsteps),
        in_specs=[pl.BlockSpec(memory_space=pltpu.SMEM), row_spec, row_spec,
                  _full((1, D_MODEL)), _full((1, D_MODEL)),
                  _full((D_MODEL, D_MODEL)), _full((D_MODEL, KV_WIDTH)), _full((D_MODEL, KV_WIDTH)),
                  _full((D_MODEL, D_MODEL))],
        out_specs=[row_spec, win_spec, win_spec],
        out_shape=[jax.ShapeDtypeStruct((bsz * seq, D_MODEL), F32),
                   jax.ShapeDtypeStruct((bsz, WINDOW, KV_WIDTH), F32),
                   jax.ShapeDtypeStruct((bsz, WINDOW, KV_WIDTH), F32)],
        scratch_shapes=[pltpu.VMEM((WINDOW + rb, KV_WIDTH), F32), pltpu.VMEM((WINDOW + rb, KV_WIDTH), F32),
                        pltpu.VMEM((rb, D_MODEL), BF16)],
        compiler_params=_params("parallel", "arbitrary"),
        name="mixer1",
    )(w["attn_sinks"].astype(F32), x, xkv, w["mix_norm"][1].reshape(1, D_MODEL),
      w["kv_norm"].reshape(1, D_MODEL), w["attn_w_q"], w["w_k"], w["w_v"], w["attn_w_o"])


def _attention_decode(q, k_cache, k_new, v_cache, v_new, sinks):
    bsz, seq = q.shape[0], q.shape[1]
    nseq = DECODE_SEQS

    def seqs(*dims):
        return pl.BlockSpec((nseq,) + dims, lambda b: (b,) + (0,) * len(dims))

    t = np.tile(np.arange(seq), N_Q_HEADS)[:, None]
    kj = np.arange(2 * WINDOW)[None, :]
    dist = WINDOW + t - kj
    valid = (dist >= 0) & (dist < WINDOW) & (PAST_LEN - WINDOW + kj >= 0)
    slope = np.repeat([_alibi_slope(h) for h in range(N_Q_HEADS)], seq)[:, None]
    bias = jnp.asarray(np.where(valid, slope * dist, np.inf), dtype=F32)

    return pl.pallas_call(
        functools.partial(_attn_decode_kernel, tq=seq, nseq=nseq),
        grid=(bsz // nseq,),
        in_specs=[seqs(seq, D_MODEL),
                  seqs(WINDOW, KV_WIDTH), seqs(seq, KV_WIDTH), seqs(WINDOW, KV_WIDTH), seqs(seq, KV_WIDTH),
                  _full((N_Q_HEADS * seq, 2 * WINDOW)), pl.BlockSpec(memory_space=pltpu.SMEM)],
        out_specs=[seqs(seq, D_MODEL), seqs(WINDOW, KV_WIDTH), seqs(WINDOW, KV_WIDTH)],
        out_shape=[jax.ShapeDtypeStruct((bsz, seq, D_MODEL), F32),
                   jax.ShapeDtypeStruct((bsz, WINDOW, KV_WIDTH), F32),
                   jax.ShapeDtypeStruct((bsz, WINDOW, KV_WIDTH), F32)],
        scratch_shapes=[pltpu.VMEM((nseq, 2 * WINDOW, KV_WIDTH), F32),
                        pltpu.VMEM((nseq, 2 * WINDOW, KV_WIDTH), F32),
                        pltpu.VMEM((nseq, N_Q_HEADS * seq, 2 * WINDOW), F32)],
        compiler_params=_params("parallel"),
        name="swa_decode",
    )(q, k_cache, k_new, v_cache, v_new, bias, sinks.astype(F32))


def _trunk(xp, xs, ssm_p, conv_p, ssm_s, conv_s, k_buf, v_buf, w):
    bp, lp = xp.shape[0], xp.shape[1]
    bs, ls = xs.shape[0], xs.shape[1]
    xp = xp.reshape(bp * lp, D_MODEL)
    xs = xs.reshape(bs * ls, D_MODEL)

    xp, xs = _ffn(xp, xs, w["ffn1_norm"], w["ffn1_w_gu"], w["ffn1_w_down"], 0)
    xp, ssm_p, conv_p = _mixer0(xp, conv_p, ssm_p, w, bp, lp)
    z, xbc, dt = _norm_proj(xs, w["mix_norm"][0], [w["ssm_w_z"], w["ssm_w_xbc"], w["ssm_w_dt"]])
    y, ssm_s, conv_s = _ssd_decode(xbc, z, dt, conv_s, ssm_s, w["ssm"], bs, ls)
    xs = _proj_res(y, w["ssm_w_out"], xs)
    xp_kv, xs_kv = _ffn(xp, xs, w["ffn2_norm"], w["ffn2_w_gu"], w["ffn2_w_down"], 0)

    xp, xs = _ffn(xp_kv, xs_kv, w["ffn1_norm"], w["ffn1_w_gu"], w["ffn1_w_down"], 1)
    xp, kw_p, vw_p = _mixer1(xp, xp_kv, w, bp, lp)
    k_new, v_new = _norm_proj(xs_kv, w["kv_norm"], [w["w_k"], w["w_v"]])
    (q,) = _norm_proj(xs, w["mix_norm"][1], [w["attn_w_q"]])
    o, kw_s, vw_s = _attention_decode(
        q.reshape(bs, ls, D_MODEL), k_buf.reshape(bs, WINDOW, KV_WIDTH), k_new.reshape(bs, ls, KV_WIDTH),
        v_buf.reshape(bs, WINDOW, KV_WIDTH), v_new.reshape(bs, ls, KV_WIDTH), w["attn_sinks"])
    xs = _proj_res(o.reshape(bs * ls, D_MODEL), w["attn_w_o"], xs)
    yp, ys = _ffn(xp, xs, w["ffn2_norm"], w["ffn2_w_gu"], w["ffn2_w_down"], 1, fg=w["final_norm"])

    def heads(t, bsz):
        return t.reshape(bsz, WINDOW, N_KV_HEADS, ATT_HEAD_DIM)

    return (yp.reshape(bp, lp, D_MODEL), ys.reshape(bs, ls, D_MODEL), ssm_p[None], conv_p[None],
            heads(kw_p, bp), heads(vw_p, bp), ssm_s[None], conv_s[None], heads(kw_s, bs), heads(vw_s, bs))


def kernel(x_prompt, x_sample, state_ssm, state_conv, cache_k_win, cache_v_win,
           ffn1_norm, ffn1_w_gu, ffn1_w_down, mix_norm, ffn2_norm, ffn2_w_gu, ffn2_w_down,
           ssm_w_in, ssm_conv_w, ssm_conv_b, ssm_dt_bias, ssm_a_log, ssm_d, ssm_gate_norm, ssm_w_out,
           kv_norm, w_kv, attn_w_q, attn_sinks, attn_w_o, final_norm):
    w_in = ssm_w_in[0]
    w_dt = jnp.pad(w_in[:, D_INNER + CONV_DIM:], ((0, 0), (0, LANES - SSM_HEADS)))
    w = dict(
        ffn1_norm=ffn1_norm, ffn2_norm=ffn2_norm, mix_norm=mix_norm, kv_norm=kv_norm,
        final_norm=final_norm, attn_sinks=attn_sinks[0],
        ffn1_w_gu=ffn1_w_gu.astype(BF16), ffn1_w_down=ffn1_w_down.astype(BF16),
        ffn2_w_gu=ffn2_w_gu.astype(BF16), ffn2_w_down=ffn2_w_down.astype(BF16),
        ssm_w_z=w_in[:, :D_INNER].astype(BF16),
        ssm_w_xbc=w_in[:, D_INNER:D_INNER + CONV_DIM].astype(BF16),
        ssm_w_dt=w_dt.astype(BF16),
        ssm_w_dtt=w_in[:, D_INNER + CONV_DIM:].T.astype(BF16),
        ssm_w_out=ssm_w_out[0].astype(BF16),
        w_k=w_kv[:, :KV_WIDTH].astype(BF16), w_v=w_kv[:, KV_WIDTH:].astype(BF16),
        attn_w_q=attn_w_q[0].astype(BF16), attn_w_o=attn_w_o[0].astype(BF16),
        ssm=dict(conv_w=ssm_conv_w[0], conv_b=ssm_conv_b[0], dt_bias=ssm_dt_bias[0],
                 a_log=ssm_a_log[0], d_skip=ssm_d[0], gate_norm=ssm_gate_norm[0]),
    )
    bp = x_prompt.shape[0]
    ssm0 = jnp.zeros((bp, SSM_HEADS, SSM_HEAD_DIM, SSM_D_STATE), F32)
    conv0 = jnp.zeros((bp, CONV_WIDTH - 1, CONV_DIM), F32)
    return _trunk(x_prompt, x_sample, ssm0, conv0, state_ssm[0], state_conv[0],
                  cache_k_win, cache_v_win, w)
```

```python
import functools
import math

import jax
import jax.numpy as jnp
import numpy as np
from jax import lax
from jax.experimental import pallas as pl
from jax.experimental.pallas import tpu as pltpu

F32 = jnp.float32
BF16 = jnp.bfloat16

D_MODEL = 1024
D_FF = 2816
D_INNER = 2048
SSM_HEAD_DIM = 64
SSM_HEADS = 32
SSM_GROUPS = 4
HEADS_PER_GROUP = SSM_HEADS // SSM_GROUPS
SSM_D_STATE = 128
GROUP_WIDTH = D_INNER // SSM_GROUPS
CONV_WIDTH = 4
CONV_DIM = D_INNER + 2 * SSM_GROUPS * SSM_D_STATE
SSD_CHUNK = 128
WINDOW = 128
ATT_HEAD_DIM = 64
N_Q_HEADS = 16
N_KV_HEADS = 4
Q_PER_KV = N_Q_HEADS // N_KV_HEADS
KV_WIDTH = N_KV_HEADS * ATT_HEAD_DIM
PAST_LEN = 8192
NORM_EPS = 1e-5
LOG2E = 1.0 / math.log(2.0)

LANES = 128
SUBLANES = 8
VMEM_LIMIT_BYTES = 56 * 1024 * 1024
FF_CHUNK = 256
ROW_TILE = 1024
PROJ_ROW_TILE = 512
PROJ_COL_CHUNK = 512
MIX_ROWS = 256
DECODE_SEQS = 4

_NT = (((1,), (1,)), ((), ()))
_TN = (((0,), (0,)), ((), ()))


def _rms(x, g):
    ms = jnp.mean(x * x, axis=-1, keepdims=True)
    return x * lax.rsqrt(ms + NORM_EPS) * g


def _silu(x):
    return x * (1.0 / (1.0 + jnp.exp(-x)))


def _softplus(x):
    return jnp.maximum(x, 0.0) + jnp.log1p(jnp.exp(-jnp.abs(x)))


def _params(*sem):
    return pltpu.CompilerParams(dimension_semantics=sem, vmem_limit_bytes=VMEM_LIMIT_BYTES)


def _full(shape):
    return pl.BlockSpec(shape, lambda *_: (0,) * len(shape))


def _ffn_kernel(xa_ref, xb_ref, g_ref, wgu_ref, wd_ref, fg_ref, oa_ref, ob_ref, *, final_norm, na):
    def run(x_ref, o_ref):
        x = x_ref[...]
        h = _rms(x, g_ref[...]).astype(BF16)
        for j in range(D_FF // FF_CHUNK):
            lo = j * FF_CHUNK
            gate = jnp.dot(h, wgu_ref[:, lo:lo + FF_CHUNK], preferred_element_type=F32)
            up = jnp.dot(h, wgu_ref[:, D_FF + lo:D_FF + lo + FF_CHUNK], preferred_element_type=F32)
            act = (_silu(gate) * up).astype(BF16)
            part = jnp.dot(act, wd_ref[lo:lo + FF_CHUNK, :], preferred_element_type=F32)
            if j == 0:
                o_ref[...] = part
            else:
                o_ref[...] += part
        out = x + 0.5 * o_ref[...]
        if final_norm:
            out = _rms(out, fg_ref[...])
        o_ref[...] = out

    i = pl.program_id(0)

    @pl.when(i < na)
    def _():
        run(xa_ref, oa_ref)

    @pl.when(i >= na)
    def _():
        run(xb_ref, ob_ref)


def _ffn(xa, xb, g, wgu, wd, layer, fg=None):
    ta, tb = min(ROW_TILE, xa.shape[0]), min(PROJ_ROW_TILE, xb.shape[0])
    na, nb = xa.shape[0] // ta, xb.shape[0] // tb
    final_norm = fg is not None
    g = g[layer]
    if fg is None:
        fg = g

    def layer_weights(*dims):
        return pl.BlockSpec((None,) + dims, lambda i: (layer, 0, 0), pipeline_mode=pl.Buffered(1))

    spec_a = pl.BlockSpec((ta, D_MODEL), lambda i: (jnp.minimum(i, na - 1), 0))
    spec_b = pl.BlockSpec((tb, D_MODEL), lambda i: (jnp.maximum(i - na, 0), 0))
    return pl.pallas_call(
        functools.partial(_ffn_kernel, final_norm=final_norm, na=na),
        grid=(na + nb,),
        in_specs=[spec_a, spec_b,
                  _full((1, D_MODEL)), layer_weights(D_MODEL, 2 * D_FF), layer_weights(D_FF, D_MODEL),
                  _full((1, D_MODEL))],
        out_specs=[spec_a, spec_b],
        out_shape=[jax.ShapeDtypeStruct(xa.shape, F32), jax.ShapeDtypeStruct(xb.shape, F32)],
        compiler_params=_params("arbitrary"),
        name="ffn",
    )(xa, xb, g.reshape(1, D_MODEL), wgu, wd, fg.reshape(1, D_MODEL))


def _norm_proj_kernel(x_ref, g_ref, *refs):
    n = len(refs) // 2
    h = _rms(x_ref[...], g_ref[...]).astype(BF16)
    for w_ref, o_ref in zip(refs[:n], refs[n:]):
        width = w_ref.shape[1]
        step = min(PROJ_COL_CHUNK, width)
        for lo in range(0, width, step):
            o_ref[:, lo:lo + step] = jnp.dot(h, w_ref[:, lo:lo + step], preferred_element_type=F32)


def _norm_proj(x, g, weights):
    t = x.shape[0]
    tm = min(PROJ_ROW_TILE, t)
    return pl.pallas_call(
        _norm_proj_kernel,
        grid=(t // tm,),
        in_specs=[pl.BlockSpec((tm, D_MODEL), lambda i: (i, 0)), _full((1, D_MODEL))]
        + [_full(w.shape) for w in weights],
        out_specs=[pl.BlockSpec((tm, w.shape[1]), lambda i: (i, 0)) for w in weights],
        out_shape=[jax.ShapeDtypeStruct((t, w.shape[1]), F32) for w in weights],
        compiler_params=_params("parallel"),
        name="norm_proj",
    )(x, g.reshape(1, D_MODEL), *weights)


def _proj_res_kernel(y_ref, w_ref, x_ref, o_ref):
    o_ref[...] = x_ref[...] + jnp.dot(y_ref[...].astype(BF16), w_ref[...], preferred_element_type=F32)


def _proj_res(y, w, x):
    t, k = y.shape
    tm = min(PROJ_ROW_TILE, t)
    return pl.pallas_call(
        _proj_res_kernel,
        grid=(t // tm,),
        in_specs=[pl.BlockSpec((tm, k), lambda i: (i, 0)), _full(w.shape),
                  pl.BlockSpec((tm, D_MODEL), lambda i: (i, 0))],
        out_specs=pl.BlockSpec((tm, D_MODEL), lambda i: (i, 0)),
        out_shape=jax.ShapeDtypeStruct((t, D_MODEL), F32),
        compiler_params=_params("parallel"),
        name="proj_res",
    )(y, w, x)


def _split3(x):
    hi = x.astype(BF16)
    r1 = x - hi.astype(F32)
    mid = r1.astype(BF16)
    lo = (r1 - mid.astype(F32)).astype(BF16)
    return hi, mid, lo


def _conv_silu(ext_scr, n, cw_ref, cb_ref):
    ext = ext_scr[...]
    x0 = ext.reshape(n // SUBLANES + 1, SUBLANES, CONV_DIM)
    sub = lax.broadcasted_iota(jnp.int32, (1, SUBLANES, CONV_DIM), 1)

    def down(a, d):
        rot = pltpu.roll(a, d, 1)
        return jnp.where(sub < d, jnp.concatenate([rot[:1], rot[:-1]], axis=0), rot)

    w = [cw_ref[k:k + 1, :].reshape(1, 1, CONV_DIM) for k in range(CONV_WIDTH)]
    x1 = down(x0, 1)
    near = x0 * w[3] + x1 * w[2]
    far = down(x0 * w[1] + x1 * w[0], 2)
    xc = _silu(((near + far)[1:]).reshape(n, CONV_DIM) + cb_ref[...])
    return xc, ext[n:n + SUBLANES]


def _ssd_scalars(dtt, dtb_ref, alog_ref, lt, ls):
    dt_r = _softplus(dtt + dtb_ref[...])
    da_r = dt_r * -jnp.exp(alog_ref[...])
    upper = (lax.broadcasted_iota(jnp.int32, (ls, ls), 0)
             <= lax.broadcasted_iota(jnp.int32, (ls, ls), 1)).astype(F32).astype(BF16)
    parts = jnp.dot(jnp.concatenate(_split3(da_r), axis=0), upper, preferred_element_type=F32)
    acs_r = (parts[0:SSM_HEADS] + parts[SSM_HEADS:2 * SSM_HEADS]) + parts[2 * SSM_HEADS:]
    last_r = acs_r[:, lt - 1:lt]
    c2_r = acs_r * LOG2E
    r2_r = jnp.log(dt_r) * LOG2E - c2_r
    w_r = dt_r * jnp.exp(last_r - acs_r)
    e_r = jnp.exp(acs_r)
    chunk_decay = jnp.exp(jnp.broadcast_to(last_r, (SSM_HEADS, LANES)))
    col = jnp.concatenate([c2_r, w_r, e_r, jnp.zeros((LANES - 3 * SSM_HEADS, ls), F32)],
                          axis=0).T[0:lt]
    return col, r2_r, chunk_decay


def _decay_rows(col, r2_r, lt, ls):
    r2_c = jnp.concatenate([r2_r, jnp.zeros((LANES - SSM_HEADS, ls), F32)], axis=0).T[0:lt]
    t_idx = lax.broadcasted_iota(jnp.int32, (lt, LANES), 0)
    head_lane = lax.broadcasted_iota(jnp.int32, (lt, LANES), 1) < SSM_HEADS
    return jnp.concatenate(
        [jnp.exp2(jnp.where((t_idx >= s) & head_lane, col + r2_c[s:s + 1, :], -jnp.inf))
         for s in range(lt)], axis=0)


def _ssd_spread(col, spread_ref):
    lane = lax.broadcasted_iota(jnp.int32, col.shape, 1)
    used = (lane >= SSM_HEADS) & (lane < 3 * SSM_HEADS)
    hi, mid, lo = _split3(jnp.where(used, col, 0.0))
    first = (hi.astype(F32) + pltpu.roll(mid.astype(F32), 2 * SSM_HEADS, 1)).astype(BF16)
    return jnp.dot(jnp.concatenate([first, lo], axis=1), spread_ref[...], preferred_element_type=F32)


def _ssd_core(xc, zg, col, r2_r, chunk_decay, wide, state_scr, dskip_ref, gn_ref, pad_scr, lt, ls, emit,
              decay_x=None):
    def pad_rows(v, k):
        if ls == lt:
            return v
        pad_scr[k][...] = jnp.zeros(pad_scr[k].shape, F32)
        pad_scr[k][0:lt, :] = v
        return pad_scr[k][...]

    xs = xc[:, :D_INNER]
    bm = xc[:, D_INNER:D_INNER + GROUP_WIDTH]
    cm = xc[:, D_INNER + GROUP_WIDTH:]
    w_x = wide[:, :D_INNER]
    e_x = wide[:, D_INNER:]

    causal = (lax.broadcasted_iota(jnp.int32, (lt, ls), 1)
              <= lax.broadcasted_iota(jnp.int32, (lt, ls), 0))
    low_half = lax.broadcasted_iota(jnp.int32, (lt, LANES), 1) < SSM_HEAD_DIM
    bm_s = pad_rows(bm, 0)
    xd_s = pad_rows(xs * w_x, 1)
    assert decay_x is not None or ls == lt

    for g in range(SSM_GROUPS):
        cm_g = cm[:, g * SSM_D_STATE:(g + 1) * SSM_D_STATE].astype(BF16)
        bm_g = bm_s[:, g * SSM_D_STATE:(g + 1) * SSM_D_STATE].astype(BF16)
        cb_g = lax.dot_general(cm_g, bm_g, _NT, preferred_element_type=F32)
        rows = slice(g * GROUP_WIDTH, (g + 1) * GROUP_WIDTH)
        y_off = lax.dot_general(cm_g, state_scr[rows, :].astype(BF16), _NT,
                                preferred_element_type=F32)
        if decay_x is not None:
            y_diag_g = sum(jnp.broadcast_to(cb_g[:, s:s + 1], (lt, GROUP_WIDTH))
                           * decay_x[s * lt:(s + 1) * lt, rows] * xs[s:s + 1, rows] for s in range(lt))
        tiles = []
        for jj in range(HEADS_PER_GROUP // 2):
            tile = g * (HEADS_PER_GROUP // 2) + jj
            cols = slice(tile * LANES, (tile + 1) * LANES)
            if decay_x is not None:
                y_diag = y_diag_g[:, jj * LANES:(jj + 1) * LANES]
            else:
                mats = []
                for h in (2 * tile, 2 * tile + 1):
                    expo = col[:, h:h + 1] + r2_r[h:h + 1, :]
                    mats.append((cb_g * jnp.exp2(jnp.where(causal, expo, -jnp.inf))).astype(BF16))
                xp = xs[:, cols]
                x_pair = jnp.concatenate([jnp.where(low_half, xp, 0.0), jnp.where(low_half, 0.0, xp)],
                                         axis=0).astype(BF16)
                y_diag = jnp.dot(jnp.concatenate(mats, axis=1), x_pair, preferred_element_type=F32)
            tiles.append(y_diag + y_off[:, jj * LANES:(jj + 1) * LANES] * e_x[:, cols]
                         + dskip_ref[:, cols] * xs[:, cols])
        yg = jnp.concatenate(tiles, axis=1) * _silu(zg[:, rows])
        ms = jnp.mean(yg * yg, axis=-1, keepdims=True)
        emit(g, yg * lax.rsqrt(ms + NORM_EPS) * gn_ref[:, rows])

        upd = lax.dot_general(xd_s[:, rows].astype(BF16), bm_g, _TN, preferred_element_type=F32)
        for hh in range(HEADS_PER_GROUP):
            h = g * HEADS_PER_GROUP + hh
            hrows = slice(h * SSM_HEAD_DIM, (h + 1) * SSM_HEAD_DIM)
            state_scr[hrows, :] = (state_scr[hrows, :] * chunk_decay[h:h + 1, :]
                                   + upd[hh * SSM_HEAD_DIM:(hh + 1) * SSM_HEAD_DIM, :])


def _ssd_decode_kernel(xbc_ref, z_ref, dtt_ref, conv0_ref, ssm0_ref,
                       cw_ref, cb_ref, dtb_ref, alog_ref, dskip_ref, gn_ref, spread_ref, headmat_ref,
                       y_ref, ssm_ref, conv_ref, ext_scr, *pad_scr, lt, ls, nseq):
    staged = []
    for i in range(nseq):
        ext_i = ext_scr.at[i]
        ext_i[0:SUBLANES, :] = conv0_ref[i]
        ext_i[SUBLANES:SUBLANES + lt, :] = xbc_ref[i]
        xc, tail = _conv_silu(ext_i, lt, cw_ref, cb_ref)
        conv_ref[i] = tail
        ssm_ref[i] = ssm0_ref[i]
        staged.append((xc,) + _ssd_scalars(dtt_ref[i], dtb_ref, alog_ref, lt, ls))
    wide = _ssd_spread(jnp.concatenate([st[1] for st in staged], axis=0), spread_ref)
    decay_x = jnp.dot(
        jnp.concatenate([_decay_rows(st[1], st[2], lt, ls) for st in staged], axis=0).astype(BF16),
        headmat_ref[...], preferred_element_type=F32)
    for i, (xc, col, r2_r, chunk_decay) in enumerate(staged):

        def emit(g, y, i=i):
            y_ref[i, :, g * GROUP_WIDTH:(g + 1) * GROUP_WIDTH] = y

        _ssd_core(xc, z_ref[i], col, r2_r, chunk_decay, wide[i * lt:(i + 1) * lt], ssm_ref.at[i],
                  dskip_ref, gn_ref, [p.at[i] for p in pad_scr], lt, ls, emit,
                  decay_x[i * lt * lt:(i + 1) * lt * lt])


def _mixer0_kernel(x_ref, g_ref, wz_ref, wxbc_ref, wdtt_ref, wout_ref, conv0_ref, ssm0_ref,
                   cw_ref, cb_ref, dtb_ref, alog_ref, dskip_ref, gn_ref, spread_ref,
                   o_ref, ssm_ref, conv_ref,
                   ext_scr, state_scr, z_scr, y_scr, *, rb, nsteps):
    c = pl.program_id(1)

    @pl.when(c == 0)
    def _():
        ext_scr[0:SUBLANES, :] = conv0_ref[...]
        state_scr[...] = ssm0_ref[...]

    x = x_ref[...]
    h = _rms(x, g_ref[...]).astype(BF16)
    for lo in range(0, CONV_DIM, PROJ_COL_CHUNK):
        ext_scr[SUBLANES:SUBLANES + rb, lo:lo + PROJ_COL_CHUNK] = jnp.dot(
            h, wxbc_ref[:, lo:lo + PROJ_COL_CHUNK], preferred_element_type=F32)
    for lo in range(0, D_INNER, PROJ_COL_CHUNK):
        z_scr[:, lo:lo + PROJ_COL_CHUNK] = jnp.dot(
            h, wz_ref[:, lo:lo + PROJ_COL_CHUNK], preferred_element_type=F32)
    dtt = lax.dot_general(wdtt_ref[...], h, _NT, preferred_element_type=F32)

    xc, tail = _conv_silu(ext_scr, rb, cw_ref, cb_ref)
    conv_ref[...] = tail
    ext_scr[0:SUBLANES, :] = tail

    chunks = [slice(k * SSD_CHUNK, (k + 1) * SSD_CHUNK) for k in range(rb // SSD_CHUNK)]
    scalars = [_ssd_scalars(dtt[:, rows], dtb_ref, alog_ref, SSD_CHUNK, SSD_CHUNK) for rows in chunks]
    wide = _ssd_spread(jnp.concatenate([sc[0] for sc in scalars], axis=0), spread_ref)
    for rows, (col, r2_r, chunk_decay) in zip(chunks, scalars):

        def emit(g, y, rows=rows):
            y_scr[rows, g * GROUP_WIDTH:(g + 1) * GROUP_WIDTH] = y.astype(BF16)

        _ssd_core(xc[rows], z_scr[rows, :], col, r2_r, chunk_decay, wide[rows], state_scr,
                  dskip_ref, gn_ref, (), SSD_CHUNK, SSD_CHUNK, emit)

    o_ref[...] = x + jnp.dot(y_scr[...], wout_ref[...], preferred_element_type=F32)

    @pl.when(c == nsteps - 1)
    def _():
        ssm_ref[...] = state_scr[...]


def _spread_matrix():
    k = np.arange(2 * LANES)[:, None]
    head = np.arange(D_INNER)[None, :] // SSM_HEAD_DIM
    w_rows = (k == SSM_HEADS + head) | (k == 3 * SSM_HEADS + head) | (k == 5 * SSM_HEADS + head)
    e_rows = (k == 2 * SSM_HEADS + head) | (k == head) | (k == 6 * SSM_HEADS + head)
    return jnp.asarray(np.concatenate([w_rows, e_rows], axis=1), dtype=BF16)


def _head_matrix():
    k = np.arange(LANES)[:, None]
    head = np.arange(D_INNER)[None, :] // SSM_HEAD_DIM
    return jnp.asarray(k == head, dtype=BF16)


def _ssd_decode(xbc, z, dt, conv0, ssm0, p, bsz, seq):
    lt, ls, nseq = seq, LANES, DECODE_SEQS
    xbc = xbc.reshape(bsz, seq, CONV_DIM)
    z = z.reshape(bsz, seq, D_INNER)
    dtt = jnp.swapaxes(dt.reshape(bsz, seq, LANES)[:, :, :SSM_HEADS], 1, 2)
    dtt = jnp.pad(dtt, ((0, 0), (0, 0), (0, ls - lt)))
    conv0 = jnp.pad(conv0, ((0, 0), (SUBLANES - (CONV_WIDTH - 1), 0), (0, 0)))
    ssm0 = ssm0.reshape(bsz, D_INNER, SSM_D_STATE)

    def head_rows(v):
        return jnp.broadcast_to(v.astype(F32)[:, None], (SSM_HEADS, ls))

    def seqs(*dims):
        return pl.BlockSpec((nseq,) + dims, lambda b: (b,) + (0,) * len(dims))

    y, ssm, conv = pl.pallas_call(
        functools.partial(_ssd_decode_kernel, lt=lt, ls=ls, nseq=nseq),
        grid=(bsz // nseq,),
        in_specs=[seqs(lt, CONV_DIM), seqs(lt, D_INNER), seqs(SSM_HEADS, ls),
                  seqs(SUBLANES, CONV_DIM), seqs(D_INNER, SSM_D_STATE),
                  _full((CONV_WIDTH, CONV_DIM)), _full((1, CONV_DIM)),
                  _full((SSM_HEADS, ls)), _full((SSM_HEADS, ls)),
                  _full((1, D_INNER)), _full((1, D_INNER)),
                  _full((2 * LANES, 2 * D_INNER)), _full((LANES, D_INNER))],
        out_specs=[seqs(lt, D_INNER), seqs(D_INNER, SSM_D_STATE), seqs(SUBLANES, CONV_DIM)],
        out_shape=[jax.ShapeDtypeStruct((bsz, seq, D_INNER), F32),
                   jax.ShapeDtypeStruct((bsz, D_INNER, SSM_D_STATE), F32),
                   jax.ShapeDtypeStruct((bsz, SUBLANES, CONV_DIM), F32)],
        scratch_shapes=[pltpu.VMEM((nseq, SUBLANES + lt, CONV_DIM), F32),
                        pltpu.VMEM((nseq, ls, GROUP_WIDTH), F32), pltpu.VMEM((nseq, ls, D_INNER), F32)],
        compiler_params=_params("parallel"),
        name="ssd_decode",
    )(xbc, z, dtt, conv0, ssm0,
      p["conv_w"], p["conv_b"].reshape(1, CONV_DIM),
      head_rows(p["dt_bias"]), head_rows(p["a_log"]),
      jnp.repeat(p["d_skip"].astype(F32), SSM_HEAD_DIM).reshape(1, D_INNER),
      p["gate_norm"].reshape(1, D_INNER), _spread_matrix(), _head_matrix())
    return (y.reshape(bsz * seq, D_INNER),
            ssm.reshape(bsz, SSM_HEADS, SSM_HEAD_DIM, SSM_D_STATE),
            conv[:, SUBLANES - (CONV_WIDTH - 1):, :])


def _mixer0(x, conv0, ssm0, w, bsz, seq):
    rb = MIX_ROWS
    nsteps = seq // rb
    p = w["ssm"]
    conv0 = jnp.pad(conv0, ((0, 0), (SUBLANES - (CONV_WIDTH - 1), 0), (0, 0)))
    ssm0 = ssm0.reshape(bsz, D_INNER, SSM_D_STATE)

    def head_rows(v):
        return jnp.broadcast_to(v.astype(F32)[:, None], (SSM_HEADS, SSD_CHUNK))

    row_spec = pl.BlockSpec((rb, D_MODEL), lambda b, c: (b * nsteps + c, 0))
    out, ssm, conv = pl.pallas_call(
        functools.partial(_mixer0_kernel, rb=rb, nsteps=nsteps),
        grid=(bsz, nsteps),
        in_specs=[row_spec,
                  _full((1, D_MODEL)), _full((D_MODEL, D_INNER)), _full((D_MODEL, CONV_DIM)),
                  _full((SSM_HEADS, D_MODEL)), _full((D_INNER, D_MODEL)),
                  pl.BlockSpec((None, SUBLANES, CONV_DIM), lambda b, c: (b, 0, 0)),
                  pl.BlockSpec((None, D_INNER, SSM_D_STATE), lambda b, c: (b, 0, 0)),
                  _full((CONV_WIDTH, CONV_DIM)), _full((1, CONV_DIM)),
                  _full((SSM_HEADS, SSD_CHUNK)), _full((SSM_HEADS, SSD_CHUNK)),
                  _full((1, D_INNER)), _full((1, D_INNER)),
                  _full((2 * LANES, 2 * D_INNER))],
        out_specs=[row_spec,
                   pl.BlockSpec((None, D_INNER, SSM_D_STATE), lambda b, c: (b, 0, 0)),
                   pl.BlockSpec((None, SUBLANES, CONV_DIM), lambda b, c: (b, 0, 0))],
        out_shape=[jax.ShapeDtypeStruct((bsz * seq, D_MODEL), F32),
                   jax.ShapeDtypeStruct((bsz, D_INNER, SSM_D_STATE), F32),
                   jax.ShapeDtypeStruct((bsz, SUBLANES, CONV_DIM), F32)],
        scratch_shapes=[pltpu.VMEM((SUBLANES + rb, CONV_DIM), F32),
                        pltpu.VMEM((D_INNER, SSM_D_STATE), F32),
                        pltpu.VMEM((rb, D_INNER), F32),
                        pltpu.VMEM((rb, D_INNER), BF16)],
        compiler_params=_params("parallel", "arbitrary"),
        name="mixer0",
    )(x, w["mix_norm"][0].reshape(1, D_MODEL), w["ssm_w_z"], w["ssm_w_xbc"], w["ssm_w_dtt"],
      w["ssm_w_out"], conv0, ssm0,
      p["conv_w"], p["conv_b"].reshape(1, CONV_DIM),
      head_rows(p["dt_bias"]), head_rows(p["a_log"]),
      jnp.repeat(p["d_skip"].astype(F32), SSM_HEAD_DIM).reshape(1, D_INNER),
      p["gate_norm"].reshape(1, D_INNER), _spread_matrix())
    return (out, ssm.reshape(bsz, SSM_HEADS, SSM_HEAD_DIM, SSM_D_STATE),
            conv[:, SUBLANES - (CONV_WIDTH - 1):, :])


def _alibi_slope(head):
    return 2.0 ** (-8.0 * (head + 1) / N_Q_HEADS)


def _attn_block(q, keys, vals, sink_ref, first_key_pos, tq):
    nkeys = 2 * WINDOW
    rows = lax.broadcasted_iota(jnp.int32, (Q_PER_KV * tq, WINDOW), 0)
    slot = lax.broadcasted_iota(jnp.int32, (Q_PER_KV * tq, WINDOW), 1)
    t = rows & (tq - 1)
    sub = lax.shift_right_logical(rows, int(math.log2(tq)))
    from_prev = slot > t
    distf = jnp.where(from_prev, WINDOW + t - slot, t - slot).astype(F32)
    valid = jnp.logical_not(from_prev) | (first_key_pos >= 0)
    low_half = lax.broadcasted_iota(jnp.int32, (nkeys, LANES), 1) < ATT_HEAD_DIM

    def pick(choices):
        return jnp.where(sub == 0, choices[0], jnp.where(sub == 1, choices[1],
                                                         jnp.where(sub == 2, choices[2], choices[3])))

    out = []
    for j in range(N_KV_HEADS):
        cols = slice((j // 2) * LANES, (j // 2 + 1) * LANES)
        kt = keys[:, cols]
        vt = vals[:, cols]
        if j % 2 == 0:
            k_lo = jnp.where(low_half, kt, 0.0)
            v_lo = jnp.where(low_half, vt, 0.0)
        else:
            k_lo = pltpu.roll(jnp.where(low_half, 0.0, kt), ATT_HEAD_DIM, 1)
            v_lo = pltpu.roll(jnp.where(low_half, 0.0, vt), ATT_HEAD_DIM, 1)
        q0 = q[:, (2 * j) * LANES:(2 * j + 1) * LANES] * (ATT_HEAD_DIM ** -0.5)
        q1 = q[:, (2 * j + 1) * LANES:(2 * j + 2) * LANES] * (ATT_HEAD_DIM ** -0.5)
        qs = jnp.concatenate([q0, pltpu.roll(q0, ATT_HEAD_DIM, 1),
                              q1, pltpu.roll(q1, ATT_HEAD_DIM, 1)], axis=0)
        s2 = lax.dot_general(qs.astype(BF16), k_lo.astype(BF16), _NT, preferred_element_type=F32)
        s = jnp.where(from_prev, s2[:, :WINDOW], s2[:, WINDOW:])
        slope = pick([_alibi_slope(Q_PER_KV * j + g) for g in range(Q_PER_KV)])
        sink = pick([sink_ref[Q_PER_KV * j + g] for g in range(Q_PER_KV)])[:, 0:1]
        s = jnp.where(valid, s - slope * distf, -jnp.inf)
        mx = jnp.maximum(jnp.max(s, axis=-1, keepdims=True), sink)
        e = jnp.exp(s - mx)
        den = jnp.sum(e, axis=-1, keepdims=True) + jnp.exp(sink - mx)
        prob = e * (1.0 / den)
        prob2 = jnp.concatenate([jnp.where(from_prev, prob, 0.0), jnp.where(from_prev, 0.0, prob)],
                                axis=1).astype(BF16)
        o = jnp.dot(prob2, v_lo.astype(BF16), preferred_element_type=F32)
        out.append(o[0:tq] + pltpu.roll(o[tq:2 * tq], ATT_HEAD_DIM, 1))
        out.append(o[2 * tq:3 * tq] + pltpu.roll(o[3 * tq:4 * tq], ATT_HEAD_DIM, 1))
    return out


def _attn_decode_kernel(q_ref, kp_ref, kc_ref, vp_ref, vc_ref, bias_ref, sink_ref,
                        o_ref, kwin_ref, vwin_ref, k_scr, v_scr, s_scr, *, tq, nseq):
    k_scr[...] = jnp.zeros(k_scr.shape, F32)
    v_scr[...] = jnp.zeros(v_scr.shape, F32)
    low_half = lax.broadcasted_iota(jnp.int32, (tq, LANES), 1) < ATT_HEAD_DIM
    zeros = jnp.zeros((tq, LANES), F32)

    def half(tile, upper):
        return jnp.where(low_half, 0.0, tile) if upper else jnp.where(low_half, tile, 0.0)

    for i in range(nseq):
        k_scr[i, 0:WINDOW, :] = kp_ref[i]
        v_scr[i, 0:WINDOW, :] = vp_ref[i]
        k_scr[i, WINDOW:WINDOW + tq, :] = kc_ref[i]
        v_scr[i, WINDOW:WINDOW + tq, :] = vc_ref[i]
        kwin_ref[i] = k_scr[i, tq:tq + WINDOW, :]
        vwin_ref[i] = v_scr[i, tq:tq + WINDOW, :]
        q = q_ref[i] * (ATT_HEAD_DIM ** -0.5)
        blocks = []
        for h in range(N_Q_HEADS):
            j = h // Q_PER_KV
            piece = half(q[:, (h // 2) * LANES:(h // 2 + 1) * LANES], h % 2 == 1)
            if h % 2 != j % 2:
                piece = pltpu.roll(piece, ATT_HEAD_DIM, 1)
            blocks.append(jnp.concatenate([piece, zeros] if j < 2 else [zeros, piece], axis=1))
        q_all = jnp.concatenate(blocks, axis=0).astype(BF16)
        s_scr[i] = lax.dot_general(q_all, k_scr[i].astype(BF16), _NT,
                                   preferred_element_type=F32)
        s = s_scr[i] - bias_ref[...]
        sink = jnp.concatenate([jnp.full((tq, 1), sink_ref[h], F32) for h in range(N_Q_HEADS)], axis=0)
        mx = jnp.maximum(jnp.max(s, axis=-1, keepdims=True), sink)
        e = jnp.exp(s - mx)
        den = jnp.sum(e, axis=-1, keepdims=True) + jnp.exp(sink - mx)
        prob = (e * (1.0 / den)).astype(BF16)
        o_all = jnp.dot(prob, v_scr[i].astype(BF16), preferred_element_type=F32)
        for m in range(N_Q_HEADS // 2):
            parts = []
            for h in (2 * m, 2 * m + 1):
                j = h // Q_PER_KV
                piece = half(o_all[h * tq:(h + 1) * tq, (j // 2) * LANES:(j // 2 + 1) * LANES],
                             j % 2 == 1)
                if h % 2 != j % 2:
                    piece = pltpu.roll(piece, ATT_HEAD_DIM, 1)
                parts.append(piece)
            o_ref[i, :, m * LANES:(m + 1) * LANES] = parts[0] + parts[1]


def _mixer1_kernel(sink_ref, x_ref, xkv_ref, gq_ref, gkv_ref, wq_ref, wk_ref, wv_ref, wo_ref,
                   o_ref, kwin_ref, vwin_ref, k_scr, v_scr, att_scr, *, rb):
    c = pl.program_id(1)

    @pl.when(c == 0)
    def _():
        k_scr[0:WINDOW, :] = jnp.zeros((WINDOW, KV_WIDTH), F32)
        v_scr[0:WINDOW, :] = jnp.zeros((WINDOW, KV_WIDTH), F32)

    hkv = _rms(xkv_ref[...], gkv_ref[...]).astype(BF16)
    k_scr[WINDOW:WINDOW + rb, :] = jnp.dot(hkv, wk_ref[...], preferred_element_type=F32)
    v_scr[WINDOW:WINDOW + rb, :] = jnp.dot(hkv, wv_ref[...], preferred_element_type=F32)
    x = x_ref[...]
    q = jnp.dot(_rms(x, gq_ref[...]).astype(BF16), wq_ref[...], preferred_element_type=F32)
    for blk in range(rb // WINDOW):
        lo = blk * WINDOW
        tiles = _attn_block(q[lo:lo + WINDOW], k_scr[lo:lo + 2 * WINDOW, :],
                            v_scr[lo:lo + 2 * WINDOW, :], sink_ref, c * rb + lo - WINDOW, WINDOW)
        for m, tile in enumerate(tiles):
            att_scr[lo:lo + WINDOW, m * LANES:(m + 1) * LANES] = tile.astype(BF16)
    o_ref[...] = x + jnp.dot(att_scr[...], wo_ref[...], preferred_element_type=F32)
    k_last = k_scr[rb:rb + WINDOW, :]
    v_last = v_scr[rb:rb + WINDOW, :]
    kwin_ref[...] = k_last
    vwin_ref[...] = v_last
    k_scr[0:WINDOW, :] = k_last
    v_scr[0:WINDOW, :] = v_last


def _mixer1(x, xkv, w, bsz, seq):
    rb = MIX_ROWS
    nsteps = seq // rb
    row_spec = pl.BlockSpec((rb, D_MODEL), lambda b, c: (b * nsteps + c, 0))
    win_spec = pl.BlockSpec((None, WINDOW, KV_WIDTH), lambda b, c: (b, 0, 0))
    return pl.pallas_call(
        functools.partial(_mixer1_kernel, rb=rb),
        grid=(bsz, nsteps),
        in_specs=[pl.BlockSpec(memory_space=pltpu.SMEM), row_spec, row_spec,
                  _full((1, D_MODEL)), _full((1, D_MODEL)),
                  _full((D_MODEL, D_MODEL)), _full((D_MODEL, KV_WIDTH)), _full((D_MODEL, KV_WIDTH)),
                  _full((D_MODEL, D_MODEL))],
        out_specs=[row_spec, win_spec, win_spec],
        out_shape=[jax.ShapeDtypeStruct((bsz * seq, D_MODEL), F32),
                   jax.ShapeDtypeStruct((bsz, WINDOW, KV_WIDTH), F32),
                   jax.ShapeDtypeStruct((bsz, WINDOW, KV_WIDTH), F32)],
        scratch_shapes=[pltpu.VMEM((WINDOW + rb, KV_WIDTH), F32), pltpu.VMEM((WINDOW + rb, KV_WIDTH), F32),
                        pltpu.VMEM((rb, D_MODEL), BF16)],
        compiler_params=_params("parallel", "arbitrary"),
        name="mixer1",
    )(w["attn_sinks"].astype(F32), x, xkv, w["mix_norm"][1].reshape(1, D_MODEL),
      w["kv_norm"].reshape(1, D_MODEL), w["attn_w_q"], w["w_k"], w["w_v"], w["attn_w_o"])


def _attention_decode(q, k_cache, k_new, v_cache, v_new, sinks):
    bsz, seq = q.shape[0], q.shape[1]
    nseq = DECODE_SEQS

    def seqs(*dims):
        return pl.BlockSpec((nseq,) + dims, lambda b: (b,) + (0,) * len(dims))

    t = np.tile(np.arange(seq), N_Q_HEADS)[:, None]
    kj = np.arange(2 * WINDOW)[None, :]
    dist = WINDOW + t - kj
    valid = (dist >= 0) & (dist < WINDOW) & (PAST_LEN - WINDOW + kj >= 0)
    slope = np.repeat([_alibi_slope(h) for h in range(N_Q_HEADS)], seq)[:, None]
    bias = jnp.asarray(np.where(valid, slope * dist, np.inf), dtype=F32)

    return pl.pallas_call(
        functools.partial(_attn_decode_kernel, tq=seq, nseq=nseq),
        grid=(bsz // nseq,),
        in_specs=[seqs(seq, D_MODEL),
                  seqs(WINDOW, KV_WIDTH), seqs(seq, KV_WIDTH), seqs(WINDOW, KV_WIDTH), seqs(seq, KV_WIDTH),
                  _full((N_Q_HEADS * seq, 2 * WINDOW)), pl.BlockSpec(memory_space=pltpu.SMEM)],
        out_specs=[seqs(seq, D_MODEL), seqs(WINDOW, KV_WIDTH), seqs(WINDOW, KV_WIDTH)],
        out_shape=[jax.ShapeDtypeStruct((bsz, seq, D_MODEL), F32),
                   jax.ShapeDtypeStruct((bsz, WINDOW, KV_WIDTH), F32),
                   jax.ShapeDtypeStruct((bsz, WINDOW, KV_WIDTH), F32)],
        scratch_shapes=[pltpu.VMEM((nseq, 2 * WINDOW, KV_WIDTH), F32),
                        pltpu.VMEM((nseq, 2 * WINDOW, KV_WIDTH), F32),
                        pltpu.VMEM((nseq, N_Q_HEADS * seq, 2 * WINDOW), F32)],
        compiler_params=_params("parallel"),
        name="swa_decode",
    )(q, k_cache, k_new, v_cache, v_new, bias, sinks.astype(F32))


def _trunk(xp, xs, ssm_p, conv_p, ssm_s, conv_s, k_buf, v_buf, w):
    bp, lp = xp.shape[0], xp.shape[1]
    bs, ls = xs.shape[0], xs.shape[1]
    xp = xp.reshape(bp * lp, D_MODEL)
    xs = xs.reshape(bs * ls, D_MODEL)

    xp, xs = _ffn(xp, xs, w["ffn1_norm"], w["ffn1_w_gu"], w["ffn1_w_down"], 0)
    xp, ssm_p, conv_p = _mixer0(xp, conv_p, ssm_p, w, bp, lp)
    z, xbc, dt = _norm_proj(xs, w["mix_norm"][0], [w["ssm_w_z"], w["ssm_w_xbc"], w["ssm_w_dt"]])
    y, ssm_s, conv_s = _ssd_decode(xbc, z, dt, conv_s, ssm_s, w["ssm"], bs, ls)
    xs = _proj_res(y, w["ssm_w_out"], xs)
    xp_kv, xs_kv = _ffn(xp, xs, w["ffn2_norm"], w["ffn2_w_gu"], w["ffn2_w_down"], 0)

    xp, xs = _ffn(xp_kv, xs_kv, w["ffn1_norm"], w["ffn1_w_gu"], w["ffn1_w_down"], 1)
    xp, kw_p, vw_p = _mixer1(xp, xp_kv, w, bp, lp)
    k_new, v_new = _norm_proj(xs_kv, w["kv_norm"], [w["w_k"], w["w_v"]])
    (q,) = _norm_proj(xs, w["mix_norm"][1], [w["attn_w_q"]])
    o, kw_s, vw_s = _attention_decode(
        q.reshape(bs, ls, D_MODEL), k_buf.reshape(bs, WINDOW, KV_WIDTH), k_new.reshape(bs, ls, KV_WIDTH),
        v_buf.reshape(bs, WINDOW, KV_WIDTH), v_new.reshape(bs, ls, KV_WIDTH), w["attn_sinks"])
    xs = _proj_res(o.reshape(bs * ls, D_MODEL), w["attn_w_o"], xs)
    yp, ys = _ffn(xp, xs, w["ffn2_norm"], w["ffn2_w_gu"], w["ffn2_w_down"], 1, fg=w["final_norm"])

    def heads(t, bsz):
        return t.reshape(bsz, WINDOW, N_KV_HEADS, ATT_HEAD_DIM)

    return (yp.reshape(bp, lp, D_MODEL), ys.reshape(bs, ls, D_MODEL), ssm_p[None], conv_p[None],
            heads(kw_p, bp), heads(vw_p, bp), ssm_s[None], conv_s[None], heads(kw_s, bs), heads(vw_s, bs))


def kernel(x_prompt, x_sample, state_ssm, state_conv, cache_k_win, cache_v_win,
           ffn1_norm, ffn1_w_gu, ffn1_w_down, mix_norm, ffn2_norm, ffn2_w_gu, ffn2_w_down,
           ssm_w_in, ssm_conv_w, ssm_conv_b, ssm_dt_bias, ssm_a_log, ssm_d, ssm_gate_norm, ssm_w_out,
           kv_norm, w_kv, attn_w_q, attn_sinks, attn_w_o, final_norm):
    w_in = ssm_w_in[0]
    w_dt = jnp.pad(w_in[:, D_INNER + CONV_DIM:], ((0, 0), (0, LANES - SSM_HEADS)))
    w = dict(
        ffn1_norm=ffn1_norm, ffn2_norm=ffn2_norm, mix_norm=mix_norm, kv_norm=kv_norm,
        final_norm=final_norm, attn_sinks=attn_sinks[0],
        ffn1_w_gu=ffn1_w_gu.astype(BF16), ffn1_w_down=ffn1_w_down.astype(BF16),
        ffn2_w_gu=ffn2_w_gu.astype(BF16), ffn2_w_down=ffn2_w_down.astype(BF16),
        ssm_w_z=w_in[:, :D_INNER].astype(BF16),
        ssm_w_xbc=w_in[:, D_INNER:D_INNER + CONV_DIM].astype(BF16),
        ssm_w_dt=w_dt.astype(BF16),
        ssm_w_dtt=w_in[:, D_INNER + CONV_DIM:].T.astype(BF16),
        ssm_w_out=ssm_w_out[0].astype(BF16),
        w_k=w_kv[:, :KV_WIDTH].astype(BF16), w_v=w_kv[:, KV_WIDTH:].astype(BF16),
        attn_w_q=attn_w_q[0].astype(BF16), attn_w_o=attn_w_o[0].astype(BF16),
        ssm=dict(conv_w=ssm_conv_w[0], conv_b=ssm_conv_b[0], dt_bias=ssm_dt_bias[0],
                 a_log=ssm_a_log[0], d_skip=ssm_d[0], gate_norm=ssm_gate_norm[0]),
    )
    bp = x_prompt.shape[0]
    ssm0 = jnp.zeros((bp, SSM_HEADS, SSM_HEAD_DIM, SSM_D_STATE), F32)
    conv0 = jnp.zeros((bp, CONV_WIDTH - 1, CONV_DIM), F32)
    return _trunk(x_prompt, x_sample, ssm0, conv0, state_ssm[0], state_conv[0],
                  cache_k_win, cache_v_win, w)
```

```python
import functools
import math

import jax
import jax.numpy as jnp
import numpy as np
from jax import lax
from jax.experimental import pallas as pl
from jax.experimental.pallas import tpu as pltpu

F32 = jnp.float32
BF16 = jnp.bfloat16

D_MODEL = 1024
D_FF = 2816
D_INNER = 2048
SSM_HEAD_DIM = 64
SSM_HEADS = 32
SSM_GROUPS = 4
HEADS_PER_GROUP = SSM_HEADS // SSM_GROUPS
SSM_D_STATE = 128
GROUP_WIDTH = D_INNER // SSM_GROUPS
CONV_WIDTH = 4
CONV_DIM = D_INNER + 2 * SSM_GROUPS * SSM_D_STATE
SSD_CHUNK = 128
WINDOW = 128
ATT_HEAD_DIM = 64
N_Q_HEADS = 16
N_KV_HEADS = 4
Q_PER_KV = N_Q_HEADS // N_KV_HEADS
KV_WIDTH = N_KV_HEADS * ATT_HEAD_DIM
PAST_LEN = 8192
NORM_EPS = 1e-5
LOG2E = 1.0 / math.log(2.0)

LANES = 128
SUBLANES = 8
VMEM_LIMIT_BYTES = 56 * 1024 * 1024
FF_CHUNK = 256
ROW_TILE = 1024
PROJ_ROW_TILE = 512
PROJ_COL_CHUNK = 512
MIX_ROWS = 256
DECODE_SEQS = 4
SSD_DECODE_SEQS = 8

_NT = (((1,), (1,)), ((), ()))
_TN = (((0,), (0,)), ((), ()))


def _rms(x, g):
    ms = jnp.mean(x * x, axis=-1, keepdims=True)
    return x * lax.rsqrt(ms + NORM_EPS) * g


def _silu(x):
    return x * (1.0 / (1.0 + jnp.exp(-x)))


def _softplus(x):
    return jnp.maximum(x, 0.0) + jnp.log1p(jnp.exp(-jnp.abs(x)))


def _params(*sem):
    return pltpu.CompilerParams(dimension_semantics=sem, vmem_limit_bytes=VMEM_LIMIT_BYTES)


def _full(shape):
    return pl.BlockSpec(shape, lambda *_: (0,) * len(shape))


def _ffn_kernel(xa_ref, xb_ref, g_ref, wgu_ref, wd_ref, fg_ref, oa_ref, ob_ref, *, final_norm, na):
    def run(x_ref, o_ref):
        x = x_ref[...]
        h = _rms(x, g_ref[...]).astype(BF16)
        for j in range(D_FF // FF_CHUNK):
            lo = j * FF_CHUNK
            gate = jnp.dot(h, wgu_ref[:, lo:lo + FF_CHUNK], preferred_element_type=F32)
            up = jnp.dot(h, wgu_ref[:, D_FF + lo:D_FF + lo + FF_CHUNK], preferred_element_type=F32)
            act = (_silu(gate) * up).astype(BF16)
            part = jnp.dot(act, wd_ref[lo:lo + FF_CHUNK, :], preferred_element_type=F32)
            if j == 0:
                o_ref[...] = part
            else:
                o_ref[...] += part
        out = x + 0.5 * o_ref[...]
        if final_norm:
            out = _rms(out, fg_ref[...])
        o_ref[...] = out

    i = pl.program_id(0)

    @pl.when(i < na)
    def _():
        run(xa_ref, oa_ref)

    @pl.when(i >= na)
    def _():
        run(xb_ref, ob_ref)


def _ffn(xa, xb, g, wgu, wd, layer, fg=None):
    ta, tb = min(ROW_TILE, xa.shape[0]), min(PROJ_ROW_TILE, xb.shape[0])
    na, nb = xa.shape[0] // ta, xb.shape[0] // tb
    final_norm = fg is not None
    g = g[layer]
    if fg is None:
        fg = g

    def layer_weights(*dims):
        return pl.BlockSpec((None,) + dims, lambda i: (layer, 0, 0), pipeline_mode=pl.Buffered(1))

    spec_a = pl.BlockSpec((ta, D_MODEL), lambda i: (jnp.minimum(i, na - 1), 0))
    spec_b = pl.BlockSpec((tb, D_MODEL), lambda i: (jnp.maximum(i - na, 0), 0))
    return pl.pallas_call(
        functools.partial(_ffn_kernel, final_norm=final_norm, na=na),
        grid=(na + nb,),
        in_specs=[spec_a, spec_b,
                  _full((1, D_MODEL)), layer_weights(D_MODEL, 2 * D_FF), layer_weights(D_FF, D_MODEL),
                  _full((1, D_MODEL))],
        out_specs=[spec_a, spec_b],
        out_shape=[jax.ShapeDtypeStruct(xa.shape, F32), jax.ShapeDtypeStruct(xb.shape, F32)],
        compiler_params=_params("arbitrary"),
        name="ffn",
    )(xa, xb, g.reshape(1, D_MODEL), wgu, wd, fg.reshape(1, D_MODEL))


def _norm_proj_kernel(x_ref, g_ref, *refs):
    n = len(refs) // 2
    h = _rms(x_ref[...], g_ref[...]).astype(BF16)
    for w_ref, o_ref in zip(refs[:n], refs[n:]):
        width = w_ref.shape[1]
        step = min(PROJ_COL_CHUNK, width)
        for lo in range(0, width, step):
            o_ref[:, lo:lo + step] = jnp.dot(h, w_ref[:, lo:lo + step], preferred_element_type=F32)


def _norm_proj(x, g, weights):
    t = x.shape[0]
    tm = min(PROJ_ROW_TILE, t)
    return pl.pallas_call(
        _norm_proj_kernel,
        grid=(t // tm,),
        in_specs=[pl.BlockSpec((tm, D_MODEL), lambda i: (i, 0)), _full((1, D_MODEL))]
        + [_full(w.shape) for w in weights],
        out_specs=[pl.BlockSpec((tm, w.shape[1]), lambda i: (i, 0)) for w in weights],
        out_shape=[jax.ShapeDtypeStruct((t, w.shape[1]), F32) for w in weights],
        compiler_params=_params("parallel"),
        name="norm_proj",
    )(x, g.reshape(1, D_MODEL), *weights)


def _proj_res_kernel(y_ref, w_ref, x_ref, o_ref):
    o_ref[...] = x_ref[...] + jnp.dot(y_ref[...].astype(BF16), w_ref[...], preferred_element_type=F32)


def _proj_res(y, w, x):
    t, k = y.shape
    tm = min(PROJ_ROW_TILE, t)
    return pl.pallas_call(
        _proj_res_kernel,
        grid=(t // tm,),
        in_specs=[pl.BlockSpec((tm, k), lambda i: (i, 0)), _full(w.shape),
                  pl.BlockSpec((tm, D_MODEL), lambda i: (i, 0))],
        out_specs=pl.BlockSpec((tm, D_MODEL), lambda i: (i, 0)),
        out_shape=jax.ShapeDtypeStruct((t, D_MODEL), F32),
        compiler_params=_params("parallel"),
        name="proj_res",
    )(y, w, x)


def _split3(x):
    hi = x.astype(BF16)
    r1 = x - hi.astype(F32)
    mid = r1.astype(BF16)
    lo = (r1 - mid.astype(F32)).astype(BF16)
    return hi, mid, lo


def _conv_silu(ext_scr, n, cw_ref, cb_ref):
    ext = ext_scr[...]
    x0 = ext.reshape(n // SUBLANES + 1, SUBLANES, CONV_DIM)
    sub = lax.broadcasted_iota(jnp.int32, (1, SUBLANES, CONV_DIM), 1)

    def down(a, d):
        rot = pltpu.roll(a, d, 1)
        return jnp.where(sub < d, jnp.concatenate([rot[:1], rot[:-1]], axis=0), rot)

    w = [cw_ref[k:k + 1, :].reshape(1, 1, CONV_DIM) for k in range(CONV_WIDTH)]
    x1 = down(x0, 1)
    near = x0 * w[3] + x1 * w[2]
    far = down(x0 * w[1] + x1 * w[0], 2)
    xc = _silu(((near + far)[1:]).reshape(n, CONV_DIM) + cb_ref[...])
    return xc, ext[n:n + SUBLANES]


def _ssd_scalars(dtt, dtb_ref, alog_ref, lt, ls):
    dt_r = _softplus(dtt + dtb_ref[...])
    da_r = dt_r * -jnp.exp(alog_ref[...])
    upper = (lax.broadcasted_iota(jnp.int32, (ls, ls), 0)
             <= lax.broadcasted_iota(jnp.int32, (ls, ls), 1)).astype(F32).astype(BF16)
    parts = jnp.dot(jnp.concatenate(_split3(da_r), axis=0), upper, preferred_element_type=F32)
    acs_r = (parts[0:SSM_HEADS] + parts[SSM_HEADS:2 * SSM_HEADS]) + parts[2 * SSM_HEADS:]
    last_r = acs_r[:, lt - 1:lt]
    c2_r = acs_r * LOG2E
    r2_r = jnp.log(dt_r) * LOG2E - c2_r
    w_r = dt_r * jnp.exp(last_r - acs_r)
    e_r = jnp.exp(acs_r)
    chunk_decay = jnp.exp(jnp.broadcast_to(last_r, (SSM_HEADS, LANES)))
    col = jnp.concatenate([c2_r, w_r, e_r, jnp.zeros((LANES - 3 * SSM_HEADS, ls), F32)],
                          axis=0).T[0:lt]
    return col, r2_r, chunk_decay


def _decay_rows(col, r2_r, lt, ls):
    r2_c = jnp.concatenate([r2_r, jnp.zeros((LANES - SSM_HEADS, ls), F32)], axis=0).T[0:lt]
    t_idx = lax.broadcasted_iota(jnp.int32, (lt, LANES), 0)
    head_lane = lax.broadcasted_iota(jnp.int32, (lt, LANES), 1) < SSM_HEADS
    return jnp.concatenate(
        [jnp.exp2(jnp.where((t_idx >= s) & head_lane, col + r2_c[s:s + 1, :], -jnp.inf))
         for s in range(lt)], axis=0)


def _ssd_spread(col, spread_ref):
    lane = lax.broadcasted_iota(jnp.int32, col.shape, 1)
    used = (lane >= SSM_HEADS) & (lane < 3 * SSM_HEADS)
    hi, mid, lo = _split3(jnp.where(used, col, 0.0))
    first = (hi.astype(F32) + pltpu.roll(mid.astype(F32), 2 * SSM_HEADS, 1)).astype(BF16)
    return jnp.dot(jnp.concatenate([first, lo], axis=1), spread_ref[...], preferred_element_type=F32)


def _ssd_core(xc, zg, col, r2_r, chunk_decay, wide, state_scr, dskip_ref, gn_ref, pad_scr, lt, ls, emit,
              decay_x=None):
    def pad_rows(v, k):
        if ls == lt:
            return v
        pad_scr[k][...] = jnp.zeros(pad_scr[k].shape, F32)
        pad_scr[k][0:lt, :] = v
        return pad_scr[k][...]

    xs = xc[:, :D_INNER]
    bm = xc[:, D_INNER:D_INNER + GROUP_WIDTH]
    cm = xc[:, D_INNER + GROUP_WIDTH:]
    w_x = wide[:, :D_INNER]
    e_x = wide[:, D_INNER:]

    causal = (lax.broadcasted_iota(jnp.int32, (lt, ls), 1)
              <= lax.broadcasted_iota(jnp.int32, (lt, ls), 0))
    low_half = lax.broadcasted_iota(jnp.int32, (lt, LANES), 1) < SSM_HEAD_DIM
    bm_s = pad_rows(bm, 0)
    xd_s = pad_rows(xs * w_x, 1)
    assert decay_x is not None or ls == lt

    for g in range(SSM_GROUPS):
        cm_g = cm[:, g * SSM_D_STATE:(g + 1) * SSM_D_STATE].astype(BF16)
        bm_g = bm_s[:, g * SSM_D_STATE:(g + 1) * SSM_D_STATE].astype(BF16)
        cb_g = lax.dot_general(cm_g, bm_g, _NT, preferred_element_type=F32)
        rows = slice(g * GROUP_WIDTH, (g + 1) * GROUP_WIDTH)
        y_off = lax.dot_general(cm_g, state_scr[rows, :].astype(BF16), _NT,
                                preferred_element_type=F32)
        if decay_x is not None:
            y_diag_g = sum(jnp.broadcast_to(cb_g[:, s:s + 1], (lt, GROUP_WIDTH))
                           * decay_x[s * lt:(s + 1) * lt, rows] * xs[s:s + 1, rows] for s in range(lt))
        tiles = []
        for jj in range(HEADS_PER_GROUP // 2):
            tile = g * (HEADS_PER_GROUP // 2) + jj
            cols = slice(tile * LANES, (tile + 1) * LANES)
            if decay_x is not None:
                y_diag = y_diag_g[:, jj * LANES:(jj + 1) * LANES]
            else:
                mats = []
                for h in (2 * tile, 2 * tile + 1):
                    expo = col[:, h:h + 1] + r2_r[h:h + 1, :]
                    mats.append((cb_g * jnp.exp2(jnp.where(causal, expo, -jnp.inf))).astype(BF16))
                xp = xs[:, cols]
                x_pair = jnp.concatenate([jnp.where(low_half, xp, 0.0), jnp.where(low_half, 0.0, xp)],
                                         axis=0).astype(BF16)
                y_diag = jnp.dot(jnp.concatenate(mats, axis=1), x_pair, preferred_element_type=F32)
            tiles.append(y_diag + y_off[:, jj * LANES:(jj + 1) * LANES] * e_x[:, cols]
                         + dskip_ref[:, cols] * xs[:, cols])
        yg = jnp.concatenate(tiles, axis=1) * _silu(zg[:, rows])
        ms = jnp.mean(yg * yg, axis=-1, keepdims=True)
        emit(g, yg * lax.rsqrt(ms + NORM_EPS) * gn_ref[:, rows])

        upd = lax.dot_general(xd_s[:, rows].astype(BF16), bm_g, _TN, preferred_element_type=F32)
        for hh in range(HEADS_PER_GROUP):
            h = g * HEADS_PER_GROUP + hh
            hrows = slice(h * SSM_HEAD_DIM, (h + 1) * SSM_HEAD_DIM)
            state_scr[hrows, :] = (state_scr[hrows, :] * chunk_decay[h:h + 1, :]
                                   + upd[hh * SSM_HEAD_DIM:(hh + 1) * SSM_HEAD_DIM, :])


def _ssd_decode_kernel(xbc_ref, z_ref, dtt_ref, conv0_ref, ssm0_ref,
                       cw_ref, cb_ref, dtb_ref, alog_ref, dskip_ref, gn_ref, spread_ref, headmat_ref,
                       y_ref, ssm_ref, conv_ref, ext_scr, *pad_scr, lt, ls, nseq):
    staged = []
    for i in range(nseq):
        ext_i = ext_scr.at[i]
        ext_i[0:SUBLANES, :] = conv0_ref[i]
        ext_i[SUBLANES:SUBLANES + lt, :] = xbc_ref[i]
        xc, tail = _conv_silu(ext_i, lt, cw_ref, cb_ref)
        conv_ref[i] = tail
        ssm_ref[i] = ssm0_ref[i]
        staged.append((xc,) + _ssd_scalars(dtt_ref[i], dtb_ref, alog_ref, lt, ls))
    wide = _ssd_spread(jnp.concatenate([st[1] for st in staged], axis=0), spread_ref)
    decay_x = jnp.dot(
        jnp.concatenate([_decay_rows(st[1], st[2], lt, ls) for st in staged], axis=0).astype(BF16),
        headmat_ref[...], preferred_element_type=F32)
    for i, (xc, col, r2_r, chunk_decay) in enumerate(staged):

        def emit(g, y, i=i):
            y_ref[i, :, g * GROUP_WIDTH:(g + 1) * GROUP_WIDTH] = y

        _ssd_core(xc, z_ref[i], col, r2_r, chunk_decay, wide[i * lt:(i + 1) * lt], ssm_ref.at[i],
                  dskip_ref, gn_ref, [p.at[i] for p in pad_scr], lt, ls, emit,
                  decay_x[i * lt * lt:(i + 1) * lt * lt])


def _mixer0_kernel(x_ref, g_ref, wz_ref, wxbc_ref, wdtt_ref, wout_ref, conv0_ref, ssm0_ref,
                   cw_ref, cb_ref, dtb_ref, alog_ref, dskip_ref, gn_ref, spread_ref,
                   o_ref, ssm_ref, conv_ref,
                   ext_scr, state_scr, z_scr, y_scr, *, rb, nsteps):
    c = pl.program_id(1)

    @pl.when(c == 0)
    def _():
        ext_scr[0:SUBLANES, :] = conv0_ref[...]
        state_scr[...] = ssm0_ref[...]

    x = x_ref[...]
    h = _rms(x, g_ref[...]).astype(BF16)
    for lo in range(0, CONV_DIM, PROJ_COL_CHUNK):
        ext_scr[SUBLANES:SUBLANES + rb, lo:lo + PROJ_COL_CHUNK] = jnp.dot(
            h, wxbc_ref[:, lo:lo + PROJ_COL_CHUNK], preferred_element_type=F32)
    for lo in range(0, D_INNER, PROJ_COL_CHUNK):
        z_scr[:, lo:lo + PROJ_COL_CHUNK] = jnp.dot(
            h, wz_ref[:, lo:lo + PROJ_COL_CHUNK], preferred_element_type=F32)
    dtt = lax.dot_general(wdtt_ref[...], h, _NT, preferred_element_type=F32)

    xc, tail = _conv_silu(ext_scr, rb, cw_ref, cb_ref)
    conv_ref[...] = tail
    ext_scr[0:SUBLANES, :] = tail

    chunks = [slice(k * SSD_CHUNK, (k + 1) * SSD_CHUNK) for k in range(rb // SSD_CHUNK)]
    scalars = [_ssd_scalars(dtt[:, rows], dtb_ref, alog_ref, SSD_CHUNK, SSD_CHUNK) for rows in chunks]
    wide = _ssd_spread(jnp.concatenate([sc[0] for sc in scalars], axis=0), spread_ref)
    for rows, (col, r2_r, chunk_decay) in zip(chunks, scalars):

        def emit(g, y, rows=rows):
            y_scr[rows, g * GROUP_WIDTH:(g + 1) * GROUP_WIDTH] = y.astype(BF16)

        _ssd_core(xc[rows], z_scr[rows, :], col, r2_r, chunk_decay, wide[rows], state_scr,
                  dskip_ref, gn_ref, (), SSD_CHUNK, SSD_CHUNK, emit)

    o_ref[...] = x + jnp.dot(y_scr[...], wout_ref[...], preferred_element_type=F32)

    @pl.when(c == nsteps - 1)
    def _():
        ssm_ref[...] = state_scr[...]


def _spread_matrix():
    k = np.arange(2 * LANES)[:, None]
    head = np.arange(D_INNER)[None, :] // SSM_HEAD_DIM
    w_rows = (k == SSM_HEADS + head) | (k == 3 * SSM_HEADS + head) | (k == 5 * SSM_HEADS + head)
    e_rows = (k == 2 * SSM_HEADS + head) | (k == head) | (k == 6 * SSM_HEADS + head)
    return jnp.asarray(np.concatenate([w_rows, e_rows], axis=1), dtype=BF16)


def _head_matrix():
    k = np.arange(LANES)[:, None]
    head = np.arange(D_INNER)[None, :] // SSM_HEAD_DIM
    return jnp.asarray(k == head, dtype=BF16)


def _ssd_decode(xbc, z, dt, conv0, ssm0, p, bsz, seq):
    lt, ls, nseq = seq, LANES, SSD_DECODE_SEQS
    xbc = xbc.reshape(bsz, seq, CONV_DIM)
    z = z.reshape(bsz, seq, D_INNER)
    dtt = jnp.swapaxes(dt.reshape(bsz, seq, LANES)[:, :, :SSM_HEADS], 1, 2)
    dtt = jnp.pad(dtt, ((0, 0), (0, 0), (0, ls - lt)))
    conv0 = jnp.pad(conv0, ((0, 0), (SUBLANES - (CONV_WIDTH - 1), 0), (0, 0)))
    ssm0 = ssm0.reshape(bsz, D_INNER, SSM_D_STATE)

    def head_rows(v):
        return jnp.broadcast_to(v.astype(F32)[:, None], (SSM_HEADS, ls))

    def seqs(*dims):
        return pl.BlockSpec((nseq,) + dims, lambda b: (b,) + (0,) * len(dims))

    y, ssm, conv = pl.pallas_call(
        functools.partial(_ssd_decode_kernel, lt=lt, ls=ls, nseq=nseq),
        grid=(bsz // nseq,),
        in_specs=[seqs(lt, CONV_DIM), seqs(lt, D_INNER), seqs(SSM_HEADS, ls),
                  seqs(SUBLANES, CONV_DIM), seqs(D_INNER, SSM_D_STATE),
                  _full((CONV_WIDTH, CONV_DIM)), _full((1, CONV_DIM)),
                  _full((SSM_HEADS, ls)), _full((SSM_HEADS, ls)),
                  _full((1, D_INNER)), _full((1, D_INNER)),
                  _full((2 * LANES, 2 * D_INNER)), _full((LANES, D_INNER))],
        out_specs=[seqs(lt, D_INNER), seqs(D_INNER, SSM_D_STATE), seqs(SUBLANES, CONV_DIM)],
        out_shape=[jax.ShapeDtypeStruct((bsz, seq, D_INNER), F32),
                   jax.ShapeDtypeStruct((bsz, D_INNER, SSM_D_STATE), F32),
                   jax.ShapeDtypeStruct((bsz, SUBLANES, CONV_DIM), F32)],
        scratch_shapes=[pltpu.VMEM((nseq, SUBLANES + lt, CONV_DIM), F32),
                        pltpu.VMEM((nseq, ls, GROUP_WIDTH), F32), pltpu.VMEM((nseq, ls, D_INNER), F32)],
        compiler_params=_params("parallel"),
        name="ssd_decode",
    )(xbc, z, dtt, conv0, ssm0,
      p["conv_w"], p["conv_b"].reshape(1, CONV_DIM),
      head_rows(p["dt_bias"]), head_rows(p["a_log"]),
      jnp.repeat(p["d_skip"].astype(F32), SSM_HEAD_DIM).reshape(1, D_INNER),
      p["gate_norm"].reshape(1, D_INNER), _spread_matrix(), _head_matrix())
    return (y.reshape(bsz * seq, D_INNER),
            ssm.reshape(bsz, SSM_HEADS, SSM_HEAD_DIM, SSM_D_STATE),
            conv[:, SUBLANES - (CONV_WIDTH - 1):, :])


def _mixer0(x, conv0, ssm0, w, bsz, seq):
    rb = MIX_ROWS
    nsteps = seq // rb
    p = w["ssm"]
    conv0 = jnp.pad(conv0, ((0, 0), (SUBLANES - (CONV_WIDTH - 1), 0), (0, 0)))
    ssm0 = ssm0.reshape(bsz, D_INNER, SSM_D_STATE)

    def head_rows(v):
        return jnp.broadcast_to(v.astype(F32)[:, None], (SSM_HEADS, SSD_CHUNK))

    row_spec = pl.BlockSpec((rb, D_MODEL), lambda b, c: (b * nsteps + c, 0))
    out, ssm, conv = pl.pallas_call(
        functools.partial(_mixer0_kernel, rb=rb, nsteps=nsteps),
        grid=(bsz, nsteps),
        in_specs=[row_spec,
                  _full((1, D_MODEL)), _full((D_MODEL, D_INNER)), _full((D_MODEL, CONV_DIM)),
                  _full((SSM_HEADS, D_MODEL)), _full((D_INNER, D_MODEL)),
                  pl.BlockSpec((None, SUBLANES, CONV_DIM), lambda b, c: (b, 0, 0)),
                  pl.BlockSpec((None, D_INNER, SSM_D_STATE), lambda b, c: (b, 0, 0)),
                  _full((CONV_WIDTH, CONV_DIM)), _full((1, CONV_DIM)),
                  _full((SSM_HEADS, SSD_CHUNK)), _full((SSM_HEADS, SSD_CHUNK)),
                  _full((1, D_INNER)), _full((1, D_INNER)),
                  _full((2 * LANES, 2 * D_INNER))],
        out_specs=[row_spec,
                   pl.BlockSpec((None, D_INNER, SSM_D_STATE), lambda b, c: (b, 0, 0)),
                   pl.BlockSpec((None, SUBLANES, CONV_DIM), lambda b, c: (b, 0, 0))],
        out_shape=[jax.ShapeDtypeStruct((bsz * seq, D_MODEL), F32),
                   jax.ShapeDtypeStruct((bsz, D_INNER, SSM_D_STATE), F32),
                   jax.ShapeDtypeStruct((bsz, SUBLANES, CONV_DIM), F32)],
        scratch_shapes=[pltpu.VMEM((SUBLANES + rb, CONV_DIM), F32),
                        pltpu.VMEM((D_INNER, SSM_D_STATE), F32),
                        pltpu.VMEM((rb, D_INNER), F32),
                        pltpu.VMEM((rb, D_INNER), BF16)],
        compiler_params=_params("parallel", "arbitrary"),
        name="mixer0",
    )(x, w["mix_norm"][0].reshape(1, D_MODEL), w["ssm_w_z"], w["ssm_w_xbc"], w["ssm_w_dtt"],
      w["ssm_w_out"], conv0, ssm0,
      p["conv_w"], p["conv_b"].reshape(1, CONV_DIM),
      head_rows(p["dt_bias"]), head_rows(p["a_log"]),
      jnp.repeat(p["d_skip"].astype(F32), SSM_HEAD_DIM).reshape(1, D_INNER),
      p["gate_norm"].reshape(1, D_INNER), _spread_matrix())
    return (out, ssm.reshape(bsz, SSM_HEADS, SSM_HEAD_DIM, SSM_D_STATE),
            conv[:, SUBLANES - (CONV_WIDTH - 1):, :])


def _alibi_slope(head):
    return 2.0 ** (-8.0 * (head + 1) / N_Q_HEADS)


def _kv_heads_low(kv):
    low_half = lax.broadcasted_iota(jnp.int32, (kv.shape[0], LANES), 1) < ATT_HEAD_DIM
    out = []
    for j in range(N_KV_HEADS):
        tile = kv[:, (j // 2) * LANES:(j // 2 + 1) * LANES]
        if j % 2 == 0:
            out.append(jnp.where(low_half, tile, 0.0).astype(BF16))
        else:
            out.append(pltpu.roll(jnp.where(low_half, 0.0, tile), ATT_HEAD_DIM, 1).astype(BF16))
    return out


def _attn_block(q, k_heads, v_heads, sink_ref, first_key_pos, tq):
    rows = lax.broadcasted_iota(jnp.int32, (Q_PER_KV * tq, WINDOW), 0)
    slot = lax.broadcasted_iota(jnp.int32, (Q_PER_KV * tq, WINDOW), 1)
    t = rows & (tq - 1)
    sub = lax.shift_right_logical(rows, int(math.log2(tq)))
    from_prev = slot > t
    distf = jnp.where(from_prev, WINDOW + t - slot, t - slot).astype(F32)
    valid = jnp.logical_not(from_prev) | (first_key_pos >= 0)

    def pick(choices):
        return jnp.where(sub == 0, choices[0], jnp.where(sub == 1, choices[1],
                                                         jnp.where(sub == 2, choices[2], choices[3])))

    out = []
    for j in range(N_KV_HEADS):
        q0 = q[:, (2 * j) * LANES:(2 * j + 1) * LANES] * (ATT_HEAD_DIM ** -0.5)
        q1 = q[:, (2 * j + 1) * LANES:(2 * j + 2) * LANES] * (ATT_HEAD_DIM ** -0.5)
        qs = jnp.concatenate([q0, pltpu.roll(q0, ATT_HEAD_DIM, 1),
                              q1, pltpu.roll(q1, ATT_HEAD_DIM, 1)], axis=0)
        s2 = lax.dot_general(qs.astype(BF16), k_heads[j], _NT, preferred_element_type=F32)
        s = jnp.where(from_prev, s2[:, :WINDOW], s2[:, WINDOW:])
        slope = pick([_alibi_slope(Q_PER_KV * j + g) for g in range(Q_PER_KV)])
        sink = pick([sink_ref[Q_PER_KV * j + g] for g in range(Q_PER_KV)])[:, 0:1]
        s = jnp.where(valid, s - slope * distf, -jnp.inf)
        mx = jnp.maximum(jnp.max(s, axis=-1, keepdims=True), sink)
        e = jnp.exp(s - mx)
        den = jnp.sum(e, axis=-1, keepdims=True) + jnp.exp(sink - mx)
        prob = e * (1.0 / den)
        prob2 = jnp.concatenate([jnp.where(from_prev, prob, 0.0), jnp.where(from_prev, 0.0, prob)],
                                axis=1).astype(BF16)
        o = jnp.dot(prob2, v_heads[j], preferred_element_type=F32)
        out.append(o[0:tq] + pltpu.roll(o[tq:2 * tq], ATT_HEAD_DIM, 1))
        out.append(o[2 * tq:3 * tq] + pltpu.roll(o[3 * tq:4 * tq], ATT_HEAD_DIM, 1))
    return out


def _attn_decode_kernel(q_ref, kp_ref, kc_ref, vp_ref, vc_ref, bias_ref, sink_ref,
                        o_ref, kwin_ref, vwin_ref, k_scr, v_scr, s_scr, *, tq, nseq):
    k_scr[...] = jnp.zeros(k_scr.shape, F32)
    v_scr[...] = jnp.zeros(v_scr.shape, F32)
    low_half = lax.broadcasted_iota(jnp.int32, (tq, LANES), 1) < ATT_HEAD_DIM
    zeros = jnp.zeros((tq, LANES), F32)

    def half(tile, upper):
        return jnp.where(low_half, 0.0, tile) if upper else jnp.where(low_half, tile, 0.0)

    for i in range(nseq):
        k_scr[i, 0:WINDOW, :] = kp_ref[i]
        v_scr[i, 0:WINDOW, :] = vp_ref[i]
        k_scr[i, WINDOW:WINDOW + tq, :] = kc_ref[i]
        v_scr[i, WINDOW:WINDOW + tq, :] = vc_ref[i]
        kwin_ref[i] = k_scr[i, tq:tq + WINDOW, :]
        vwin_ref[i] = v_scr[i, tq:tq + WINDOW, :]
        q = q_ref[i] * (ATT_HEAD_DIM ** -0.5)
        blocks = []
        for h in range(N_Q_HEADS):
            j = h // Q_PER_KV
            piece = half(q[:, (h // 2) * LANES:(h // 2 + 1) * LANES], h % 2 == 1)
            if h % 2 != j % 2:
                piece = pltpu.roll(piece, ATT_HEAD_DIM, 1)
            blocks.append(jnp.concatenate([piece, zeros] if j < 2 else [zeros, piece], axis=1))
        q_all = jnp.concatenate(blocks, axis=0).astype(BF16)
        s_scr[i] = lax.dot_general(q_all, k_scr[i].astype(BF16), _NT,
                                   preferred_element_type=F32)
        s = s_scr[i] - bias_ref[...]
        sink = jnp.concatenate([jnp.full((tq, 1), sink_ref[h], F32) for h in range(N_Q_HEADS)], axis=0)
        mx = jnp.maximum(jnp.max(s, axis=-1, keepdims=True), sink)
        e = jnp.exp(s - mx)
        den = jnp.sum(e, axis=-1, keepdims=True) + jnp.exp(sink - mx)
        prob = (e * (1.0 / den)).astype(BF16)
        o_all = jnp.dot(prob, v_scr[i].astype(BF16), preferred_element_type=F32)
        for m in range(N_Q_HEADS // 2):
            parts = []
            for h in (2 * m, 2 * m + 1):
                j = h // Q_PER_KV
                piece = half(o_all[h * tq:(h + 1) * tq, (j // 2) * LANES:(j // 2 + 1) * LANES],
                             j % 2 == 1)
                if h % 2 != j % 2:
                    piece = pltpu.roll(piece, ATT_HEAD_DIM, 1)
                parts.append(piece)
            o_ref[i, :, m * LANES:(m + 1) * LANES] = parts[0] + parts[1]


def _mixer1_kernel(sink_ref, x_ref, xkv_ref, gq_ref, gkv_ref, wq_ref, wk_ref, wv_ref, wo_ref,
                   o_ref, kwin_ref, vwin_ref, k_scr, v_scr, att_scr, *, rb):
    c = pl.program_id(1)

    @pl.when(c == 0)
    def _():
        k_scr[0:WINDOW, :] = jnp.zeros((WINDOW, KV_WIDTH), F32)
        v_scr[0:WINDOW, :] = jnp.zeros((WINDOW, KV_WIDTH), F32)

    hkv = _rms(xkv_ref[...], gkv_ref[...]).astype(BF16)
    k_scr[WINDOW:WINDOW + rb, :] = jnp.dot(hkv, wk_ref[...], preferred_element_type=F32)
    v_scr[WINDOW:WINDOW + rb, :] = jnp.dot(hkv, wv_ref[...], preferred_element_type=F32)
    x = x_ref[...]
    q = jnp.dot(_rms(x, gq_ref[...]).astype(BF16), wq_ref[...], preferred_element_type=F32)
    k_heads = _kv_heads_low(k_scr[...])
    v_heads = _kv_heads_low(v_scr[...])
    for blk in range(rb // WINDOW):
        lo = blk * WINDOW
        tiles = _attn_block(q[lo:lo + WINDOW], [k[lo:lo + 2 * WINDOW] for k in k_heads],
                            [v[lo:lo + 2 * WINDOW] for v in v_heads], sink_ref,
                            c * rb + lo - WINDOW, WINDOW)
        for m, tile in enumerate(tiles):
            att_scr[lo:lo + WINDOW, m * LANES:(m + 1) * LANES] = tile.astype(BF16)
    o_ref[...] = x + jnp.dot(att_scr[...], wo_ref[...], preferred_element_type=F32)
    k_last = k_scr[rb:rb + WINDOW, :]
    v_last = v_scr[rb:rb + WINDOW, :]
    kwin_ref[...] = k_last
    vwin_ref[...] = v_last
    k_scr[0:WINDOW, :] = k_last
    v_scr[0:WINDOW, :] = v_last


def _mixer1(x, xkv, w, bsz, seq):
    rb = MIX_ROWS
    nsteps = seq // rb
    row_spec = pl.BlockSpec((rb, D_MODEL), lambda b, c: (b * nsteps + c, 0))
    win_spec = pl.BlockSpec((None, WINDOW, KV_WIDTH), lambda b, c: (b, 0, 0))
    return pl.pallas_call(
        functools.partial(_mixer1_kernel, rb=rb),
        grid=(bsz, nsteps),
        in_specs=[pl.BlockSpec(memory_space=pltpu.SMEM), row_spec, row_spec,
                  _full((1, D_MODEL)), _full((1, D_MODEL)),
                  _full((D_MODEL, D_MODEL)), _full((D_MODEL, KV_WIDTH)), _full((D_MODEL, KV_WIDTH)),
                  _full((D_MODEL, D_MODEL))],
        out_specs=[row_spec, win_spec, win_spec],
        out_shape=[jax.ShapeDtypeStruct((bsz * seq, D_MODEL), F32),
                   jax.ShapeDtypeStruct((bsz, WINDOW, KV_WIDTH), F32),
                   jax.ShapeDtypeStruct((bsz, WINDOW, KV_WIDTH), F32)],
        scratch_shapes=[pltpu.VMEM((WINDOW + rb, KV_WIDTH), F32), pltpu.VMEM((WINDOW + rb, KV_WIDTH), F32),
                        pltpu.VMEM((rb, D_MODEL), BF16)],
        compiler_params=_params("parallel", "arbitrary"),
        name="mixer1",
    )(w["attn_sinks"].astype(F32), x, xkv, w["mix_norm"][1].reshape(1, D_MODEL),
      w["kv_norm"].reshape(1, D_MODEL), w["attn_w_q"], w["w_k"], w["w_v"], w["attn_w_o"])


def _attention_decode(q, k_cache, k_new, v_cache, v_new, sinks):
    bsz, seq = q.shape[0], q.shape[1]
    nseq = DECODE_SEQS

    def seqs(*dims):
        return pl.BlockSpec((nseq,) + dims, lambda b: (b,) + (0,) * len(dims))

    t = np.tile(np.arange(seq), N_Q_HEADS)[:, None]
    kj = np.arange(2 * WINDOW)[None, :]
    dist = WINDOW + t - kj
    valid = (dist >= 0) & (dist < WINDOW) & (PAST_LEN - WINDOW + kj >= 0)
    slope = np.repeat([_alibi_slope(h) for h in range(N_Q_HEADS)], seq)[:, None]
    bias = jnp.asarray(np.where(valid, slope * dist, np.inf), dtype=F32)

    return pl.pallas_call(
        functools.partial(_attn_decode_kernel, tq=seq, nseq=nseq),
        grid=(bsz // nseq,),
        in_specs=[seqs(seq, D_MODEL),
                  seqs(WINDOW, KV_WIDTH), seqs(seq, KV_WIDTH), seqs(WINDOW, KV_WIDTH), seqs(seq, KV_WIDTH),
                  _full((N_Q_HEADS * seq, 2 * WINDOW)), pl.BlockSpec(memory_space=pltpu.SMEM)],
        out_specs=[seqs(seq, D_MODEL), seqs(WINDOW, KV_WIDTH), seqs(WINDOW, KV_WIDTH)],
        out_shape=[jax.ShapeDtypeStruct((bsz, seq, D_MODEL), F32),
                   jax.ShapeDtypeStruct((bsz, WINDOW, KV_WIDTH), F32),
                   jax.ShapeDtypeStruct((bsz, WINDOW, KV_WIDTH), F32)],
        scratch_shapes=[pltpu.VMEM((nseq, 2 * WINDOW, KV_WIDTH), F32),
                        pltpu.VMEM((nseq, 2 * WINDOW, KV_WIDTH), F32),
                        pltpu.VMEM((nseq, N_Q_HEADS * seq, 2 * WINDOW), F32)],
        compiler_params=_params("parallel"),
        name="swa_decode",
    )(q, k_cache, k_new, v_cache, v_new, bias, sinks.astype(F32))


def _trunk(xp, xs, ssm_p, conv_p, ssm_s, conv_s, k_buf, v_buf, w):
    bp, lp = xp.shape[0], xp.shape[1]
    bs, ls = xs.shape[0], xs.shape[1]
    xp = xp.reshape(bp * lp, D_MODEL)
    xs = xs.reshape(bs * ls, D_MODEL)

    xp, xs = _ffn(xp, xs, w["ffn1_norm"], w["ffn1_w_gu"], w["ffn1_w_down"], 0)
    xp, ssm_p, conv_p = _mixer0(xp, conv_p, ssm_p, w, bp, lp)
    z, xbc, dt = _norm_proj(xs, w["mix_norm"][0], [w["ssm_w_z"], w["ssm_w_xbc"], w["ssm_w_dt"]])
    y, ssm_s, conv_s = _ssd_decode(xbc, z, dt, conv_s, ssm_s, w["ssm"], bs, ls)
    xs = _proj_res(y, w["ssm_w_out"], xs)
    xp_kv, xs_kv = _ffn(xp, xs, w["ffn2_norm"], w["ffn2_w_gu"], w["ffn2_w_down"], 0)

    xp, xs = _ffn(xp_kv, xs_kv, w["ffn1_norm"], w["ffn1_w_gu"], w["ffn1_w_down"], 1)
    xp, kw_p, vw_p = _mixer1(xp, xp_kv, w, bp, lp)
    k_new, v_new = _norm_proj(xs_kv, w["kv_norm"], [w["w_k"], w["w_v"]])
    (q,) = _norm_proj(xs, w["mix_norm"][1], [w["attn_w_q"]])
    o, kw_s, vw_s = _attention_decode(
        q.reshape(bs, ls, D_MODEL), k_buf.reshape(bs, WINDOW, KV_WIDTH), k_new.reshape(bs, ls, KV_WIDTH),
        v_buf.reshape(bs, WINDOW, KV_WIDTH), v_new.reshape(bs, ls, KV_WIDTH), w["attn_sinks"])
    xs = _proj_res(o.reshape(bs * ls, D_MODEL), w["attn_w_o"], xs)
    yp, ys = _ffn(xp, xs, w["ffn2_norm"], w["ffn2_w_gu"], w["ffn2_w_down"], 1, fg=w["final_norm"])

    def heads(t, bsz):
        return t.reshape(bsz, WINDOW, N_KV_HEADS, ATT_HEAD_DIM)

    return (yp.reshape(bp, lp, D_MODEL), ys.reshape(bs, ls, D_MODEL), ssm_p[None], conv_p[None],
            heads(kw_p, bp), heads(vw_p, bp), ssm_s[None], conv_s[None], heads(kw_s, bs), heads(vw_s, bs))


def kernel(x_prompt, x_sample, state_ssm, state_conv, cache_k_win, cache_v_win,
           ffn1_norm, ffn1_w_gu, ffn1_w_down, mix_norm, ffn2_norm, ffn2_w_gu, ffn2_w_down,
           ssm_w_in, ssm_conv_w, ssm_conv_b, ssm_dt_bias, ssm_a_log, ssm_d, ssm_gate_norm, ssm_w_out,
           kv_norm, w_kv, attn_w_q, attn_sinks, attn_w_o, final_norm):
    w_in = ssm_w_in[0]
    w_dt = jnp.pad(w_in[:, D_INNER + CONV_DIM:], ((0, 0), (0, LANES - SSM_HEADS)))
    w = dict(
        ffn1_norm=ffn1_norm, ffn2_norm=ffn2_norm, mix_norm=mix_norm, kv_norm=kv_norm,
        final_norm=final_norm, attn_sinks=attn_sinks[0],
        ffn1_w_gu=ffn1_w_gu.astype(BF16), ffn1_w_down=ffn1_w_down.astype(BF16),
        ffn2_w_gu=ffn2_w_gu.astype(BF16), ffn2_w_down=ffn2_w_down.astype(BF16),
        ssm_w_z=w_in[:, :D_INNER].astype(BF16),
        ssm_w_xbc=w_in[:, D_INNER:D_INNER + CONV_DIM].astype(BF16),
        ssm_w_dt=w_dt.astype(BF16),
        ssm_w_dtt=w_in[:, D_INNER + CONV_DIM:].T.astype(BF16),
        ssm_w_out=ssm_w_out[0].astype(BF16),
        w_k=w_kv[:, :KV_WIDTH].astype(BF16), w_v=w_kv[:, KV_WIDTH:].astype(BF16),
        attn_w_q=attn_w_q[0].astype(BF16), attn_w_o=attn_w_o[0].astype(BF16),
        ssm=dict(conv_w=ssm_conv_w[0], conv_b=ssm_conv_b[0], dt_bias=ssm_dt_bias[0],
                 a_log=ssm_a_log[0], d_skip=ssm_d[0], gate_norm=ssm_gate_norm[0]),
    )
    bp = x_prompt.shape[0]
    ssm0 = jnp.zeros((bp, SSM_HEADS, SSM_HEAD_DIM, SSM_D_STATE), F32)
    conv0 = jnp.zeros((bp, CONV_WIDTH - 1, CONV_DIM), F32)
    return _trunk(x_prompt, x_sample, ssm0, conv0, state_ssm[0], state_conv[0],
                  cache_k_win, cache_v_win, w)
```

```python
import functools
import math

import jax
import jax.numpy as jnp
import numpy as np
from jax import lax
from jax.experimental import pallas as pl
from jax.experimental.pallas import tpu as pltpu

F32 = jnp.float32
BF16 = jnp.bfloat16

D_MODEL = 1024
D_FF = 2816
D_INNER = 2048
SSM_HEAD_DIM = 64
SSM_HEADS = 32
SSM_GROUPS = 4
HEADS_PER_GROUP = SSM_HEADS // SSM_GROUPS
SSM_D_STATE = 128
GROUP_WIDTH = D_INNER // SSM_GROUPS
CONV_WIDTH = 4
CONV_DIM = D_INNER + 2 * SSM_GROUPS * SSM_D_STATE
SSD_CHUNK = 128
WINDOW = 128
ATT_HEAD_DIM = 64
N_Q_HEADS = 16
N_KV_HEADS = 4
Q_PER_KV = N_Q_HEADS // N_KV_HEADS
KV_WIDTH = N_KV_HEADS * ATT_HEAD_DIM
PAST_LEN = 8192
NORM_EPS = 1e-5
LOG2E = 1.0 / math.log(2.0)

LANES = 128
SUBLANES = 8
VMEM_LIMIT_BYTES = 56 * 1024 * 1024
FF_CHUNK = 256
ROW_TILE = 1024
PROJ_ROW_TILE = 512
PROJ_COL_CHUNK = 512
MIX_ROWS = 256
DECODE_SEQS = 4
SSD_DECODE_SEQS = 8

_NT = (((1,), (1,)), ((), ()))
_TN = (((0,), (0,)), ((), ()))


def _rms(x, g):
    ms = jnp.mean(x * x, axis=-1, keepdims=True)
    return x * lax.rsqrt(ms + NORM_EPS) * g


def _silu(x):
    return x * (1.0 / (1.0 + jnp.exp(-x)))


def _softplus(x):
    return jnp.maximum(x, 0.0) + jnp.log1p(jnp.exp(-jnp.abs(x)))


def _params(*sem):
    return pltpu.CompilerParams(dimension_semantics=sem, vmem_limit_bytes=VMEM_LIMIT_BYTES)


def _full(shape):
    return pl.BlockSpec(shape, lambda *_: (0,) * len(shape))


def _ffn_kernel(xa_ref, xb_ref, g_ref, wgu_ref, wd_ref, fg_ref, oa_ref, ob_ref, *, final_norm, na):
    def run(x_ref, o_ref):
        x = x_ref[...]
        h = _rms(x, g_ref[...]).astype(BF16)
        for j in range(D_FF // FF_CHUNK):
            lo = j * FF_CHUNK
            gate = jnp.dot(h, wgu_ref[:, lo:lo + FF_CHUNK], preferred_element_type=F32)
            up = jnp.dot(h, wgu_ref[:, D_FF + lo:D_FF + lo + FF_CHUNK], preferred_element_type=F32)
            act = (_silu(gate) * up).astype(BF16)
            part = jnp.dot(act, wd_ref[lo:lo + FF_CHUNK, :], preferred_element_type=F32)
            if j == 0:
                o_ref[...] = part
            else:
                o_ref[...] += part
        out = x + 0.5 * o_ref[...]
        if final_norm:
            out = _rms(out, fg_ref[...])
        o_ref[...] = out

    i = pl.program_id(0)

    @pl.when(i < na)
    def _():
        run(xa_ref, oa_ref)

    @pl.when(i >= na)
    def _():
        run(xb_ref, ob_ref)


def _ffn(xa, xb, g, wgu, wd, layer, fg=None):
    ta, tb = min(ROW_TILE, xa.shape[0]), min(PROJ_ROW_TILE, xb.shape[0])
    na, nb = xa.shape[0] // ta, xb.shape[0] // tb
    final_norm = fg is not None
    g = g[layer]
    if fg is None:
        fg = g

    def layer_weights(*dims):
        return pl.BlockSpec((None,) + dims, lambda i: (layer, 0, 0), pipeline_mode=pl.Buffered(1))

    spec_a = pl.BlockSpec((ta, D_MODEL), lambda i: (jnp.minimum(i, na - 1), 0))
    spec_b = pl.BlockSpec((tb, D_MODEL), lambda i: (jnp.maximum(i - na, 0), 0))
    return pl.pallas_call(
        functools.partial(_ffn_kernel, final_norm=final_norm, na=na),
        grid=(na + nb,),
        in_specs=[spec_a, spec_b,
                  _full((1, D_MODEL)), layer_weights(D_MODEL, 2 * D_FF), layer_weights(D_FF, D_MODEL),
                  _full((1, D_MODEL))],
        out_specs=[spec_a, spec_b],
        out_shape=[jax.ShapeDtypeStruct(xa.shape, F32), jax.ShapeDtypeStruct(xb.shape, F32)],
        compiler_params=_params("arbitrary"),
        name="ffn",
    )(xa, xb, g.reshape(1, D_MODEL), wgu, wd, fg.reshape(1, D_MODEL))


def _norm_proj_kernel(x_ref, g_ref, *refs):
    n = len(refs) // 2
    h = _rms(x_ref[...], g_ref[...]).astype(BF16)
    for w_ref, o_ref in zip(refs[:n], refs[n:]):
        width = w_ref.shape[1]
        step = min(PROJ_COL_CHUNK, width)
        for lo in range(0, width, step):
            o_ref[:, lo:lo + step] = jnp.dot(h, w_ref[:, lo:lo + step], preferred_element_type=F32)


def _norm_proj(x, g, weights):
    t = x.shape[0]
    tm = min(PROJ_ROW_TILE, t)
    return pl.pallas_call(
        _norm_proj_kernel,
        grid=(t // tm,),
        in_specs=[pl.BlockSpec((tm, D_MODEL), lambda i: (i, 0)), _full((1, D_MODEL))]
        + [_full(w.shape) for w in weights],
        out_specs=[pl.BlockSpec((tm, w.shape[1]), lambda i: (i, 0)) for w in weights],
        out_shape=[jax.ShapeDtypeStruct((t, w.shape[1]), F32) for w in weights],
        compiler_params=_params("parallel"),
        name="norm_proj",
    )(x, g.reshape(1, D_MODEL), *weights)


def _proj_res_kernel(y_ref, w_ref, x_ref, o_ref):
    o_ref[...] = x_ref[...] + jnp.dot(y_ref[...].astype(BF16), w_ref[...], preferred_element_type=F32)


def _proj_res(y, w, x):
    t, k = y.shape
    tm = min(PROJ_ROW_TILE, t)
    return pl.pallas_call(
        _proj_res_kernel,
        grid=(t // tm,),
        in_specs=[pl.BlockSpec((tm, k), lambda i: (i, 0)), _full(w.shape),
                  pl.BlockSpec((tm, D_MODEL), lambda i: (i, 0))],
        out_specs=pl.BlockSpec((tm, D_MODEL), lambda i: (i, 0)),
        out_shape=jax.ShapeDtypeStruct((t, D_MODEL), F32),
        compiler_params=_params("parallel"),
        name="proj_res",
    )(y, w, x)


def _split3(x):
    hi = x.astype(BF16)
    r1 = x - hi.astype(F32)
    mid = r1.astype(BF16)
    lo = (r1 - mid.astype(F32)).astype(BF16)
    return hi, mid, lo


def _conv_silu(ext_scr, n, cw_ref, cb_ref):
    ext = ext_scr[...]
    x0 = ext.reshape(n // SUBLANES + 1, SUBLANES, CONV_DIM)
    sub = lax.broadcasted_iota(jnp.int32, (1, SUBLANES, CONV_DIM), 1)

    def down(a, d):
        rot = pltpu.roll(a, d, 1)
        return jnp.where(sub < d, jnp.concatenate([rot[:1], rot[:-1]], axis=0), rot)

    w = [cw_ref[k:k + 1, :].reshape(1, 1, CONV_DIM) for k in range(CONV_WIDTH)]
    x1 = down(x0, 1)
    near = x0 * w[3] + x1 * w[2]
    far = down(x0 * w[1] + x1 * w[0], 2)
    xc = _silu(((near + far)[1:]).reshape(n, CONV_DIM) + cb_ref[...])
    return xc, ext[n:n + SUBLANES]


def _ssd_scalars(dtt, dtb_ref, alog_ref, lt, ls):
    dt_r = _softplus(dtt + dtb_ref[...])
    da_r = dt_r * -jnp.exp(alog_ref[...])
    upper = (lax.broadcasted_iota(jnp.int32, (ls, ls), 0)
             <= lax.broadcasted_iota(jnp.int32, (ls, ls), 1)).astype(F32).astype(BF16)
    parts = jnp.dot(jnp.concatenate(_split3(da_r), axis=0), upper, preferred_element_type=F32)
    acs_r = (parts[0:SSM_HEADS] + parts[SSM_HEADS:2 * SSM_HEADS]) + parts[2 * SSM_HEADS:]
    last_r = acs_r[:, lt - 1:lt]
    c2_r = acs_r * LOG2E
    r2_r = jnp.log(dt_r) * LOG2E - c2_r
    w_r = dt_r * jnp.exp(last_r - acs_r)
    e_r = jnp.exp(acs_r)
    chunk_decay = jnp.exp(jnp.broadcast_to(last_r, (SSM_HEADS, LANES)))
    col = jnp.concatenate([c2_r, w_r, e_r, jnp.zeros((LANES - 3 * SSM_HEADS, ls), F32)],
                          axis=0).T[0:lt]
    return col, r2_r, chunk_decay


def _decay_rows(col, r2_r, lt, ls):
    r2_c = jnp.concatenate([r2_r, jnp.zeros((LANES - SSM_HEADS, ls), F32)], axis=0).T[0:lt]
    t_idx = lax.broadcasted_iota(jnp.int32, (lt, LANES), 0)
    head_lane = lax.broadcasted_iota(jnp.int32, (lt, LANES), 1) < SSM_HEADS
    return jnp.concatenate(
        [jnp.exp2(jnp.where((t_idx >= s) & head_lane, col + r2_c[s:s + 1, :], -jnp.inf))
         for s in range(lt)], axis=0)


def _ssd_spread(col, spread_ref):
    lane = lax.broadcasted_iota(jnp.int32, col.shape, 1)
    used = (lane >= SSM_HEADS) & (lane < 3 * SSM_HEADS)
    hi, mid, lo = _split3(jnp.where(used, col, 0.0))
    first = (hi.astype(F32) + pltpu.roll(mid.astype(F32), 2 * SSM_HEADS, 1)).astype(BF16)
    return jnp.dot(jnp.concatenate([first, lo], axis=1), spread_ref[...], preferred_element_type=F32)


def _ssd_core(xc, zg, col, r2_r, chunk_decay, wide, state_scr, dskip_ref, gn_ref, pad_scr, lt, ls, emit,
              decay_x=None):
    def pad_rows(v, k):
        if ls == lt:
            return v
        pad_scr[k][...] = jnp.zeros(pad_scr[k].shape, F32)
        pad_scr[k][0:lt, :] = v
        return pad_scr[k][...]

    xs = xc[:, :D_INNER]
    bm = xc[:, D_INNER:D_INNER + GROUP_WIDTH]
    cm = xc[:, D_INNER + GROUP_WIDTH:]
    w_x = wide[:, :D_INNER]
    e_x = wide[:, D_INNER:]

    causal = (lax.broadcasted_iota(jnp.int32, (lt, ls), 1)
              <= lax.broadcasted_iota(jnp.int32, (lt, ls), 0))
    low_half = lax.broadcasted_iota(jnp.int32, (lt, LANES), 1) < SSM_HEAD_DIM
    bm_s = pad_rows(bm, 0)
    xd_s = pad_rows(xs * w_x, 1)
    assert decay_x is not None or ls == lt

    for g in range(SSM_GROUPS):
        cm_g = cm[:, g * SSM_D_STATE:(g + 1) * SSM_D_STATE].astype(BF16)
        bm_g = bm_s[:, g * SSM_D_STATE:(g + 1) * SSM_D_STATE].astype(BF16)
        cb_g = lax.dot_general(cm_g, bm_g, _NT, preferred_element_type=F32)
        rows = slice(g * GROUP_WIDTH, (g + 1) * GROUP_WIDTH)
        y_off = lax.dot_general(cm_g, state_scr[rows, :].astype(BF16), _NT,
                                preferred_element_type=F32)
        if decay_x is not None:
            y_diag_g = sum(jnp.broadcast_to(cb_g[:, s:s + 1], (lt, GROUP_WIDTH))
                           * decay_x[s * lt:(s + 1) * lt, rows] * xs[s:s + 1, rows] for s in range(lt))
        tiles = []
        for jj in range(HEADS_PER_GROUP // 2):
            tile = g * (HEADS_PER_GROUP // 2) + jj
            cols = slice(tile * LANES, (tile + 1) * LANES)
            if decay_x is not None:
                y_diag = y_diag_g[:, jj * LANES:(jj + 1) * LANES]
            else:
                mats = []
                for h in (2 * tile, 2 * tile + 1):
                    expo = col[:, h:h + 1] + r2_r[h:h + 1, :]
                    mats.append((cb_g * jnp.exp2(jnp.where(causal, expo, -jnp.inf))).astype(BF16))
                xp = xs[:, cols]
                x_pair = jnp.concatenate([jnp.where(low_half, xp, 0.0), jnp.where(low_half, 0.0, xp)],
                                         axis=0).astype(BF16)
                y_diag = jnp.dot(jnp.concatenate(mats, axis=1), x_pair, preferred_element_type=F32)
            tiles.append(y_diag + y_off[:, jj * LANES:(jj + 1) * LANES] * e_x[:, cols]
                         + dskip_ref[:, cols] * xs[:, cols])
        yg = jnp.concatenate(tiles, axis=1) * _silu(zg[:, rows])
        ms = jnp.mean(yg * yg, axis=-1, keepdims=True)
        emit(g, yg * lax.rsqrt(ms + NORM_EPS) * gn_ref[:, rows])

        upd = lax.dot_general(xd_s[:, rows].astype(BF16), bm_g, _TN, preferred_element_type=F32)
        for hh in range(HEADS_PER_GROUP):
            h = g * HEADS_PER_GROUP + hh
            hrows = slice(h * SSM_HEAD_DIM, (h + 1) * SSM_HEAD_DIM)
            state_scr[hrows, :] = (state_scr[hrows, :] * chunk_decay[h:h + 1, :]
                                   + upd[hh * SSM_HEAD_DIM:(hh + 1) * SSM_HEAD_DIM, :])


def _ssd_decode_kernel(xbc_ref, z_ref, dtt_ref, conv0_ref, ssm0_ref,
                       cw_ref, cb_ref, dtb_ref, alog_ref, dskip_ref, gn_ref, spread_ref, headmat_ref,
                       y_ref, ssm_ref, conv_ref, ext_scr, *pad_scr, lt, ls, nseq):
    staged = []
    for i in range(nseq):
        ext_i = ext_scr.at[i]
        ext_i[0:SUBLANES, :] = conv0_ref[i]
        ext_i[SUBLANES:SUBLANES + lt, :] = xbc_ref[i]
        xc, tail = _conv_silu(ext_i, lt, cw_ref, cb_ref)
        conv_ref[i] = tail
        ssm_ref[i] = ssm0_ref[i]
        staged.append((xc,) + _ssd_scalars(dtt_ref[i], dtb_ref, alog_ref, lt, ls))
    wide = _ssd_spread(jnp.concatenate([st[1] for st in staged], axis=0), spread_ref)
    decay_x = jnp.dot(
        jnp.concatenate([_decay_rows(st[1], st[2], lt, ls) for st in staged], axis=0).astype(BF16),
        headmat_ref[...], preferred_element_type=F32)
    for i, (xc, col, r2_r, chunk_decay) in enumerate(staged):

        def emit(g, y, i=i):
            y_ref[i, :, g * GROUP_WIDTH:(g + 1) * GROUP_WIDTH] = y

        _ssd_core(xc, z_ref[i], col, r2_r, chunk_decay, wide[i * lt:(i + 1) * lt], ssm_ref.at[i],
                  dskip_ref, gn_ref, [p.at[i] for p in pad_scr], lt, ls, emit,
                  decay_x[i * lt * lt:(i + 1) * lt * lt])


def _mixer0_kernel(x_ref, g_ref, wz_ref, wxbc_ref, wdtt_ref, wout_ref, conv0_ref, ssm0_ref,
                   cw_ref, cb_ref, dtb_ref, alog_ref, dskip_ref, gn_ref, spread_ref,
                   o_ref, ssm_ref, conv_ref,
                   ext_scr, state_scr, z_scr, y_scr, *, rb, nsteps):
    c = pl.program_id(1)

    @pl.when(c == 0)
    def _():
        ext_scr[0:SUBLANES, :] = conv0_ref[...]
        state_scr[...] = ssm0_ref[...]

    x = x_ref[...]
    h = _rms(x, g_ref[...]).astype(BF16)
    for lo in range(0, CONV_DIM, PROJ_COL_CHUNK):
        ext_scr[SUBLANES:SUBLANES + rb, lo:lo + PROJ_COL_CHUNK] = jnp.dot(
            h, wxbc_ref[:, lo:lo + PROJ_COL_CHUNK], preferred_element_type=F32)
    for lo in range(0, D_INNER, PROJ_COL_CHUNK):
        z_scr[:, lo:lo + PROJ_COL_CHUNK] = jnp.dot(
            h, wz_ref[:, lo:lo + PROJ_COL_CHUNK], preferred_element_type=F32)
    dtt = lax.dot_general(wdtt_ref[...], h, _NT, preferred_element_type=F32)

    xc, tail = _conv_silu(ext_scr, rb, cw_ref, cb_ref)
    conv_ref[...] = tail
    ext_scr[0:SUBLANES, :] = tail

    chunks = [slice(k * SSD_CHUNK, (k + 1) * SSD_CHUNK) for k in range(rb // SSD_CHUNK)]
    scalars = [_ssd_scalars(dtt[:, rows], dtb_ref, alog_ref, SSD_CHUNK, SSD_CHUNK) for rows in chunks]
    wide = _ssd_spread(jnp.concatenate([sc[0] for sc in scalars], axis=0), spread_ref)
    for rows, (col, r2_r, chunk_decay) in zip(chunks, scalars):

        def emit(g, y, rows=rows):
            y_scr[rows, g * GROUP_WIDTH:(g + 1) * GROUP_WIDTH] = y.astype(BF16)

        _ssd_core(xc[rows], z_scr[rows, :], col, r2_r, chunk_decay, wide[rows], state_scr,
                  dskip_ref, gn_ref, (), SSD_CHUNK, SSD_CHUNK, emit)

    o_ref[...] = x + jnp.dot(y_scr[...], wout_ref[...], preferred_element_type=F32)

    @pl.when(c == nsteps - 1)
    def _():
        ssm_ref[...] = state_scr[...]


def _spread_matrix():
    k = np.arange(2 * LANES)[:, None]
    head = np.arange(D_INNER)[None, :] // SSM_HEAD_DIM
    w_rows = (k == SSM_HEADS + head) | (k == 3 * SSM_HEADS + head) | (k == 5 * SSM_HEADS + head)
    e_rows = (k == 2 * SSM_HEADS + head) | (k == head) | (k == 6 * SSM_HEADS + head)
    return jnp.asarray(np.concatenate([w_rows, e_rows], axis=1), dtype=BF16)


def _head_matrix():
    k = np.arange(LANES)[:, None]
    head = np.arange(D_INNER)[None, :] // SSM_HEAD_DIM
    return jnp.asarray(k == head, dtype=BF16)


def _ssd_decode(xbc, z, dt, conv0, ssm0, p, bsz, seq):
    lt, ls, nseq = seq, LANES, SSD_DECODE_SEQS
    xbc = xbc.reshape(bsz, seq, CONV_DIM)
    z = z.reshape(bsz, seq, D_INNER)
    dtt = jnp.swapaxes(dt.reshape(bsz, seq, LANES)[:, :, :SSM_HEADS], 1, 2)
    dtt = jnp.pad(dtt, ((0, 0), (0, 0), (0, ls - lt)))
    conv0 = jnp.pad(conv0, ((0, 0), (SUBLANES - (CONV_WIDTH - 1), 0), (0, 0)))
    ssm0 = ssm0.reshape(bsz, D_INNER, SSM_D_STATE)

    def head_rows(v):
        return jnp.broadcast_to(v.astype(F32)[:, None], (SSM_HEADS, ls))

    def seqs(*dims):
        return pl.BlockSpec((nseq,) + dims, lambda b: (b,) + (0,) * len(dims))

    y, ssm, conv = pl.pallas_call(
        functools.partial(_ssd_decode_kernel, lt=lt, ls=ls, nseq=nseq),
        grid=(bsz // nseq,),
        in_specs=[seqs(lt, CONV_DIM), seqs(lt, D_INNER), seqs(SSM_HEADS, ls),
                  seqs(SUBLANES, CONV_DIM), seqs(D_INNER, SSM_D_STATE),
                  _full((CONV_WIDTH, CONV_DIM)), _full((1, CONV_DIM)),
                  _full((SSM_HEADS, ls)), _full((SSM_HEADS, ls)),
                  _full((1, D_INNER)), _full((1, D_INNER)),
                  _full((2 * LANES, 2 * D_INNER)), _full((LANES, D_INNER))],
        out_specs=[seqs(lt, D_INNER), seqs(D_INNER, SSM_D_STATE), seqs(SUBLANES, CONV_DIM)],
        out_shape=[jax.ShapeDtypeStruct((bsz, seq, D_INNER), F32),
                   jax.ShapeDtypeStruct((bsz, D_INNER, SSM_D_STATE), F32),
                   jax.ShapeDtypeStruct((bsz, SUBLANES, CONV_DIM), F32)],
        scratch_shapes=[pltpu.VMEM((nseq, SUBLANES + lt, CONV_DIM), F32),
                        pltpu.VMEM((nseq, ls, GROUP_WIDTH), F32), pltpu.VMEM((nseq, ls, D_INNER), F32)],
        compiler_params=_params("parallel"),
        name="ssd_decode",
    )(xbc, z, dtt, conv0, ssm0,
      p["conv_w"], p["conv_b"].reshape(1, CONV_DIM),
      head_rows(p["dt_bias"]), head_rows(p["a_log"]),
      jnp.repeat(p["d_skip"].astype(F32), SSM_HEAD_DIM).reshape(1, D_INNER),
      p["gate_norm"].reshape(1, D_INNER), _spread_matrix(), _head_matrix())
    return (y.reshape(bsz * seq, D_INNER),
            ssm.reshape(bsz, SSM_HEADS, SSM_HEAD_DIM, SSM_D_STATE),
            conv[:, SUBLANES - (CONV_WIDTH - 1):, :])


def _mixer0(x, conv0, ssm0, w, bsz, seq):
    rb = MIX_ROWS
    nsteps = seq // rb
    p = w["ssm"]
    conv0 = jnp.pad(conv0, ((0, 0), (SUBLANES - (CONV_WIDTH - 1), 0), (0, 0)))
    ssm0 = ssm0.reshape(bsz, D_INNER, SSM_D_STATE)

    def head_rows(v):
        return jnp.broadcast_to(v.astype(F32)[:, None], (SSM_HEADS, SSD_CHUNK))

    row_spec = pl.BlockSpec((rb, D_MODEL), lambda b, c: (b * nsteps + c, 0))
    out, ssm, conv = pl.pallas_call(
        functools.partial(_mixer0_kernel, rb=rb, nsteps=nsteps),
        grid=(bsz, nsteps),
        in_specs=[row_spec,
                  _full((1, D_MODEL)), _full((D_MODEL, D_INNER)), _full((D_MODEL, CONV_DIM)),
                  _full((SSM_HEADS, D_MODEL)), _full((D_INNER, D_MODEL)),
                  pl.BlockSpec((None, SUBLANES, CONV_DIM), lambda b, c: (b, 0, 0)),
                  pl.BlockSpec((None, D_INNER, SSM_D_STATE), lambda b, c: (b, 0, 0)),
                  _full((CONV_WIDTH, CONV_DIM)), _full((1, CONV_DIM)),
                  _full((SSM_HEADS, SSD_CHUNK)), _full((SSM_HEADS, SSD_CHUNK)),
                  _full((1, D_INNER)), _full((1, D_INNER)),
                  _full((2 * LANES, 2 * D_INNER))],
        out_specs=[row_spec,
                   pl.BlockSpec((None, D_INNER, SSM_D_STATE), lambda b, c: (b, 0, 0)),
                   pl.BlockSpec((None, SUBLANES, CONV_DIM), lambda b, c: (b, 0, 0))],
        out_shape=[jax.ShapeDtypeStruct((bsz * seq, D_MODEL), F32),
                   jax.ShapeDtypeStruct((bsz, D_INNER, SSM_D_STATE), F32),
                   jax.ShapeDtypeStruct((bsz, SUBLANES, CONV_DIM), F32)],
        scratch_shapes=[pltpu.VMEM((SUBLANES + rb, CONV_DIM), F32),
                        pltpu.VMEM((D_INNER, SSM_D_STATE), F32),
                        pltpu.VMEM((rb, D_INNER), F32),
                        pltpu.VMEM((rb, D_INNER), BF16)],
        compiler_params=_params("parallel", "arbitrary"),
        name="mixer0",
    )(x, w["mix_norm"][0].reshape(1, D_MODEL), w["ssm_w_z"], w["ssm_w_xbc"], w["ssm_w_dtt"],
      w["ssm_w_out"], conv0, ssm0,
      p["conv_w"], p["conv_b"].reshape(1, CONV_DIM),
      head_rows(p["dt_bias"]), head_rows(p["a_log"]),
      jnp.repeat(p["d_skip"].astype(F32), SSM_HEAD_DIM).reshape(1, D_INNER),
      p["gate_norm"].reshape(1, D_INNER), _spread_matrix())
    return (out, ssm.reshape(bsz, SSM_HEADS, SSM_HEAD_DIM, SSM_D_STATE),
            conv[:, SUBLANES - (CONV_WIDTH - 1):, :])


def _alibi_slope(head):
    return 2.0 ** (-8.0 * (head + 1) / N_Q_HEADS)


def _kv_heads_low(kv):
    low_half = lax.broadcasted_iota(jnp.int32, (kv.shape[0], LANES), 1) < ATT_HEAD_DIM
    out = []
    for j in range(N_KV_HEADS):
        tile = kv[:, (j // 2) * LANES:(j // 2 + 1) * LANES]
        if j % 2 == 0:
            out.append(jnp.where(low_half, tile, 0.0).astype(BF16))
        else:
            out.append(pltpu.roll(jnp.where(low_half, 0.0, tile), ATT_HEAD_DIM, 1).astype(BF16))
    return out


def _attn_block(q, k_heads, v_heads, sink_ref, first_key_pos, tq):
    rows = lax.broadcasted_iota(jnp.int32, (Q_PER_KV * tq, WINDOW), 0)
    slot = lax.broadcasted_iota(jnp.int32, (Q_PER_KV * tq, WINDOW), 1)
    t = rows & (tq - 1)
    sub = lax.shift_right_logical(rows, int(math.log2(tq)))
    from_prev = slot > t
    distf = jnp.where(from_prev, WINDOW + t - slot, t - slot).astype(F32)
    valid = jnp.logical_not(from_prev) | (first_key_pos >= 0)

    def pick(choices):
        return jnp.where(sub == 0, choices[0], jnp.where(sub == 1, choices[1],
                                                         jnp.where(sub == 2, choices[2], choices[3])))

    out = []
    for j in range(N_KV_HEADS):
        q0 = q[:, (2 * j) * LANES:(2 * j + 1) * LANES] * (ATT_HEAD_DIM ** -0.5)
        q1 = q[:, (2 * j + 1) * LANES:(2 * j + 2) * LANES] * (ATT_HEAD_DIM ** -0.5)
        qs = jnp.concatenate([q0, pltpu.roll(q0, ATT_HEAD_DIM, 1),
                              q1, pltpu.roll(q1, ATT_HEAD_DIM, 1)], axis=0)
        s2 = lax.dot_general(qs.astype(BF16), k_heads[j], _NT, preferred_element_type=F32)
        s = jnp.where(from_prev, s2[:, :WINDOW], s2[:, WINDOW:])
        slope = pick([_alibi_slope(Q_PER_KV * j + g) for g in range(Q_PER_KV)])
        sink = pick([sink_ref[Q_PER_KV * j + g] for g in range(Q_PER_KV)])[:, 0:1]
        s = jnp.where(valid, s - slope * distf, -jnp.inf)
        mx = jnp.maximum(jnp.max(s, axis=-1, keepdims=True), sink)
        e = jnp.exp(s - mx)
        den = jnp.sum(e, axis=-1, keepdims=True) + jnp.exp(sink - mx)
        prob = e * (1.0 / den)
        prob2 = jnp.concatenate([jnp.where(from_prev, prob, 0.0), jnp.where(from_prev, 0.0, prob)],
                                axis=1).astype(BF16)
        o = jnp.dot(prob2, v_heads[j], preferred_element_type=F32)
        out.append(o[0:tq] + pltpu.roll(o[tq:2 * tq], ATT_HEAD_DIM, 1))
        out.append(o[2 * tq:3 * tq] + pltpu.roll(o[3 * tq:4 * tq], ATT_HEAD_DIM, 1))
    return out


def _attn_decode_kernel(q_ref, kp_ref, kc_ref, vp_ref, vc_ref, bias_ref, sink_ref,
                        o_ref, kwin_ref, vwin_ref, k_scr, v_scr, s_scr, *, tq, nseq):
    k_scr[...] = jnp.zeros(k_scr.shape, F32)
    v_scr[...] = jnp.zeros(v_scr.shape, F32)
    low_half = lax.broadcasted_iota(jnp.int32, (tq, LANES), 1) < ATT_HEAD_DIM
    zeros = jnp.zeros((tq, LANES), F32)

    def half(tile, upper):
        return jnp.where(low_half, 0.0, tile) if upper else jnp.where(low_half, tile, 0.0)

    for i in range(nseq):
        k_scr[i, 0:WINDOW, :] = kp_ref[i]
        v_scr[i, 0:WINDOW, :] = vp_ref[i]
        k_scr[i, WINDOW:WINDOW + tq, :] = kc_ref[i]
        v_scr[i, WINDOW:WINDOW + tq, :] = vc_ref[i]
        kwin_ref[i] = k_scr[i, tq:tq + WINDOW, :]
        vwin_ref[i] = v_scr[i, tq:tq + WINDOW, :]
        q = q_ref[i] * (ATT_HEAD_DIM ** -0.5)
        blocks = []
        for h in range(N_Q_HEADS):
            j = h // Q_PER_KV
            piece = half(q[:, (h // 2) * LANES:(h // 2 + 1) * LANES], h % 2 == 1)
            if h % 2 != j % 2:
                piece = pltpu.roll(piece, ATT_HEAD_DIM, 1)
            blocks.append(jnp.concatenate([piece, zeros] if j < 2 else [zeros, piece], axis=1))
        q_all = jnp.concatenate(blocks, axis=0).astype(BF16)
        s_scr[i] = lax.dot_general(q_all, k_scr[i].astype(BF16), _NT,
                                   preferred_element_type=F32)
        s = s_scr[i] - bias_ref[...]
        sink = jnp.concatenate([jnp.full((tq, 1), sink_ref[h], F32) for h in range(N_Q_HEADS)], axis=0)
        mx = jnp.maximum(jnp.max(s, axis=-1, keepdims=True), sink)
        e = jnp.exp(s - mx)
        den = jnp.sum(e, axis=-1, keepdims=True) + jnp.exp(sink - mx)
        prob = (e * (1.0 / den)).astype(BF16)
        o_all = jnp.dot(prob, v_scr[i].astype(BF16), preferred_element_type=F32)
        for m in range(N_Q_HEADS // 2):
            parts = []
            for h in (2 * m, 2 * m + 1):
                j = h // Q_PER_KV
                piece = half(o_all[h * tq:(h + 1) * tq, (j // 2) * LANES:(j // 2 + 1) * LANES],
                             j % 2 == 1)
                if h % 2 != j % 2:
                    piece = pltpu.roll(piece, ATT_HEAD_DIM, 1)
                parts.append(piece)
            o_ref[i, :, m * LANES:(m + 1) * LANES] = parts[0] + parts[1]


def _mixer1_kernel(sink_ref, x_ref, xkv_ref, gq_ref, gkv_ref, wq_ref, wk_ref, wv_ref, wo_ref,
                   o_ref, kwin_ref, vwin_ref, k_scr, v_scr, att_scr, q_scr, *, rb):
    c = pl.program_id(1)

    @pl.when(c == 0)
    def _():
        k_scr[0:WINDOW, :] = jnp.zeros((WINDOW, KV_WIDTH), F32)
        v_scr[0:WINDOW, :] = jnp.zeros((WINDOW, KV_WIDTH), F32)

    hkv = _rms(xkv_ref[...], gkv_ref[...]).astype(BF16)
    k_scr[WINDOW:WINDOW + rb, :] = jnp.dot(hkv, wk_ref[...], preferred_element_type=F32)
    v_scr[WINDOW:WINDOW + rb, :] = jnp.dot(hkv, wv_ref[...], preferred_element_type=F32)
    q_scr[...] = jnp.dot(_rms(x_ref[...], gq_ref[...]).astype(BF16), wq_ref[...],
                         preferred_element_type=F32)
    k_heads = _kv_heads_low(k_scr[...])
    v_heads = _kv_heads_low(v_scr[...])
    for blk in range(rb // WINDOW):
        lo = blk * WINDOW
        tiles = _attn_block(q_scr[lo:lo + WINDOW, :], [k[lo:lo + 2 * WINDOW] for k in k_heads],
                            [v[lo:lo + 2 * WINDOW] for v in v_heads], sink_ref,
                            c * rb + lo - WINDOW, WINDOW)
        for m, tile in enumerate(tiles):
            att_scr[lo:lo + WINDOW, m * LANES:(m + 1) * LANES] = tile.astype(BF16)
    o_ref[...] = x_ref[...] + jnp.dot(att_scr[...], wo_ref[...], preferred_element_type=F32)
    k_last = k_scr[rb:rb + WINDOW, :]
    v_last = v_scr[rb:rb + WINDOW, :]
    kwin_ref[...] = k_last
    vwin_ref[...] = v_last
    k_scr[0:WINDOW, :] = k_last
    v_scr[0:WINDOW, :] = v_last


def _mixer1(x, xkv, w, bsz, seq):
    rb = MIX_ROWS
    nsteps = seq // rb
    row_spec = pl.BlockSpec((rb, D_MODEL), lambda b, c: (b * nsteps + c, 0))
    win_spec = pl.BlockSpec((None, WINDOW, KV_WIDTH), lambda b, c: (b, 0, 0))
    return pl.pallas_call(
        functools.partial(_mixer1_kernel, rb=rb),
        grid=(bsz, nsteps),
        in_specs=[pl.BlockSpec(memory_space=pltpu.SMEM), row_spec, row_spec,
                  _full((1, D_MODEL)), _full((1, D_MODEL)),
                  _full((D_MODEL, D_MODEL)), _full((D_MODEL, KV_WIDTH)), _full((D_MODEL, KV_WIDTH)),
                  _full((D_MODEL, D_MODEL))],
        out_specs=[row_spec, win_spec, win_spec],
        out_shape=[jax.ShapeDtypeStruct((bsz * seq, D_MODEL), F32),
                   jax.ShapeDtypeStruct((bsz, WINDOW, KV_WIDTH), F32),
                   jax.ShapeDtypeStruct((bsz, WINDOW, KV_WIDTH), F32)],
        scratch_shapes=[pltpu.VMEM((WINDOW + rb, KV_WIDTH), F32), pltpu.VMEM((WINDOW + rb, KV_WIDTH), F32),
                        pltpu.VMEM((rb, D_MODEL), BF16), pltpu.VMEM((rb, D_MODEL), F32)],
        compiler_params=_params("parallel", "arbitrary"),
        name="mixer1",
    )(w["attn_sinks"].astype(F32), x, xkv, w["mix_norm"][1].reshape(1, D_MODEL),
      w["kv_norm"].reshape(1, D_MODEL), w["attn_w_q"], w["w_k"], w["w_v"], w["attn_w_o"])


def _attention_decode(q, k_cache, k_new, v_cache, v_new, sinks):
    bsz, seq = q.shape[0], q.shape[1]
    nseq = DECODE_SEQS

    def seqs(*dims):
        return pl.BlockSpec((nseq,) + dims, lambda b: (b,) + (0,) * len(dims))

    t = np.tile(np.arange(seq), N_Q_HEADS)[:, None]
    kj = np.arange(2 * WINDOW)[None, :]
    dist = WINDOW + t - kj
    valid = (dist >= 0) & (dist < WINDOW) & (PAST_LEN - WINDOW + kj >= 0)
    slope = np.repeat([_alibi_slope(h) for h in range(N_Q_HEADS)], seq)[:, None]
    bias = jnp.asarray(np.where(valid, slope * dist, np.inf), dtype=F32)

    return pl.pallas_call(
        functools.partial(_attn_decode_kernel, tq=seq, nseq=nseq),
        grid=(bsz // nseq,),
        in_specs=[seqs(seq, D_MODEL),
                  seqs(WINDOW, KV_WIDTH), seqs(seq, KV_WIDTH), seqs(WINDOW, KV_WIDTH), seqs(seq, KV_WIDTH),
                  _full((N_Q_HEADS * seq, 2 * WINDOW)), pl.BlockSpec(memory_space=pltpu.SMEM)],
        out_specs=[seqs(seq, D_MODEL), seqs(WINDOW, KV_WIDTH), seqs(WINDOW, KV_WIDTH)],
        out_shape=[jax.ShapeDtypeStruct((bsz, seq, D_MODEL), F32),
                   jax.ShapeDtypeStruct((bsz, WINDOW, KV_WIDTH), F32),
                   jax.ShapeDtypeStruct((bsz, WINDOW, KV_WIDTH), F32)],
        scratch_shapes=[pltpu.VMEM((nseq, 2 * WINDOW, KV_WIDTH), F32),
                        pltpu.VMEM((nseq, 2 * WINDOW, KV_WIDTH), F32),
                        pltpu.VMEM((nseq, N_Q_HEADS * seq, 2 * WINDOW), F32)],
        compiler_params=_params("parallel"),
        name="swa_decode",
    )(q, k_cache, k_new, v_cache, v_new, bias, sinks.astype(F32))


def _trunk(xp, xs, ssm_p, conv_p, ssm_s, conv_s, k_buf, v_buf, w):
    bp, lp = xp.shape[0], xp.shape[1]
    bs, ls = xs.shape[0], xs.shape[1]
    xp = xp.reshape(bp * lp, D_MODEL)
    xs = xs.reshape(bs * ls, D_MODEL)

    xp, xs = _ffn(xp, xs, w["ffn1_norm"], w["ffn1_w_gu"], w["ffn1_w_down"], 0)
    xp, ssm_p, conv_p = _mixer0(xp, conv_p, ssm_p, w, bp, lp)
    z, xbc, dt = _norm_proj(xs, w["mix_norm"][0], [w["ssm_w_z"], w["ssm_w_xbc"], w["ssm_w_dt"]])
    y, ssm_s, conv_s = _ssd_decode(xbc, z, dt, conv_s, ssm_s, w["ssm"], bs, ls)
    xs = _proj_res(y, w["ssm_w_out"], xs)
    xp_kv, xs_kv = _ffn(xp, xs, w["ffn2_norm"], w["ffn2_w_gu"], w["ffn2_w_down"], 0)

    xp, xs = _ffn(xp_kv, xs_kv, w["ffn1_norm"], w["ffn1_w_gu"], w["ffn1_w_down"], 1)
    xp, kw_p, vw_p = _mixer1(xp, xp_kv, w, bp, lp)
    k_new, v_new = _norm_proj(xs_kv, w["kv_norm"], [w["w_k"], w["w_v"]])
    (q,) = _norm_proj(xs, w["mix_norm"][1], [w["attn_w_q"]])
    o, kw_s, vw_s = _attention_decode(
        q.reshape(bs, ls, D_MODEL), k_buf.reshape(bs, WINDOW, KV_WIDTH), k_new.reshape(bs, ls, KV_WIDTH),
        v_buf.reshape(bs, WINDOW, KV_WIDTH), v_new.reshape(bs, ls, KV_WIDTH), w["attn_sinks"])
    xs = _proj_res(o.reshape(bs * ls, D_MODEL), w["attn_w_o"], xs)
    yp, ys = _ffn(xp, xs, w["ffn2_norm"], w["ffn2_w_gu"], w["ffn2_w_down"], 1, fg=w["final_norm"])

    def heads(t, bsz):
        return t.reshape(bsz, WINDOW, N_KV_HEADS, ATT_HEAD_DIM)

    return (yp.reshape(bp, lp, D_MODEL), ys.reshape(bs, ls, D_MODEL), ssm_p[None], conv_p[None],
            heads(kw_p, bp), heads(vw_p, bp), ssm_s[None], conv_s[None], heads(kw_s, bs), heads(vw_s, bs))


def kernel(x_prompt, x_sample, state_ssm, state_conv, cache_k_win, cache_v_win,
           ffn1_norm, ffn1_w_gu, ffn1_w_down, mix_norm, ffn2_norm, ffn2_w_gu, ffn2_w_down,
           ssm_w_in, ssm_conv_w, ssm_conv_b, ssm_dt_bias, ssm_a_log, ssm_d, ssm_gate_norm, ssm_w_out,
           kv_norm, w_kv, attn_w_q, attn_sinks, attn_w_o, final_norm):
    w_in = ssm_w_in[0]
    w_dt = jnp.pad(w_in[:, D_INNER + CONV_DIM:], ((0, 0), (0, LANES - SSM_HEADS)))
    w = dict(
        ffn1_norm=ffn1_norm, ffn2_norm=ffn2_norm, mix_norm=mix_norm, kv_norm=kv_norm,
        final_norm=final_norm, attn_sinks=attn_sinks[0],
        ffn1_w_gu=ffn1_w_gu.astype(BF16), ffn1_w_down=ffn1_w_down.astype(BF16),
        ffn2_w_gu=ffn2_w_gu.astype(BF16), ffn2_w_down=ffn2_w_down.astype(BF16),
        ssm_w_z=w_in[:, :D_INNER].astype(BF16),
        ssm_w_xbc=w_in[:, D_INNER:D_INNER + CONV_DIM].astype(BF16),
        ssm_w_dt=w_dt.astype(BF16),
        ssm_w_dtt=w_in[:, D_INNER + CONV_DIM:].T.astype(BF16),
        ssm_w_out=ssm_w_out[0].astype(BF16),
        w_k=w_kv[:, :KV_WIDTH].astype(BF16), w_v=w_kv[:, KV_WIDTH:].astype(BF16),
        attn_w_q=attn_w_q[0].astype(BF16), attn_w_o=attn_w_o[0].astype(BF16),
        ssm=dict(conv_w=ssm_conv_w[0], conv_b=ssm_conv_b[0], dt_bias=ssm_dt_bias[0],
                 a_log=ssm_a_log[0], d_skip=ssm_d[0], gate_norm=ssm_gate_norm[0]),
    )
    bp = x_prompt.shape[0]
    ssm0 = jnp.zeros((bp, SSM_HEADS, SSM_HEAD_DIM, SSM_D_STATE), F32)
    conv0 = jnp.zeros((bp, CONV_WIDTH - 1, CONV_DIM), F32)
    return _trunk(x_prompt, x_sample, ssm0, conv0, state_ssm[0], state_conv[0],
                  cache_k_win, cache_v_win, w)
```
